```python
import jax
import jax.numpy as jnp
from jax import lax
import numpy as np

D_MODEL = 1024
BATCH = 16
SEQ = 2048
DEPTH = 1
DEC_BATCH = 16
DEC_SEQ = 64
PAST_LEN = 4096

CHUNK = 64
N_META = 16
H_R = 8
HEAD_R = 64
D_R = H_R * HEAD_R
LORA_W = 64
LORA_A = 64
LORA_G = 128
RWKV_IN = 3 * D_R + LORA_W + LORA_A + LORA_G
RWKV_SPLITS = (D_R, 2 * D_R, 3 * D_R, 3 * D_R + LORA_W, 3 * D_R + LORA_W + LORA_A)
LNX_EPS = 64e-5
H_F = 8
HEAD_F = 64
D_F = H_F * HEAD_F
FOX_IN = 3 * D_F + H_F
Q_BLOCK = 128
D_IN = RWKV_IN + FOX_IN
D_MIX = D_R + D_F
N_KEYS = 128
N_EXPERTS = N_KEYS * N_KEYS
PEER_HEADS = 8
PEER_TOPK = 16
D_KEY = 256
D_KEY_HALF = D_KEY // 2
PEER_BLOCK = 256
ALPHA = (2.0 * DEPTH) ** 0.25
BETA = (8.0 * DEPTH) ** -0.25
LN_EPS = 1e-5

kernel_name = 'rwkv7_fox_peer_streaming_step'


def _normal(key, shape, scale):
    return jax.random.normal(key, shape, jnp.float32) * scale


def _layernorm(x, g, b):
    xf = x.astype(jnp.float32)
    xc = xf - jnp.mean(xf, axis=-1, keepdims=True)
    var = jnp.mean(xc * xc, axis=-1, keepdims=True)
    return (xc * lax.rsqrt(var + LN_EPS) * g.astype(jnp.float32) + b.astype(jnp.float32)).astype(x.dtype)


def _wkv7_scan(r, w, k, v, a, b, s0):
    def step(S, inp):
        r_t, w_t, k_t, v_t, a_t, b_t = inp
        sa = jnp.einsum('bhvk,bhk->bhv', S, a_t)
        S = S * w_t[:, :, None, :] + sa[..., None] * b_t[:, :, None, :] + v_t[..., None] * k_t[:, :, None, :]
        return S, jnp.einsum('bhvk,bhk->bhv', S, r_t)
    xs = tuple(jnp.swapaxes(t, 0, 1) for t in (r, w, k, v, a, b))
    s_final, ys = lax.scan(step, s0.astype(jnp.float32), xs)
    return jnp.swapaxes(ys, 0, 1), s_final


def _rwkv7(pr, shift_prev, wkv_prev, rwkv_mix, rwkv_w0, rwkv_w2, rwkv_a0, rwkv_a2, rwkv_g2,
           rwkv_k_k, rwkv_k_a, rwkv_r_k, rwkv_lnx_g, rwkv_lnx_b):
    B, T, _ = pr.shape
    f32 = jnp.float32
    prf = pr.astype(f32)
    prev = jnp.concatenate([shift_prev.astype(f32), prf[:, :-1]], axis=1)
    xm = prf + (prev - prf) * rwkv_mix.astype(f32)
    r, k, v, xw, xa, xg = jnp.split(xm, RWKV_SPLITS, axis=-1)
    w = -jax.nn.softplus(-(rwkv_w0 + jnp.tanh(xw) @ rwkv_w2)) - 0.5
    a = jax.nn.sigmoid(rwkv_a0 + xa @ rwkv_a2)
    g = jax.nn.sigmoid(xg) @ rwkv_g2

    def heads(t):
        return t.reshape(B, T, H_R, HEAD_R)

    kk = heads(k * rwkv_k_k)
    kk = kk / jnp.maximum(jnp.linalg.norm(kk, axis=-1, keepdims=True), 1e-12)
    k = heads(k * (1.0 + (a - 1.0) * rwkv_k_a))
    r, v, a = heads(r), heads(v), heads(a)
    decay = jnp.exp(-jnp.exp(heads(w)))
    y, wkv_new = _wkv7_scan(r, decay, k, v, -kk, kk * a, wkv_prev)
    yc = y - jnp.mean(y, axis=-1, keepdims=True)
    y = yc * lax.rsqrt(jnp.mean(yc * yc, axis=-1, keepdims=True) + LNX_EPS)
    y = y.reshape(B, T, D_R) * rwkv_lnx_g + rwkv_lnx_b
    bonus = jnp.sum(r * k * rwkv_r_k, axis=-1, keepdims=True) * v
    y = (y + bonus.reshape(B, T, D_R)) * g
    return y.astype(pr.dtype), pr[:, -1:], wkv_new


def _fox_attend(q, k, v, logf, q_offset):
    T = q.shape[1]
    c = jnp.swapaxes(jnp.cumsum(logf, axis=1), 1, 2)
    scale = HEAD_F ** -0.5
    outs = []
    for start in range(0, T, Q_BLOCK):
        stop = min(start + Q_BLOCK, T)
        k_end = q_offset + stop
        s = jnp.einsum('bqhd,bkhd->bhqk', q[:, start:stop], k[:, :k_end]).astype(jnp.float32) * scale
        s = s + c[:, :, q_offset + start:k_end, None] - c[:, :, None, :k_end]
        q_pos = jnp.arange(q_offset + start, k_end)
        mask = jnp.arange(k_end)[None, :] <= q_pos[:, None]
        s = jnp.where(mask, s, -jnp.inf)
        p = jax.nn.softmax(s, axis=-1).astype(v.dtype)
        outs.append(jnp.einsum('bhqk,bkhd->bqhd', p, v[:, :k_end]))
    return jnp.concatenate(outs, axis=1)


def _fox(pf, k_past, v_past, logf_past, fox_b_f, fox_norm_g):
    B, T, _ = pf.shape
    q, k, v, f_logit = jnp.split(pf, (D_F, 2 * D_F, 3 * D_F), axis=-1)
    q = q.reshape(B, T, H_F, HEAD_F)
    k = k.reshape(B, T, H_F, HEAD_F)
    v = v.reshape(B, T, H_F, HEAD_F)
    logf = jax.nn.log_sigmoid((f_logit + fox_b_f).astype(jnp.float32))
    k_all = jnp.concatenate([k_past.astype(k.dtype), k], axis=1)
    v_all = jnp.concatenate([v_past.astype(v.dtype), v], axis=1)
    logf_all = jnp.concatenate([logf_past.astype(jnp.float32), logf], axis=1)
    o = _fox_attend(q, k_all, v_all, logf_all, k_past.shape[1]).astype(jnp.float32)
    o = o * lax.rsqrt(jnp.mean(o * o, axis=-1, keepdims=True) + 1e-6)
    o = o.reshape(B, T, D_F) * fox_norm_g
    return o.astype(pf.dtype), k, v, logf


def _peer(h, peer_w_q, peer_keys1, peer_keys2, peer_u, peer_v):
    B, T, D = h.shape
    f32 = jnp.float32
    n = B * T
    n_pad = (-n) % PEER_BLOCK
    xb = jnp.pad(h.reshape(n, D), ((0, n_pad), (0, 0))).reshape(-1, PEER_BLOCK, D)
    keys1 = peer_keys1.astype(f32)
    keys2 = peer_keys2.astype(f32)

    def block(xt):
        q = (xt @ peer_w_q).reshape(PEER_BLOCK, PEER_HEADS, 2, D_KEY_HALF).astype(f32)
        s1 = jnp.einsum('thd,nd->thn', q[:, :, 0], keys1)
        s2 = jnp.einsum('thd,nd->thn', q[:, :, 1], keys2)
        v1, i1 = lax.top_k(s1, PEER_TOPK)
        v2, i2 = lax.top_k(s2, PEER_TOPK)
        cand_s = (v1[..., :, None] + v2[..., None, :]).reshape(PEER_BLOCK, PEER_HEADS, PEER_TOPK * PEER_TOPK)
        cand_i = (i1[..., :, None] * N_KEYS + i2[..., None, :]).reshape(PEER_BLOCK, PEER_HEADS, PEER_TOPK * PEER_TOPK)
        best_s, pos = lax.top_k(cand_s, PEER_TOPK)
        idx = jnp.take_along_axis(cand_i, pos, axis=-1)
        gate = jax.nn.softmax(best_s, axis=-1)
        act = jax.nn.gelu(jnp.einsum('thkd,td->thk', peer_u[idx], xt).astype(f32), approximate=False)
        return jnp.einsum('thk,thkd->td', (gate * act).astype(peer_v.dtype), peer_v[idx])

    out = lax.map(block, xb).reshape(-1, D)[:n]
    return out.reshape(B, T, D).astype(h.dtype)


def _layer(h, shift_prev, wkv_prev, k_past, v_past, logf_past, *, w_in, rwkv_mix, rwkv_w0, rwkv_w2,
           rwkv_a0, rwkv_a2, rwkv_g2, rwkv_k_k, rwkv_k_a, rwkv_r_k, rwkv_lnx_g, rwkv_lnx_b, fox_b_f,
           fox_norm_g, w_out, ln1_g, ln1_b, peer_w_q, peer_keys1, peer_keys2, peer_u, peer_v, ln2_g, ln2_b):
    p = h @ w_in
    y_r, shift_new, wkv_new = _rwkv7(p[..., :RWKV_IN], shift_prev, wkv_prev, rwkv_mix, rwkv_w0, rwkv_w2,
                                     rwkv_a0, rwkv_a2, rwkv_g2, rwkv_k_k, rwkv_k_a, rwkv_r_k,
                                     rwkv_lnx_g, rwkv_lnx_b)
    y_f, k_new, v_new, logf_new = _fox(p[..., RWKV_IN:], k_past, v_past, logf_past, fox_b_f, fox_norm_g)
    mix = jnp.concatenate([y_r, y_f], axis=-1) @ w_out
    h = _layernorm(ALPHA * h + mix, ln1_g, ln1_b)
    h = _layernorm(ALPHA * h + _peer(h, peer_w_q, peer_keys1, peer_keys2, peer_u, peer_v), ln2_g, ln2_b)
    return h, (shift_new, wkv_new, k_new, v_new, logf_new)


def setup_inputs(seed: int = 0) -> dict:
    key = jax.random.key(seed)
    k = jax.random.split(key, 40)
    L = DEPTH
    col_scale = np.ones((D_IN,), np.float32)
    col_scale[2 * D_R:3 * D_R] = BETA
    col_scale[RWKV_IN + 2 * D_F:RWKV_IN + 3 * D_F] = BETA
    return {
        'x_prompt': _normal(k[0], (BATCH, SEQ, D_MODEL), 1.0),
        'x_sample': _normal(k[1], (DEC_BATCH, DEC_SEQ, D_MODEL), 1.0),
        'state_rwkv_shift': _normal(k[2], (L, DEC_BATCH, 1, RWKV_IN), 1.0),
        'state_rwkv_wkv': _normal(k[3], (L, DEC_BATCH, H_R, HEAD_R, HEAD_R), 0.3),
        'cache_fox_k': _normal(k[4], (L, DEC_BATCH, PAST_LEN, H_F, HEAD_F), 1.0),
        'cache_fox_v': _normal(k[5], (L, DEC_BATCH, PAST_LEN, H_F, HEAD_F), BETA),
        'cache_fox_logf': jax.nn.log_sigmoid(_normal(k[6], (L, DEC_BATCH, PAST_LEN, H_F), 0.5) + 2.0),
        'meta_tokens': _normal(k[7], (N_META, D_MODEL), 1.0),
        'ln_in_g': 1.0 + _normal(k[8], (D_MODEL,), 0.05),
        'ln_in_b': _normal(k[9], (D_MODEL,), 0.02),
        'w_in': _normal(k[10], (L, D_MODEL, D_IN), D_MODEL ** -0.5) * jnp.asarray(col_scale),
        'rwkv_mix': jax.random.uniform(k[11], (L, RWKV_IN), jnp.float32),
        'rwkv_w0': jax.random.uniform(k[12], (L, D_R), jnp.float32, -6.0, -1.0),
        'rwkv_w2': _normal(k[13], (L, LORA_W, D_R), 0.1 * LORA_W ** -0.5),
        'rwkv_a0': _normal(k[14], (L, D_R), 0.1),
        'rwkv_a2': _normal(k[15], (L, LORA_A, D_R), 0.3 * LORA_A ** -0.5),
        'rwkv_g2': _normal(k[16], (L, LORA_G, D_R), LORA_G ** -0.5),
        'rwkv_k_k': 0.85 + _normal(k[17], (L, D_R), 0.05),
        'rwkv_k_a': 1.0 + _normal(k[18], (L, D_R), 0.05),
        'rwkv_r_k': _normal(k[19], (L, H_R, HEAD_R), 0.1),
        'rwkv_lnx_g': 1.0 + _normal(k[20], (L, D_R), 0.05),
        'rwkv_lnx_b': _normal(k[21], (L, D_R), 0.02),
        'fox_b_f': 2.0 + _normal(k[22], (L, H_F), 0.5),
        'fox_norm_g': 1.0 + _normal(k[23], (L, D_F), 0.05),
        'w_out': _normal(k[24], (L, D_MIX, D_MODEL), BETA * D_MIX ** -0.5),
        'ln1_g': 1.0 + _normal(k[25], (L, D_MODEL), 0.05),
        'ln1_b': _normal(k[26], (L, D_MODEL), 0.02),
        'peer_w_q': _normal(k[27], (L, D_MODEL, PEER_HEADS * D_KEY), D_MODEL ** -0.5),
        'peer_keys1': _normal(k[28], (L, N_KEYS, D_KEY_HALF), D_KEY_HALF ** -0.5),
        'peer_keys2': _normal(k[29], (L, N_KEYS, D_KEY_HALF), D_KEY_HALF ** -0.5),
        'peer_u': _normal(k[30], (L, N_EXPERTS, D_MODEL), D_MODEL ** -0.5),
        'peer_v': _normal(k[31], (L, N_EXPERTS, D_MODEL), BETA),
        'ln2_g': 1.0 + _normal(k[32], (L, D_MODEL), 0.05),
        'ln2_b': _normal(k[33], (L, D_MODEL), 0.02),
    }


def reference(x_prompt, x_sample, state_rwkv_shift, state_rwkv_wkv, cache_fox_k, cache_fox_v, cache_fox_logf,
              meta_tokens, ln_in_g, ln_in_b, w_in, rwkv_mix, rwkv_w0, rwkv_w2, rwkv_a0, rwkv_a2, rwkv_g2,
              rwkv_k_k, rwkv_k_a, rwkv_r_k, rwkv_lnx_g, rwkv_lnx_b, fox_b_f, fox_norm_g, w_out, ln1_g, ln1_b,
              peer_w_q, peer_keys1, peer_keys2, peer_u, peer_v, ln2_g, ln2_b):
    B = x_prompt.shape[0]
    dt = x_prompt.dtype
    meta = jnp.broadcast_to(meta_tokens.astype(dt)[None], (B, N_META, D_MODEL))
    hp = _layernorm(jnp.concatenate([meta, x_prompt], axis=1), ln_in_g, ln_in_b)
    hs = _layernorm(x_sample, ln_in_g, ln_in_b)
    zero_shift = jnp.zeros((B, 1, RWKV_IN), dt)
    zero_wkv = jnp.zeros((B, H_R, HEAD_R, HEAD_R), jnp.float32)
    no_kv = jnp.zeros((B, 0, H_F, HEAD_F), dt)
    no_logf = jnp.zeros((B, 0, H_F), jnp.float32)
    p_states = []
    s_states = []
    for l in range(DEPTH):
        lp = dict(w_in=w_in[l], rwkv_mix=rwkv_mix[l], rwkv_w0=rwkv_w0[l], rwkv_w2=rwkv_w2[l],
                  rwkv_a0=rwkv_a0[l], rwkv_a2=rwkv_a2[l], rwkv_g2=rwkv_g2[l], rwkv_k_k=rwkv_k_k[l],
                  rwkv_k_a=rwkv_k_a[l], rwkv_r_k=rwkv_r_k[l], rwkv_lnx_g=rwkv_lnx_g[l],
                  rwkv_lnx_b=rwkv_lnx_b[l], fox_b_f=fox_b_f[l], fox_norm_g=fox_norm_g[l], w_out=w_out[l],
                  ln1_g=ln1_g[l], ln1_b=ln1_b[l], peer_w_q=peer_w_q[l], peer_keys1=peer_keys1[l],
                  peer_keys2=peer_keys2[l], peer_u=peer_u[l], peer_v=peer_v[l], ln2_g=ln2_g[l], ln2_b=ln2_b[l])
        hp, st_p = _layer(hp, zero_shift, zero_wkv, no_kv, no_kv, no_logf, **lp)
        hs, st_s = _layer(hs, state_rwkv_shift[l], state_rwkv_wkv[l], cache_fox_k[l], cache_fox_v[l],
                          cache_fox_logf[l], **lp)
        p_states.append(st_p)
        s_states.append(st_s)
    p_shift, p_wkv, p_k, p_v, p_logf = (jnp.stack(z) for z in zip(*p_states))
    s_shift, s_wkv, s_k, s_v, s_logf = (jnp.stack(z) for z in zip(*s_states))
    y_prompt = hp[:, N_META:]
    y_sample = hs
    return (y_prompt, y_sample, p_shift, p_wkv, p_k, p_v, p_logf, s_shift, s_wkv, s_k, s_v, s_logf)
```

```python
import functools

import jax
import jax.numpy as jnp
from jax import lax
from jax.experimental import pallas as pl
from jax.experimental.pallas import tpu as pltpu

F32 = jnp.float32
BF16 = jnp.bfloat16
I32 = jnp.int32

D_MODEL = 1024
N_HEADS = 8
HEAD = 64
D_HEADS = N_HEADS * HEAD
LORA_W, LORA_A, LORA_G = 64, 64, 128
RWKV_IN = 3 * D_HEADS + LORA_W + LORA_A + LORA_G
N_META = 16
LNX_EPS = 64e-5
LN_EPS = 1e-5
ALPHA = 2.0 ** 0.25
N_KEYS = 128
PEER_HEADS = 8
PEER_TOPK = 16
PEER_SLOTS = PEER_HEADS * PEER_TOPK
D_KEY_HALF = 128

LANES = 128
CHUNK = 64
TOKEN_BLOCK = 256
FOX_BLOCK = 256
GATHER_TOKENS = 8
VMEM_LIMIT = 48 * 1024 * 1024

HIGHEST = lax.Precision.HIGHEST


def _params(semantics):
    return pltpu.CompilerParams(dimension_semantics=semantics, vmem_limit_bytes=VMEM_LIMIT)


def _dot(a, b, precision=None):
    return jnp.dot(a, b, preferred_element_type=F32, precision=precision)


def _dot_nt(a, b, precision=None):
    return lax.dot_general(a, b, (((1,), (1,)), ((), ())), preferred_element_type=F32, precision=precision)


def _dot_tn(a, b, precision=None):
    return lax.dot_general(a, b, (((0,), (0,)), ((), ())), preferred_element_type=F32, precision=precision)


def _split3(a):
    a1 = a.astype(BF16)
    r1 = a - a1.astype(F32)
    a2 = r1.astype(BF16)
    a3 = (r1 - a2.astype(F32)).astype(BF16)
    return a1, a2, a3


def _dot_exact_rhs(a, m_bf16):
    a1, a2, a3 = _split3(a)
    return _dot(a1, m_bf16) + _dot(a2, m_bf16) + _dot(a3, m_bf16)


def _dot_exact_lhs(m_bf16, a):
    a1, a2, a3 = _split3(a)
    return _dot(m_bf16, a1) + _dot(m_bf16, a2) + _dot(m_bf16, a3)


def _layernorm(x, g, b):
    xc = x - jnp.mean(x, axis=-1, keepdims=True)
    var = jnp.mean(xc * xc, axis=-1, keepdims=True)
    return xc * lax.rsqrt(var + LN_EPS) * g + b


def _softplus(x):
    return jnp.maximum(x, 0.0) + jnp.log1p(jnp.exp(-jnp.abs(x)))


def _sigmoid(x):
    return 1.0 / (1.0 + jnp.exp(-x))


def _full(shape):
    return pl.BlockSpec(shape, lambda *_: (0,) * len(shape))


def _in_proj_body(x_ref, g_ref, b_ref, wr_ref, wqkv_ref, wf_ref, bf_ref,
                  h_ref, pr_ref, q_ref, k_ref, v_ref, lf_ref):
    h = _layernorm(x_ref[...], g_ref[...], b_ref[...])
    h_ref[...] = h
    hb = h.astype(BF16)
    pr_ref[...] = _dot(hb, wr_ref[...])
    qkv = _dot(hb, wqkv_ref[...])
    q_ref[...] = qkv[:, :D_HEADS]
    k_ref[...] = qkv[:, D_HEADS:2 * D_HEADS]
    v_ref[...] = qkv[:, 2 * D_HEADS:]
    fl = _dot(hb, wf_ref[...]) + bf_ref[...]
    lf = jnp.minimum(fl, 0.0) - jnp.log1p(jnp.exp(-jnp.abs(fl)))
    lf_ref[...] = lf[:, :N_HEADS]


def _in_proj(x, g, b, wr, wqkv, wf, bfp):
    n = x.shape[0]
    tb = TOKEN_BLOCK
    rows = lambda w: pl.BlockSpec((tb, w), lambda i: (i, 0))
    return pl.pallas_call(
        _in_proj_body,
        grid=(n // tb,),
        in_specs=[rows(D_MODEL), _full((1, D_MODEL)), _full((1, D_MODEL)), _full(wr.shape),
                  _full(wqkv.shape), _full(wf.shape), _full(bfp.shape)],
        out_specs=[rows(D_MODEL), rows(RWKV_IN), rows(D_HEADS), rows(D_HEADS), rows(D_HEADS), rows(N_HEADS)],
        out_shape=[jax.ShapeDtypeStruct((n, w), F32)
                   for w in (D_MODEL, RWKV_IN, D_HEADS, D_HEADS, D_HEADS, N_HEADS)],
        compiler_params=_params(("parallel",)),
        name="in_proj",
    )(x, g, b, wr, wqkv, wf, bfp)


def _rwkv_body(seq_len, n_chunks, pr_ref, shift0_ref, s0_ref, mix_ref, w0_ref, w2_ref, a0_ref, a2_ref,
               g2_ref, kk_ref, ka_ref, rk_ref, lng_ref, lnb_ref, seg_ref, tri_ref,
               y_ref, sout_ref, shift_sc, state_sc, y_sc):
    c = pl.program_id(1)

    @pl.when(c == 0)
    def _():
        shift_sc[...] = shift0_ref[0]
        state_sc[...] = s0_ref[0]

    row = lax.broadcasted_iota(I32, (CHUNK, 1), 0)
    valid = (c * CHUNK + row) < seq_len
    prf = jnp.where(valid, pr_ref[0], 0.0)
    prev = jnp.where(row == 0, shift_sc[...], pltpu.roll(prf, 1, 0))
    shift_sc[...] = prf[CHUNK - 1:CHUNK, :]
    xm = prf + (prev - prf) * mix_ref[...]

    r = xm[:, :D_HEADS]
    k = xm[:, D_HEADS:2 * D_HEADS]
    v = xm[:, 2 * D_HEADS:3 * D_HEADS]
    o = 3 * D_HEADS
    xw = xm[:, o:o + LORA_W]
    xa = xm[:, o + LORA_W:o + LORA_W + LORA_A]
    xg = xm[:, o + LORA_W + LORA_A:]

    seg = seg_ref[...]
    wlog = -_softplus(-(w0_ref[...] + _dot(jnp.tanh(xw), w2_ref[...], HIGHEST))) - 0.5
    lw = jnp.where(valid, -jnp.exp(wlog), 0.0)
    a = _sigmoid(a0_ref[...] + _dot(xa, a2_ref[...], HIGHEST))
    g = _dot(_sigmoid(xg), g2_ref[...], HIGHEST)

    kk = k * kk_ref[...]
    kk_norm = jnp.sqrt(_dot_exact_rhs(kk * kk, seg))
    kk = jnp.where(valid, kk / jnp.maximum(kk_norm, 1e-12), 0.0)
    k = jnp.where(valid, k * (1.0 + (a - 1.0) * ka_ref[...]), 0.0)
    v = jnp.where(valid, v, 0.0)

    cl = _dot_exact_lhs(tri_ref[...], lw)
    dec_in = jnp.exp(cl)
    dec_out = jnp.exp(-cl)
    a_t = -kk * jnp.exp(cl - lw)
    b_t = kk * a * dec_out
    k_t = k * dec_out
    r_t = r * dec_in
    gamma = jnp.exp(cl[CHUNK - 1:CHUNK, :])

    ti = lax.broadcasted_iota(I32, (CHUNK, CHUNK), 0)
    si = lax.broadcasted_iota(I32, (CHUNK, CHUNK), 1)
    strict = si < ti
    incl = si <= ti
    eye = (si == ti).astype(F32)

    for h in range(N_HEADS):
        sl = slice(h * HEAD, (h + 1) * HEAD)
        ah, bh, kh, rh, vh = a_t[:, sl], b_t[:, sl], k_t[:, sl], r_t[:, sl], v[:, sl]
        s0 = state_sc[h]
        m_ab = jnp.where(strict, _dot_nt(ah, bh, HIGHEST), 0.0)
        m_ak = jnp.where(strict, _dot_nt(ah, kh, HIGHEST), 0.0)
        m_rb = jnp.where(incl, _dot_nt(rh, bh, HIGHEST), 0.0)
        m_rk = jnp.where(incl, _dot_nt(rh, kh, HIGHEST), 0.0)
        rhs = _dot_nt(ah, s0, HIGHEST) + _dot(m_ak, vh, HIGHEST)
        inv = eye + m_ab
        pw = m_ab
        for _ in range(5):
            pw = _dot(pw, pw, HIGHEST)
            inv = inv + _dot(inv, pw, HIGHEST)
        u = _dot(inv, rhs, HIGHEST)
        y_sc[:, sl] = _dot_nt(rh, s0, HIGHEST) + _dot(m_rb, u, HIGHEST) + _dot(m_rk, vh, HIGHEST)
        s_new = s0 + _dot_tn(u, bh, HIGHEST) + _dot_tn(vh, kh, HIGHEST)
        state_sc[h] = s_new * gamma[:, sl]

    y = y_sc[...]
    yc = y - _dot_exact_rhs(y, seg) * (1.0 / HEAD)
    var = _dot_exact_rhs(yc * yc, seg) * (1.0 / HEAD)
    y = yc * lax.rsqrt(var + LNX_EPS) * lng_ref[...] + lnb_ref[...]
    r = xm[:, :D_HEADS]
    bonus = _dot_exact_rhs(r * k * rk_ref[...], seg) * v
    y_ref[0] = (y + bonus) * g

    @pl.when(c == n_chunks - 1)
    def _():
        sout_ref[0] = state_sc[...]


def _rwkv(pr, shift0, wkv0, p):
    bsz, seq_len, _ = pr.shape
    n_chunks = pl.cdiv(seq_len, CHUNK)
    lane = jnp.arange(D_HEADS) // HEAD
    seg = (lane[:, None] == lane[None, :]).astype(BF16)
    tri = (jnp.arange(CHUNK)[None, :] <= jnp.arange(CHUNK)[:, None]).astype(BF16)
    vec = lambda a: a.reshape(1, -1)
    consts = [vec(p["rwkv_mix"]), vec(p["rwkv_w0"]), p["rwkv_w2"], vec(p["rwkv_a0"]), p["rwkv_a2"], p["rwkv_g2"],
              vec(p["rwkv_k_k"]), vec(p["rwkv_k_a"]), vec(p["rwkv_r_k"]), vec(p["rwkv_lnx_g"]),
              vec(p["rwkv_lnx_b"]), seg, tri]
    return pl.pallas_call(
        functools.partial(_rwkv_body, seq_len, n_chunks),
        grid=(bsz, n_chunks),
        in_specs=[pl.BlockSpec((1, CHUNK, RWKV_IN), lambda b, c: (b, c, 0)),
                  pl.BlockSpec((1, 1, RWKV_IN), lambda b, c: (b, 0, 0)),
                  pl.BlockSpec((1, N_HEADS, HEAD, HEAD), lambda b, c: (b, 0, 0, 0))]
                 + [_full(a.shape) for a in consts],
        out_specs=[pl.BlockSpec((1, CHUNK, D_HEADS), lambda b, c: (b, c, 0)),
                   pl.BlockSpec((1, N_HEADS, HEAD, HEAD), lambda b, c: (b, 0, 0, 0))],
        out_shape=[jax.ShapeDtypeStruct((bsz, seq_len, D_HEADS), F32),
                   jax.ShapeDtypeStruct((bsz, N_HEADS, HEAD, HEAD), F32)],
        scratch_shapes=[pltpu.VMEM((1, RWKV_IN), F32), pltpu.VMEM((N_HEADS, HEAD, HEAD), F32),
                        pltpu.VMEM((CHUNK, D_HEADS), F32)],
        compiler_params=_params(("parallel", "arbitrary")),
        name="rwkv",
    )(pr, shift0, wkv0, *consts)


def _cumsum_body(seq_len, lf_ref, col_ref, row_ref):
    carry = jnp.zeros((1, N_HEADS), F32)
    eye = (lax.broadcasted_iota(I32, (N_HEADS, N_HEADS), 0)
           == lax.broadcasted_iota(I32, (N_HEADS, N_HEADS), 1)).astype(BF16)
    for start in range(0, seq_len, LANES):
        n = min(LANES, seq_len - start)
        tri = (lax.broadcasted_iota(I32, (n, n), 1) <= lax.broadcasted_iota(I32, (n, n), 0)).astype(BF16)
        cum = _dot_exact_lhs(tri, lf_ref[0, start:start + n, :]) + carry
        col_ref[0, start:start + n, :] = cum
        c1, c2, c3 = _split3(cum)
        row_ref[0, :, start:start + n] = _dot_nt(eye, c1) + _dot_nt(eye, c2) + _dot_nt(eye, c3)
        carry = cum[n - 1:n, :]


def _cumsum(lf):
    bsz, seq_len, _ = lf.shape
    return pl.pallas_call(
        functools.partial(_cumsum_body, seq_len),
        grid=(bsz,),
        in_specs=[pl.BlockSpec((1, seq_len, N_HEADS), lambda b: (b, 0, 0))],
        out_specs=[pl.BlockSpec((1, seq_len, N_HEADS), lambda b: (b, 0, 0)),
                   pl.BlockSpec((1, N_HEADS, seq_len), lambda b: (b, 0, 0))],
        out_shape=[jax.ShapeDtypeStruct((bsz, seq_len, N_HEADS), F32),
                   jax.ShapeDtypeStruct((bsz, N_HEADS, seq_len), F32)],
        compiler_params=_params(("parallel",)),
        name="cumsum",
    )(lf)


def _fox_body(kv_len, q_off, tq, tk, n_kv, q_ref, k_ref, v_ref, cq_ref, ck_ref, g_ref, o_ref, m_sc, l_sc, acc_sc):
    qi = pl.program_id(1)
    ki = pl.program_id(2)

    @pl.when(ki == 0)
    def _():
        m_sc[...] = jnp.full(m_sc.shape, -jnp.inf, F32)
        l_sc[...] = jnp.zeros(l_sc.shape, F32)
        acc_sc[...] = jnp.zeros(acc_sc.shape, F32)

    q_start = q_off + qi * tq
    k_start = ki * tk

    @pl.when(k_start <= q_start + tq - 1)
    def _():
        rows = q_start + lax.broadcasted_iota(I32, (tq, 1), 0)
        cols = k_start + lax.broadcasted_iota(I32, (1, tk), 1)
        mask = cols <= rows
        k_valid = (k_start + lax.broadcasted_iota(I32, (tk, 1), 0)) < kv_len
        q = q_ref[0] * (HEAD ** -0.5)
        kb = k_ref[0].astype(BF16)
        vb = jnp.where(k_valid, v_ref[0], 0.0).astype(BF16)
        cq = cq_ref[0]
        ck = ck_ref[0]
        for h in range(N_HEADS):
            sl = slice(h * HEAD, (h + 1) * HEAD)
            s = _dot_nt(q[:, sl].astype(BF16), kb[:, sl])
            s = s + (cq[:, h:h + 1] - ck[h:h + 1, :])
            s = jnp.where(mask, s, -jnp.inf)
            m_prev = m_sc[h]
            m_new = jnp.maximum(m_prev, jnp.max(s, axis=-1, keepdims=True))
            scale = jnp.exp(m_prev - m_new)
            pexp = jnp.exp(s - m_new)
            l_sc[h] = scale * l_sc[h] + jnp.sum(pexp, axis=-1, keepdims=True)
            acc_sc[:, sl] = scale * acc_sc[:, sl] + _dot(pexp.astype(BF16), vb[:, sl])
            m_sc[h] = m_new

    @pl.when(ki == n_kv - 1)
    def _():
        for h in range(N_HEADS):
            sl = slice(h * HEAD, (h + 1) * HEAD)
            o = acc_sc[:, sl] / l_sc[h]
            o = o * lax.rsqrt(jnp.mean(o * o, axis=-1, keepdims=True) + 1e-6)
            acc_sc[:, sl] = o
        o_ref[0] = acc_sc[...] * g_ref[...]


def _fox(q, k_all, v_all, c_col, c_row, norm_g, q_off, tq, tk):
    bsz, q_len, _ = q.shape
    kv_len = k_all.shape[1]
    n_q = pl.cdiv(q_len, tq)
    n_kv = pl.cdiv(kv_len, tk)
    assert q_off % tq == 0
    last_kv = lambda qi: (q_off + (qi + 1) * tq - 1) // tk
    kv_map = lambda b, qi, ki: (b, jnp.minimum(ki, last_kv(qi)), 0)
    return pl.pallas_call(
        functools.partial(_fox_body, kv_len, q_off, tq, tk, n_kv),
        grid=(bsz, n_q, n_kv),
        in_specs=[pl.BlockSpec((1, tq, D_HEADS), lambda b, qi, ki: (b, qi, 0)),
                  pl.BlockSpec((1, tk, D_HEADS), kv_map),
                  pl.BlockSpec((1, tk, D_HEADS), kv_map),
                  pl.BlockSpec((1, tq, N_HEADS), lambda b, qi, ki: (b, q_off // tq + qi, 0)),
                  pl.BlockSpec((1, N_HEADS, tk), lambda b, qi, ki: (b, 0, jnp.minimum(ki, last_kv(qi)))),
                  _full((1, D_HEADS))],
        out_specs=pl.BlockSpec((1, tq, D_HEADS), lambda b, qi, ki: (b, qi, 0)),
        out_shape=jax.ShapeDtypeStruct((bsz, q_len, D_HEADS), F32),
        scratch_shapes=[pltpu.VMEM((N_HEADS, tq, 1), F32), pltpu.VMEM((N_HEADS, tq, 1), F32),
                        pltpu.VMEM((tq, D_HEADS), F32)],
        compiler_params=_params(("parallel", "parallel", "arbitrary")),
        name="fox",
    )(q, k_all, v_all, c_col, c_row, norm_g)


def _out_proj_body(yr_ref, yf_ref, h_ref, wr_ref, wf_ref, g_ref, b_ref, o_ref):
    mix = _dot(yr_ref[...].astype(BF16), wr_ref[...]) + _dot(yf_ref[...].astype(BF16), wf_ref[...])
    o_ref[...] = _layernorm(ALPHA * h_ref[...] + mix, g_ref[...], b_ref[...])


def _out_proj(yr, yf, h, wr, wf, g, b):
    n = h.shape[0]
    tb = TOKEN_BLOCK
    rows = lambda w: pl.BlockSpec((tb, w), lambda i: (i, 0))
    return pl.pallas_call(
        _out_proj_body,
        grid=(n // tb,),
        in_specs=[rows(D_HEADS), rows(D_HEADS), rows(D_MODEL), _full(wr.shape), _full(wf.shape),
                  _full((1, D_MODEL)), _full((1, D_MODEL))],
        out_specs=rows(D_MODEL),
        out_shape=jax.ShapeDtypeStruct((n, D_MODEL), F32),
        compiler_params=_params(("parallel",)),
        name="out_proj",
    )(yr, yf, h, wr, wf, g, b)


def _top_rows(s, payload, order, k):
    big = jnp.int32(2 ** 30)
    vals, pays = [], []
    for _ in range(k):
        m = jnp.max(s, axis=0, keepdims=True)
        first = jnp.min(jnp.where(s == m, order, big), axis=0, keepdims=True)
        sel = order == first
        vals.append(m)
        pays.append(jnp.max(jnp.where(sel, payload, -1), axis=0, keepdims=True))
        s = jnp.where(sel, -jnp.inf, s)
    return jnp.concatenate(vals, axis=0), jnp.concatenate(pays, axis=0)


def _topk_body(h_ref, wq_ref, k1_ref, k2_ref, idx_ref, gate_ref):
    tb = h_ref.shape[0]
    q = _dot(h_ref[...].astype(BF16), wq_ref[...])
    key_id = lax.broadcasted_iota(I32, (N_KEYS, tb), 0)
    n_cand = PEER_TOPK * PEER_TOPK
    cand_pos = lax.broadcasted_iota(I32, (n_cand, tb), 0)
    k1 = k1_ref[...].astype(BF16)
    k2 = k2_ref[...].astype(BF16)
    for h in range(PEER_HEADS):
        o = h * 2 * D_KEY_HALF
        s1 = _dot_nt(k1, q[:, o:o + D_KEY_HALF].astype(BF16))
        s2 = _dot_nt(k2, q[:, o + D_KEY_HALF:o + 2 * D_KEY_HALF].astype(BF16))
        v1, i1 = _top_rows(s1, key_id, key_id, PEER_TOPK)
        v2, i2 = _top_rows(s2, key_id, key_id, PEER_TOPK)
        cand_s = jnp.concatenate([v1[i:i + 1] + v2 for i in range(PEER_TOPK)], axis=0)
        cand_i = jnp.concatenate([i1[i:i + 1] * N_KEYS + i2 for i in range(PEER_TOPK)], axis=0)
        best_s, best_i = _top_rows(cand_s, cand_i, cand_pos, PEER_TOPK)
        e = jnp.exp(best_s - best_s[0:1])
        rows = slice(h * PEER_TOPK, (h + 1) * PEER_TOPK)
        idx_ref[rows, :] = best_i
        gate_ref[rows, :] = e / jnp.sum(e, axis=0, keepdims=True)


def _topk(h1, wq, keys1, keys2):
    n = h1.shape[0]
    tb = TOKEN_BLOCK
    return pl.pallas_call(
        _topk_body,
        grid=(n // tb,),
        in_specs=[pl.BlockSpec((tb, D_MODEL), lambda i: (i, 0)), _full(wq.shape), _full(keys1.shape),
                  _full(keys2.shape)],
        out_specs=[pl.BlockSpec((PEER_SLOTS, tb), lambda i: (0, i)), pl.BlockSpec((PEER_SLOTS, tb), lambda i: (0, i))],
        out_shape=[jax.ShapeDtypeStruct((PEER_SLOTS, n), I32), jax.ShapeDtypeStruct((PEER_SLOTS, n), F32)],
        compiler_params=_params(("parallel",)),
        name="peer_topk",
    )(h1, wq, keys1, keys2)


def _gather_body(n_steps, idx_ref, idx_next_ref, x_ref, gate_ref, g_ref, b_ref, uv_ref, o_ref, buf, sems):
    i = pl.program_id(0)
    tg = GATHER_TOKENS
    n_rows = tg * PEER_SLOTS

    def row_copy(idx, slot, j):
        e = idx[j // PEER_SLOTS, j % PEER_SLOTS]
        return pltpu.make_async_copy(uv_ref.at[pl.ds(e, 1), :], buf.at[slot, pl.ds(j, 1), :], sems.at[slot])

    def issue(idx, slot):
        def step(j, carry):
            row_copy(idx, slot, j).start()
            return carry
        lax.fori_loop(0, n_rows, step, 0)

    slot = i % 2

    @pl.when(i == 0)
    def _():
        issue(idx_ref, 0)

    @pl.when(i + 1 < n_steps)
    def _():
        issue(idx_next_ref, 1 - slot)

    pltpu.make_async_copy(uv_ref.at[pl.ds(0, n_rows), :], buf.at[slot], sems.at[slot]).wait()

    x = x_ref[...]
    rows = buf[slot]
    ub = rows[:, :D_MODEL].astype(BF16)
    vb = rows[:, D_MODEL:].astype(BF16)
    res = _dot_nt(x.astype(BF16), ub)
    tok = lax.broadcasted_iota(I32, (tg, PEER_SLOTS), 0)
    act = jnp.zeros((tg, PEER_SLOTS), F32)
    for t in range(tg):
        act = act + jnp.where(tok == t, res[:, t * PEER_SLOTS:(t + 1) * PEER_SLOTS], 0.0)
    coef = gate_ref[...] * (act * 0.5 * (1.0 + lax.erf(act * (2.0 ** -0.5))))
    col_tok = lax.broadcasted_iota(I32, (tg, n_rows), 1) // PEER_SLOTS
    row_tok = lax.broadcasted_iota(I32, (tg, n_rows), 0)
    wide = jnp.where(col_tok == row_tok, jnp.concatenate([coef] * tg, axis=1), 0.0)
    out = _dot(wide.astype(BF16), vb)
    o_ref[...] = _layernorm(ALPHA * x + out, g_ref[...], b_ref[...])


def _gather(idx, gate, h1, uv, g, b):
    n = h1.shape[0]
    tg = GATHER_TOKENS
    n_steps = n // tg
    smem_rows = lambda shift: pl.BlockSpec((tg, PEER_SLOTS), lambda i: (jnp.minimum(i + shift, n_steps - 1), 0),
                                           memory_space=pltpu.SMEM)
    return pl.pallas_call(
        functools.partial(_gather_body, n_steps),
        grid=(n_steps,),
        in_specs=[smem_rows(0), smem_rows(1),
                  pl.BlockSpec((tg, D_MODEL), lambda i: (i, 0)),
                  pl.BlockSpec((tg, PEER_SLOTS), lambda i: (i, 0)),
                  _full((1, D_MODEL)), _full((1, D_MODEL)),
                  pl.BlockSpec(memory_space=pl.ANY)],
        out_specs=pl.BlockSpec((tg, D_MODEL), lambda i: (i, 0)),
        out_shape=jax.ShapeDtypeStruct((n, D_MODEL), F32),
        scratch_shapes=[pltpu.VMEM((2, tg * PEER_SLOTS, 2 * D_MODEL), F32), pltpu.SemaphoreType.DMA((2,))],
        compiler_params=_params(("arbitrary",)),
        name="peer_gather",
    )(idx, idx, h1, gate, g, b, uv)


def _group(x, shift0, wkv0, past, p):
    bsz, seq_len, _ = x.shape
    n = bsz * seq_len
    h, pr, q, k, v, lf = _in_proj(x.reshape(n, D_MODEL), p["ln_in_g"], p["ln_in_b"], p["w_r"], p["w_qkv"],
                                  p["w_f"], p["b_f"])
    pr = pr.reshape(bsz, seq_len, RWKV_IN)
    y_r, wkv_new = _rwkv(pr, shift0, wkv0, p)

    seq3 = lambda a: a.reshape(bsz, seq_len, -1)
    k3, v3, lf3 = seq3(k), seq3(v), seq3(lf)
    if past is None:
        k_all, v_all, lf_all, q_off, tq = k3, v3, lf3, 0, FOX_BLOCK
    else:
        k_past, v_past, lf_past = past
        q_off = k_past.shape[1]
        k_all = jnp.concatenate([k_past.reshape(bsz, q_off, D_HEADS), k3], axis=1)
        v_all = jnp.concatenate([v_past.reshape(bsz, q_off, D_HEADS), v3], axis=1)
        lf_all = jnp.concatenate([lf_past, lf3], axis=1)
        tq = seq_len
    c_col, c_row = _cumsum(lf_all)
    y_f = _fox(seq3(q), k_all, v_all, c_col, c_row, p["fox_norm_g"], q_off, tq, FOX_BLOCK)

    h1 = _out_proj(y_r.reshape(n, D_HEADS), y_f.reshape(n, D_HEADS), h, p["w_out_r"], p["w_out_f"],
                   p["ln1_g"], p["ln1_b"])
    idx_t, gate_t = _topk(h1, p["peer_w_q"], p["peer_keys1"], p["peer_keys2"])
    y = _gather(idx_t.T, gate_t.T, h1, p["peer_uv"], p["ln2_g"], p["ln2_b"])

    heads = lambda a: a.reshape(1, bsz, seq_len, N_HEADS, HEAD)
    return (y.reshape(bsz, seq_len, D_MODEL), pr[None, :, seq_len - 1:, :], wkv_new[None],
            heads(k), heads(v), lf3[None])


def kernel(x_prompt, x_sample, state_rwkv_shift, state_rwkv_wkv, cache_fox_k, cache_fox_v, cache_fox_logf,
           meta_tokens, ln_in_g, ln_in_b, w_in, rwkv_mix, rwkv_w0, rwkv_w2, rwkv_a0, rwkv_a2, rwkv_g2,
           rwkv_k_k, rwkv_k_a, rwkv_r_k, rwkv_lnx_g, rwkv_lnx_b, fox_b_f, fox_norm_g, w_out, ln1_g, ln1_b,
           peer_w_q, peer_keys1, peer_keys2, peer_u, peer_v, ln2_g, ln2_b):
    assert w_in.shape[0] == 1, "single-layer step"
    bsz = x_prompt.shape[0]
    row = lambda a: a.reshape(1, -1)
    w = w_in[0]
    fox0 = RWKV_IN
    pad_f = LANES - N_HEADS
    p = dict(
        ln_in_g=row(ln_in_g), ln_in_b=row(ln_in_b),
        w_r=w[:, :RWKV_IN].astype(BF16),
        w_qkv=w[:, fox0:fox0 + 3 * D_HEADS].astype(BF16),
        w_f=jnp.pad(w[:, fox0 + 3 * D_HEADS:], ((0, 0), (0, pad_f))).astype(BF16),
        b_f=jnp.pad(row(fox_b_f[0]), ((0, 0), (0, pad_f))),
        rwkv_mix=rwkv_mix[0], rwkv_w0=rwkv_w0[0], rwkv_w2=rwkv_w2[0], rwkv_a0=rwkv_a0[0], rwkv_a2=rwkv_a2[0],
        rwkv_g2=rwkv_g2[0], rwkv_k_k=rwkv_k_k[0], rwkv_k_a=rwkv_k_a[0], rwkv_r_k=rwkv_r_k[0],
        rwkv_lnx_g=rwkv_lnx_g[0], rwkv_lnx_b=rwkv_lnx_b[0],
        fox_norm_g=row(fox_norm_g[0]),
        w_out_r=w_out[0, :D_HEADS].astype(BF16), w_out_f=w_out[0, D_HEADS:].astype(BF16),
        ln1_g=row(ln1_g[0]), ln1_b=row(ln1_b[0]),
        peer_w_q=peer_w_q[0].astype(BF16), peer_keys1=peer_keys1[0], peer_keys2=peer_keys2[0],
        peer_uv=jnp.concatenate([peer_u[0], peer_v[0]], axis=1),
        ln2_g=row(ln2_g[0]), ln2_b=row(ln2_b[0]),
    )
    meta = jnp.broadcast_to(meta_tokens[None], (bsz, N_META, D_MODEL))
    hp0 = jnp.concatenate([meta, x_prompt], axis=1)
    zero_shift = jnp.zeros((bsz, 1, RWKV_IN), F32)
    zero_wkv = jnp.zeros((bsz, N_HEADS, HEAD, HEAD), F32)
    yp, p_shift, p_wkv, p_k, p_v, p_lf = _group(hp0, zero_shift, zero_wkv, None, p)
    ys, s_shift, s_wkv, s_k, s_v, s_lf = _group(
        x_sample, state_rwkv_shift[0], state_rwkv_wkv[0],
        (cache_fox_k[0], cache_fox_v[0], cache_fox_logf[0]), p)
    return (yp[:, N_META:], ys, p_shift, p_wkv, p_k, p_v, p_lf, s_shift, s_wkv, s_k, s_v, s_lf)
```

```python
import functools

import jax
import jax.numpy as jnp
from jax import lax
from jax.experimental import pallas as pl
from jax.experimental.pallas import tpu as pltpu
from jax.experimental.pallas import tpu_sc as plsc

F32 = jnp.float32
BF16 = jnp.bfloat16
I32 = jnp.int32

D_MODEL = 1024
N_HEADS = 8
HEAD = 64
D_HEADS = N_HEADS * HEAD
LORA_W, LORA_A, LORA_G = 64, 64, 128
RWKV_IN = 3 * D_HEADS + LORA_W + LORA_A + LORA_G
N_META = 16
LNX_EPS = 64e-5
LN_EPS = 1e-5
ALPHA = 2.0 ** 0.25
N_KEYS = 128
PEER_HEADS = 8
PEER_TOPK = 16
PEER_SLOTS = PEER_HEADS * PEER_TOPK
D_KEY_HALF = 128

LANES = 128
CHUNK = 64
TOKEN_BLOCK = 256
FOX_BLOCK = 256
GATHER_TOKENS = 8
SC_CORES, SC_SUBCORES = 2, 16
SC_ROWS = 32
SC_INDEX_STAGE = 64
PEER_PARTS = 4
PEER_PART_QUANTUM = SC_ROWS * SC_CORES * SC_SUBCORES * 2 // PEER_SLOTS
VMEM_LIMIT = 48 * 1024 * 1024

HIGHEST = lax.Precision.HIGHEST


def _params(semantics):
    return pltpu.CompilerParams(dimension_semantics=semantics, vmem_limit_bytes=VMEM_LIMIT)


def _dot(a, b, precision=None):
    return jnp.dot(a, b, preferred_element_type=F32, precision=precision)


def _dot_nt(a, b, precision=None):
    return lax.dot_general(a, b, (((1,), (1,)), ((), ())), preferred_element_type=F32, precision=precision)


def _dot_tn(a, b, precision=None):
    return lax.dot_general(a, b, (((0,), (0,)), ((), ())), preferred_element_type=F32, precision=precision)


def _split3(a):
    a1 = a.astype(BF16)
    r1 = a - a1.astype(F32)
    a2 = r1.astype(BF16)
    a3 = (r1 - a2.astype(F32)).astype(BF16)
    return a1, a2, a3


def _split2(a):
    hi = a.astype(BF16)
    return hi, (a - hi.astype(F32)).astype(BF16)


def _mm3(dot, a, b):
    return dot(a[0], b[0]) + (dot(a[0], b[1]) + dot(a[1], b[0]))


def _dot_exact_rhs(a, m_bf16):
    a1, a2, a3 = _split3(a)
    return _dot(a1, m_bf16) + _dot(a2, m_bf16) + _dot(a3, m_bf16)


def _dot_exact_lhs(m_bf16, a):
    a1, a2, a3 = _split3(a)
    return _dot(m_bf16, a1) + _dot(m_bf16, a2) + _dot(m_bf16, a3)


def _layernorm(x, g, b):
    xc = x - jnp.mean(x, axis=-1, keepdims=True)
    var = jnp.mean(xc * xc, axis=-1, keepdims=True)
    return xc * lax.rsqrt(var + LN_EPS) * g + b


def _softplus(x):
    return jnp.maximum(x, 0.0) + jnp.log1p(jnp.exp(-jnp.abs(x)))


def _sigmoid(x):
    return 1.0 / (1.0 + jnp.exp(-x))


def _full(shape):
    return pl.BlockSpec(shape, lambda *_: (0,) * len(shape))


def _in_proj_body(x_ref, g_ref, b_ref, wr_ref, wqkv_ref, wf_ref, bf_ref,
                  h_ref, pr_ref, q_ref, k_ref, v_ref, lf_ref):
    h = _layernorm(x_ref[...], g_ref[...], b_ref[...])
    h_ref[...] = h
    hb = h.astype(BF16)
    pr_ref[...] = _dot(hb, wr_ref[...])
    qkv = _dot(hb, wqkv_ref[...])
    q_ref[...] = qkv[:, :D_HEADS]
    k_ref[...] = qkv[:, D_HEADS:2 * D_HEADS]
    v_ref[...] = qkv[:, 2 * D_HEADS:]
    fl = _dot(hb, wf_ref[...]) + bf_ref[...]
    lf = jnp.minimum(fl, 0.0) - jnp.log1p(jnp.exp(-jnp.abs(fl)))
    lf_ref[...] = lf[:, :N_HEADS]


def _in_proj(x, g, b, wr, wqkv, wf, bfp):
    n = x.shape[0]
    tb = TOKEN_BLOCK
    rows = lambda w: pl.BlockSpec((tb, w), lambda i: (i, 0))
    return pl.pallas_call(
        _in_proj_body,
        grid=(n // tb,),
        in_specs=[rows(D_MODEL), _full((1, D_MODEL)), _full((1, D_MODEL)), _full(wr.shape),
                  _full(wqkv.shape), _full(wf.shape), _full(bfp.shape)],
        out_specs=[rows(D_MODEL), rows(RWKV_IN), rows(D_HEADS), rows(D_HEADS), rows(D_HEADS), rows(N_HEADS)],
        out_shape=[jax.ShapeDtypeStruct((n, w), F32)
                   for w in (D_MODEL, RWKV_IN, D_HEADS, D_HEADS, D_HEADS, N_HEADS)],
        compiler_params=_params(("parallel",)),
        name="in_proj",
    )(x, g, b, wr, wqkv, wf, bfp)


def _rwkv_body(seq_len, n_chunks, pr_ref, shift0_ref, s0_ref, mix_ref, w0_ref, w2_ref, a0_ref, a2_ref,
               g2_ref, kk_ref, ka_ref, rk_ref, lng_ref, lnb_ref, seg_ref, tri_ref,
               y_ref, sout_ref, shift_sc, state_sc):
    c = pl.program_id(1)

    @pl.when(c == 0)
    def _():
        shift_sc[...] = shift0_ref[0]
        state_sc[...] = s0_ref[0]

    row = lax.broadcasted_iota(I32, (CHUNK, 1), 0)
    valid = (c * CHUNK + row) < seq_len
    prf = jnp.where(valid, pr_ref[0], 0.0)
    prev = jnp.where(row == 0, shift_sc[...], pltpu.roll(prf, 1, 0))
    shift_sc[...] = prf[CHUNK - 1:CHUNK, :]
    xm = prf + (prev - prf) * mix_ref[...]

    r = xm[:, :D_HEADS]
    k = xm[:, D_HEADS:2 * D_HEADS]
    v = xm[:, 2 * D_HEADS:3 * D_HEADS]
    o = 3 * D_HEADS
    xw = xm[:, o:o + LORA_W]
    xa = xm[:, o + LORA_W:o + LORA_W + LORA_A]
    xg = xm[:, o + LORA_W + LORA_A:]

    seg = seg_ref[...]
    wlog = -_softplus(-(w0_ref[...] + _dot(jnp.tanh(xw), w2_ref[...], HIGHEST))) - 0.5
    lw = jnp.where(valid, -jnp.exp(wlog), 0.0)
    a = _sigmoid(a0_ref[...] + _dot(xa, a2_ref[...], HIGHEST))
    g = _dot(_sigmoid(xg), g2_ref[...], HIGHEST)

    kk = k * kk_ref[...]
    kk_norm = jnp.sqrt(_dot_exact_rhs(kk * kk, seg))
    kk = jnp.where(valid, kk / jnp.maximum(kk_norm, 1e-12), 0.0)
    k = jnp.where(valid, k * (1.0 + (a - 1.0) * ka_ref[...]), 0.0)
    v = jnp.where(valid, v, 0.0)

    cl = _dot_exact_lhs(tri_ref[...], lw)
    dec_in = jnp.exp(cl)
    dec_out = jnp.exp(-cl)
    a_t = -kk * jnp.exp(cl - lw)
    b_t = kk * a * dec_out
    k_t = k * dec_out
    r_t = r * dec_in
    gamma = jnp.exp(cl[CHUNK - 1:CHUNK, :])

    ti = lax.broadcasted_iota(I32, (CHUNK, 2 * CHUNK), 0)
    si = lax.broadcasted_iota(I32, (CHUNK, 2 * CHUNK), 1)
    si = jnp.where(si >= CHUNK, si - CHUNK, si)
    strict = si < ti
    incl = si <= ti
    eye = (lax.broadcasted_iota(I32, (CHUNK, CHUNK), 0) == lax.broadcasted_iota(I32, (CHUNK, CHUNK), 1)).astype(F32)
    zeros = jnp.zeros((CHUNK, HEAD), F32)

    heads = range(N_HEADS)
    sls = [slice(h * HEAD, (h + 1) * HEAD) for h in heads]
    vh = [v[:, sl] for sl in sls]
    ar = [_split2(jnp.concatenate([a_t[:, sl], r_t[:, sl]], axis=0)) for sl in sls]
    bk = [_split2(jnp.concatenate([b_t[:, sl], k_t[:, sl]], axis=0)) for sl in sls]
    s0 = [state_sc[h] for h in heads]
    gram = [_mm3(_dot_nt, ar[h], bk[h]) for h in heads]
    top = [jnp.where(strict, gram[h][:CHUNK], 0.0) for h in heads]
    bot = [jnp.where(incl, gram[h][CHUNK:], 0.0) for h in heads]
    hs = [_mm3(_dot_nt, ar[h], _split2(s0[h])) for h in heads]
    rhs = [hs[h][:CHUNK] + _mm3(_dot, _split2(top[h]), _split2(jnp.concatenate([zeros, vh[h]], axis=0)))
           for h in heads]
    inv = [eye + top[h][:, :CHUNK] for h in heads]
    pws = [_split2(top[h][:, :CHUNK]) for h in heads]
    for _ in range(5):
        pws = [_split2(_mm3(_dot, pws[h], pws[h])) for h in heads]
        inv = [inv[h] + _mm3(_dot, _split2(inv[h]), pws[h]) for h in heads]
    u = [_mm3(_dot, _split2(inv[h]), _split2(rhs[h])) for h in heads]
    uv = [_split2(jnp.concatenate([u[h], vh[h]], axis=0)) for h in heads]
    ys = [hs[h][CHUNK:] + _dot(bot[h].astype(BF16), uv[h][0]) for h in heads]
    for h in heads:
        state_sc[h] = (s0[h] + _mm3(_dot_tn, uv[h], bk[h])) * gamma[:, sls[h]]

    y = jnp.concatenate(ys, axis=1)
    yc = y - _dot_exact_rhs(y, seg) * (1.0 / HEAD)
    var = _dot_exact_rhs(yc * yc, seg) * (1.0 / HEAD)
    y = yc * lax.rsqrt(var + LNX_EPS) * lng_ref[...] + lnb_ref[...]
    r = xm[:, :D_HEADS]
    bonus = _dot_exact_rhs(r * k * rk_ref[...], seg) * v
    y_ref[0] = (y + bonus) * g

    @pl.when(c == n_chunks - 1)
    def _():
        sout_ref[0] = state_sc[...]


def _rwkv(pr, shift0, wkv0, p):
    bsz, seq_len, _ = pr.shape
    n_chunks = pl.cdiv(seq_len, CHUNK)
    lane = jnp.arange(D_HEADS) // HEAD
    seg = (lane[:, None] == lane[None, :]).astype(BF16)
    tri = (jnp.arange(CHUNK)[None, :] <= jnp.arange(CHUNK)[:, None]).astype(BF16)
    vec = lambda a: a.reshape(1, -1)
    consts = [vec(p["rwkv_mix"]), vec(p["rwkv_w0"]), p["rwkv_w2"], vec(p["rwkv_a0"]), p["rwkv_a2"], p["rwkv_g2"],
              vec(p["rwkv_k_k"]), vec(p["rwkv_k_a"]), vec(p["rwkv_r_k"]), vec(p["rwkv_lnx_g"]),
              vec(p["rwkv_lnx_b"]), seg, tri]
    return pl.pallas_call(
        functools.partial(_rwkv_body, seq_len, n_chunks),
        grid=(bsz, n_chunks),
        in_specs=[pl.BlockSpec((1, CHUNK, RWKV_IN), lambda b, c: (b, c, 0)),
                  pl.BlockSpec((1, 1, RWKV_IN), lambda b, c: (b, 0, 0)),
                  pl.BlockSpec((1, N_HEADS, HEAD, HEAD), lambda b, c: (b, 0, 0, 0))]
                 + [_full(a.shape) for a in consts],
        out_specs=[pl.BlockSpec((1, CHUNK, D_HEADS), lambda b, c: (b, c, 0)),
                   pl.BlockSpec((1, N_HEADS, HEAD, HEAD), lambda b, c: (b, 0, 0, 0))],
        out_shape=[jax.ShapeDtypeStruct((bsz, seq_len, D_HEADS), F32),
                   jax.ShapeDtypeStruct((bsz, N_HEADS, HEAD, HEAD), F32)],
        scratch_shapes=[pltpu.VMEM((1, RWKV_IN), F32), pltpu.VMEM((N_HEADS, HEAD, HEAD), F32)],
        compiler_params=_params(("parallel", "arbitrary")),
        name="rwkv",
    )(pr, shift0, wkv0, *consts)


def _cumsum_body(seq_len, lf_ref, col_ref, row_ref):
    carry = jnp.zeros((1, N_HEADS), F32)
    eye = (lax.broadcasted_iota(I32, (N_HEADS, N_HEADS), 0)
           == lax.broadcasted_iota(I32, (N_HEADS, N_HEADS), 1)).astype(BF16)
    for start in range(0, seq_len, LANES):
        n = min(LANES, seq_len - start)
        tri = (lax.broadcasted_iota(I32, (n, n), 1) <= lax.broadcasted_iota(I32, (n, n), 0)).astype(BF16)
        cum = _dot_exact_lhs(tri, lf_ref[0, start:start + n, :]) + carry
        col_ref[0, start:start + n, :] = cum
        c1, c2, c3 = _split3(cum)
        row_ref[0, :, start:start + n] = _dot_nt(eye, c1) + _dot_nt(eye, c2) + _dot_nt(eye, c3)
        carry = cum[n - 1:n, :]


def _cumsum(lf):
    bsz, seq_len, _ = lf.shape
    return pl.pallas_call(
        functools.partial(_cumsum_body, seq_len),
        grid=(bsz,),
        in_specs=[pl.BlockSpec((1, seq_len, N_HEADS), lambda b: (b, 0, 0))],
        out_specs=[pl.BlockSpec((1, seq_len, N_HEADS), lambda b: (b, 0, 0)),
                   pl.BlockSpec((1, N_HEADS, seq_len), lambda b: (b, 0, 0))],
        out_shape=[jax.ShapeDtypeStruct((bsz, seq_len, N_HEADS), F32),
                   jax.ShapeDtypeStruct((bsz, N_HEADS, seq_len), F32)],
        compiler_params=_params(("parallel",)),
        name="cumsum",
    )(lf)


def _fox_body(kv_len, q_off, tq, tk, n_kv, q_ref, k_ref, v_ref, cq_ref, ck_ref, g_ref, o_ref, m_sc, l_sc, acc_sc):
    qi = pl.program_id(1)
    ki = pl.program_id(2)

    @pl.when(ki == 0)
    def _():
        m_sc[...] = jnp.full(m_sc.shape, -jnp.inf, F32)
        l_sc[...] = jnp.zeros(l_sc.shape, F32)
        acc_sc[...] = jnp.zeros(acc_sc.shape, F32)

    q_start = q_off + qi * tq
    k_start = ki * tk

    @pl.when(k_start <= q_start + tq - 1)
    def _():
        rows = q_start + lax.broadcasted_iota(I32, (tq, 1), 0)
        cols = k_start + lax.broadcasted_iota(I32, (1, tk), 1)
        mask = cols <= rows
        k_valid = (k_start + lax.broadcasted_iota(I32, (tk, 1), 0)) < kv_len
        q = q_ref[0] * (HEAD ** -0.5)
        kb = k_ref[0].astype(BF16)
        vb = jnp.where(k_valid, v_ref[0], 0.0).astype(BF16)
        cq = cq_ref[0]
        ck = ck_ref[0]
        for h in range(N_HEADS):
            sl = slice(h * HEAD, (h + 1) * HEAD)
            s = _dot_nt(q[:, sl].astype(BF16), kb[:, sl])
            s = s + (cq[:, h:h + 1] - ck[h:h + 1, :])
            s = jnp.where(mask, s, -jnp.inf)
            m_prev = m_sc[h]
            m_new = jnp.maximum(m_prev, jnp.max(s, axis=-1, keepdims=True))
            scale = jnp.exp(m_prev - m_new)
            pexp = jnp.exp(s - m_new)
            l_sc[h] = scale * l_sc[h] + jnp.sum(pexp, axis=-1, keepdims=True)
            acc_sc[:, sl] = scale * acc_sc[:, sl] + _dot(pexp.astype(BF16), vb[:, sl])
            m_sc[h] = m_new

    @pl.when(ki == n_kv - 1)
    def _():
        for h in range(N_HEADS):
            sl = slice(h * HEAD, (h + 1) * HEAD)
            o = acc_sc[:, sl] / l_sc[h]
            o = o * lax.rsqrt(jnp.mean(o * o, axis=-1, keepdims=True) + 1e-6)
            acc_sc[:, sl] = o
        o_ref[0] = acc_sc[...] * g_ref[...]


def _fox(q, k_all, v_all, c_col, c_row, norm_g, q_off, tq, tk):
    bsz, q_len, _ = q.shape
    kv_len = k_all.shape[1]
    n_q = pl.cdiv(q_len, tq)
    n_kv = pl.cdiv(kv_len, tk)
    assert q_off % tq == 0
    last_kv = lambda qi: (q_off + (qi + 1) * tq - 1) // tk
    kv_map = lambda b, qi, ki: (b, jnp.minimum(ki, last_kv(qi)), 0)
    return pl.pallas_call(
        functools.partial(_fox_body, kv_len, q_off, tq, tk, n_kv),
        grid=(bsz, n_q, n_kv),
        in_specs=[pl.BlockSpec((1, tq, D_HEADS), lambda b, qi, ki: (b, qi, 0)),
                  pl.BlockSpec((1, tk, D_HEADS), kv_map),
                  pl.BlockSpec((1, tk, D_HEADS), kv_map),
                  pl.BlockSpec((1, tq, N_HEADS), lambda b, qi, ki: (b, q_off // tq + qi, 0)),
                  pl.BlockSpec((1, N_HEADS, tk), lambda b, qi, ki: (b, 0, jnp.minimum(ki, last_kv(qi)))),
                  _full((1, D_HEADS))],
        out_specs=pl.BlockSpec((1, tq, D_HEADS), lambda b, qi, ki: (b, qi, 0)),
        out_shape=jax.ShapeDtypeStruct((bsz, q_len, D_HEADS), F32),
        scratch_shapes=[pltpu.VMEM((N_HEADS, tq, 1), F32), pltpu.VMEM((N_HEADS, tq, 1), F32),
                        pltpu.VMEM((tq, D_HEADS), F32)],
        compiler_params=_params(("parallel", "parallel", "arbitrary")),
        name="fox",
    )(q, k_all, v_all, c_col, c_row, norm_g)


def _out_proj_body(yr_ref, yf_ref, h_ref, wr_ref, wf_ref, g_ref, b_ref, o_ref):
    mix = _dot(yr_ref[...].astype(BF16), wr_ref[...]) + _dot(yf_ref[...].astype(BF16), wf_ref[...])
    o_ref[...] = _layernorm(ALPHA * h_ref[...] + mix, g_ref[...], b_ref[...])


def _out_proj(yr, yf, h, wr, wf, g, b):
    n = h.shape[0]
    tb = TOKEN_BLOCK
    rows = lambda w: pl.BlockSpec((tb, w), lambda i: (i, 0))
    return pl.pallas_call(
        _out_proj_body,
        grid=(n // tb,),
        in_specs=[rows(D_HEADS), rows(D_HEADS), rows(D_MODEL), _full(wr.shape), _full(wf.shape),
                  _full((1, D_MODEL)), _full((1, D_MODEL))],
        out_specs=rows(D_MODEL),
        out_shape=jax.ShapeDtypeStruct((n, D_MODEL), F32),
        compiler_params=_params(("parallel",)),
        name="out_proj",
    )(yr, yf, h, wr, wf, g, b)


def _top_rows(s, payload, order, k):
    big = jnp.int32(2 ** 30)
    vals, pays = [], []
    for _ in range(k):
        m = jnp.max(s, axis=0, keepdims=True)
        first = jnp.min(jnp.where(s == m, order, big), axis=0, keepdims=True)
        sel = order == first
        vals.append(m)
        pays.append(jnp.max(jnp.where(sel, payload, -1), axis=0, keepdims=True))
        s = jnp.where(sel, -jnp.inf, s)
    return jnp.concatenate(vals, axis=0), jnp.concatenate(pays, axis=0)


def _topk_body(h_ref, wq_ref, k1_ref, k2_ref, idx_ref, gate_ref):
    tb = h_ref.shape[0]
    q = _dot(h_ref[...].astype(BF16), wq_ref[...])
    key_id = lax.broadcasted_iota(I32, (N_KEYS, tb), 0)
    n_cand = PEER_TOPK * PEER_TOPK
    cand_pos = lax.broadcasted_iota(I32, (n_cand, tb), 0)
    k1 = k1_ref[...].astype(BF16)
    k2 = k2_ref[...].astype(BF16)
    for h in range(PEER_HEADS):
        o = h * 2 * D_KEY_HALF
        s1 = _dot_nt(k1, q[:, o:o + D_KEY_HALF].astype(BF16))
        s2 = _dot_nt(k2, q[:, o + D_KEY_HALF:o + 2 * D_KEY_HALF].astype(BF16))
        v1, i1 = _top_rows(s1, key_id, key_id, PEER_TOPK)
        v2, i2 = _top_rows(s2, key_id, key_id, PEER_TOPK)
        cand_s = jnp.concatenate([v1[i:i + 1] + v2 for i in range(PEER_TOPK)], axis=0)
        cand_i = jnp.concatenate([i1[i:i + 1] * N_KEYS + i2 for i in range(PEER_TOPK)], axis=0)
        best_s, best_i = _top_rows(cand_s, cand_i, cand_pos, PEER_TOPK)
        e = jnp.exp(best_s - best_s[0:1])
        rows = slice(h * PEER_TOPK, (h + 1) * PEER_TOPK)
        idx_ref[rows, :] = best_i
        gate_ref[rows, :] = e / jnp.sum(e, axis=0, keepdims=True)


def _topk(h1, wq, keys1, keys2):
    n = h1.shape[0]
    tb = TOKEN_BLOCK
    return pl.pallas_call(
        _topk_body,
        grid=(n // tb,),
        in_specs=[pl.BlockSpec((tb, D_MODEL), lambda i: (i, 0)), _full(wq.shape), _full(keys1.shape),
                  _full(keys2.shape)],
        out_specs=[pl.BlockSpec((PEER_SLOTS, tb), lambda i: (0, i)), pl.BlockSpec((PEER_SLOTS, tb), lambda i: (0, i))],
        out_shape=[jax.ShapeDtypeStruct((PEER_SLOTS, n), I32), jax.ShapeDtypeStruct((PEER_SLOTS, n), F32)],
        compiler_params=_params(("parallel",)),
        name="peer_topk",
    )(h1, wq, keys1, keys2)


def _expert_rows(table, idx):
    n_rows = idx.shape[0]
    width = table.shape[1]
    n_workers = SC_CORES * SC_SUBCORES
    assert n_rows % (SC_ROWS * n_workers) == 0
    chunks_per_worker = n_rows // (SC_ROWS * n_workers)
    stage = max(d for d in range(2, SC_INDEX_STAGE + 1, 2) if chunks_per_worker % d == 0)
    mesh = plsc.VectorSubcoreMesh(core_axis_name="c", subcore_axis_name="s",
                                  num_cores=SC_CORES, num_subcores=SC_SUBCORES)

    @functools.partial(
        pl.kernel, mesh=mesh,
        out_type=jax.ShapeDtypeStruct((n_rows, width), I32),
        scratch_types=[pltpu.VMEM((stage, SC_ROWS), I32),
                       pltpu.VMEM((SC_ROWS, width), I32), pltpu.VMEM((SC_ROWS, width), I32),
                       pltpu.SemaphoreType.DMA, pltpu.SemaphoreType.DMA,
                       pltpu.SemaphoreType.DMA, pltpu.SemaphoreType.DMA],
    )
    def gather_kernel(table_hbm, idx_hbm, out_hbm, idx_v, buf_a, buf_b, sem_ga, sem_gb, sem_wa, sem_wb):
        worker = lax.axis_index("s") * SC_CORES + lax.axis_index("c")
        first_chunk = worker * chunks_per_worker

        def gather(j, buf, sem):
            return pltpu.make_async_copy(table_hbm.at[idx_v.at[j]], buf, sem)

        def write(c, buf, sem):
            return pltpu.make_async_copy(buf, out_hbm.at[pl.ds(c * SC_ROWS, SC_ROWS)], sem)

        @pl.loop(0, chunks_per_worker // stage)
        def _(o):
            c0 = first_chunk + o * stage
            pltpu.sync_copy(idx_hbm.at[pl.ds(c0, stage)], idx_v)
            gather(0, buf_a, sem_ga).start()

            @pl.loop(0, stage, step=2)
            def _(j):
                gather(j + 1, buf_b, sem_gb).start()
                gather(j, buf_a, sem_ga).wait()
                write(c0 + j, buf_a, sem_wa).start()
                gather(j + 1, buf_b, sem_gb).wait()
                write(c0 + j + 1, buf_b, sem_wb).start()
                write(c0 + j, buf_a, sem_wa).wait()

                @pl.when(j + 2 < stage)
                def _():
                    gather(j + 2, buf_a, sem_ga).start()

                write(c0 + j + 1, buf_b, sem_wb).wait()

    return gather_kernel(table, idx.reshape(n_rows // SC_ROWS, SC_ROWS))


def _mix_body(rows_ref, x_ref, gate_ref, g_ref, b_ref, o_ref):
    tg = GATHER_TOKENS
    n_rows = tg * PEER_SLOTS
    x = x_ref[...]
    packed = rows_ref[...]
    ub = lax.bitcast_convert_type(packed << 16, F32).astype(BF16)
    vb = lax.bitcast_convert_type(packed & jnp.int32(-65536), F32).astype(BF16)
    res = _dot_nt(x.astype(BF16), ub)
    tok = lax.broadcasted_iota(I32, (tg, PEER_SLOTS), 0)
    act = jnp.zeros((tg, PEER_SLOTS), F32)
    for t in range(tg):
        act = act + jnp.where(tok == t, res[:, t * PEER_SLOTS:(t + 1) * PEER_SLOTS], 0.0)
    coef = gate_ref[...] * (act * 0.5 * (1.0 + lax.erf(act * (2.0 ** -0.5))))
    col_tok = lax.broadcasted_iota(I32, (tg, n_rows), 1) // PEER_SLOTS
    row_tok = lax.broadcasted_iota(I32, (tg, n_rows), 0)
    wide = jnp.where(col_tok == row_tok, jnp.concatenate([coef] * tg, axis=1), 0.0)
    out = _dot(wide.astype(BF16), vb)
    o_ref[...] = _layernorm(ALPHA * x + out, g_ref[...], b_ref[...])


def _mix(rows, h1, gate, g, b, first_token):
    tg = GATHER_TOKENS
    n = rows.shape[0] // PEER_SLOTS
    first = first_token // tg
    return pl.pallas_call(
        _mix_body,
        grid=(n // tg,),
        in_specs=[pl.BlockSpec((tg * PEER_SLOTS, D_MODEL), lambda i: (i, 0)),
                  pl.BlockSpec((tg, D_MODEL), lambda i: (first + i, 0)),
                  pl.BlockSpec((tg, PEER_SLOTS), lambda i: (first + i, 0)),
                  _full((1, D_MODEL)), _full((1, D_MODEL))],
        out_specs=pl.BlockSpec((tg, D_MODEL), lambda i: (i, 0)),
        out_shape=jax.ShapeDtypeStruct((n, D_MODEL), F32),
        compiler_params=_params(("parallel",)),
        name="peer_mix",
    )(rows, h1, gate, g, b)


def _peer(idx, gate, h1, table, g, b, n_parts):
    n = h1.shape[0]
    part = n // n_parts
    assert part * n_parts == n and part % GATHER_TOKENS == 0
    outs = []
    for i in range(n_parts):
        rows = _expert_rows(table, idx[i * part:(i + 1) * part].reshape(-1))
        outs.append(_mix(rows, h1, gate, g, b, i * part))
    return outs[0] if n_parts == 1 else jnp.concatenate(outs, axis=0)


def _group(x, shift0, wkv0, past, p):
    bsz, seq_len, _ = x.shape
    n = bsz * seq_len
    h, pr, q, k, v, lf = _in_proj(x.reshape(n, D_MODEL), p["ln_in_g"], p["ln_in_b"], p["w_r"], p["w_qkv"],
                                  p["w_f"], p["b_f"])
    pr = pr.reshape(bsz, seq_len, RWKV_IN)
    y_r, wkv_new = _rwkv(pr, shift0, wkv0, p)

    seq3 = lambda a: a.reshape(bsz, seq_len, -1)
    k3, v3, lf3 = seq3(k), seq3(v), seq3(lf)
    if past is None:
        k_all, v_all, lf_all, q_off, tq = k3, v3, lf3, 0, FOX_BLOCK
    else:
        k_past, v_past, lf_past = past
        q_off = k_past.shape[1]
        k_all = jnp.concatenate([k_past.reshape(bsz, q_off, D_HEADS), k3], axis=1)
        v_all = jnp.concatenate([v_past.reshape(bsz, q_off, D_HEADS), v3], axis=1)
        lf_all = jnp.concatenate([lf_past, lf3], axis=1)
        tq = seq_len
    c_col, c_row = _cumsum(lf_all)
    y_f = _fox(seq3(q), k_all, v_all, c_col, c_row, p["fox_norm_g"], q_off, tq, FOX_BLOCK)

    h1 = _out_proj(y_r.reshape(n, D_HEADS), y_f.reshape(n, D_HEADS), h, p["w_out_r"], p["w_out_f"],
                   p["ln1_g"], p["ln1_b"])
    idx_t, gate_t = _topk(h1, p["peer_w_q"], p["peer_keys1"], p["peer_keys2"])
    n_parts = PEER_PARTS if n % (PEER_PARTS * PEER_PART_QUANTUM) == 0 else 1
    y = _peer(idx_t.T, gate_t.T, h1, p["peer_uv"], p["ln2_g"], p["ln2_b"], n_parts)

    heads = lambda a: a.reshape(1, bsz, seq_len, N_HEADS, HEAD)
    return (y.reshape(bsz, seq_len, D_MODEL), pr[None, :, seq_len - 1:, :], wkv_new[None],
            heads(k), heads(v), lf3[None])


def _pack_bf16_pairs(lo, hi):
    bits = lambda a: lax.bitcast_convert_type(a.astype(BF16), jnp.uint16).astype(jnp.uint32)
    return lax.bitcast_convert_type(bits(lo) | (bits(hi) << 16), I32)


def kernel(x_prompt, x_sample, state_rwkv_shift, state_rwkv_wkv, cache_fox_k, cache_fox_v, cache_fox_logf,
           meta_tokens, ln_in_g, ln_in_b, w_in, rwkv_mix, rwkv_w0, rwkv_w2, rwkv_a0, rwkv_a2, rwkv_g2,
           rwkv_k_k, rwkv_k_a, rwkv_r_k, rwkv_lnx_g, rwkv_lnx_b, fox_b_f, fox_norm_g, w_out, ln1_g, ln1_b,
           peer_w_q, peer_keys1, peer_keys2, peer_u, peer_v, ln2_g, ln2_b):
    assert w_in.shape[0] == 1, "single-layer step"
    bsz = x_prompt.shape[0]
    row = lambda a: a.reshape(1, -1)
    w = w_in[0]
    fox0 = RWKV_IN
    pad_f = LANES - N_HEADS
    p = dict(
        ln_in_g=row(ln_in_g), ln_in_b=row(ln_in_b),
        w_r=w[:, :RWKV_IN].astype(BF16),
        w_qkv=w[:, fox0:fox0 + 3 * D_HEADS].astype(BF16),
        w_f=jnp.pad(w[:, fox0 + 3 * D_HEADS:], ((0, 0), (0, pad_f))).astype(BF16),
        b_f=jnp.pad(row(fox_b_f[0]), ((0, 0), (0, pad_f))),
        rwkv_mix=rwkv_mix[0], rwkv_w0=rwkv_w0[0], rwkv_w2=rwkv_w2[0], rwkv_a0=rwkv_a0[0], rwkv_a2=rwkv_a2[0],
        rwkv_g2=rwkv_g2[0], rwkv_k_k=rwkv_k_k[0], rwkv_k_a=rwkv_k_a[0], rwkv_r_k=rwkv_r_k[0],
        rwkv_lnx_g=rwkv_lnx_g[0], rwkv_lnx_b=rwkv_lnx_b[0],
        fox_norm_g=row(fox_norm_g[0]),
        w_out_r=w_out[0, :D_HEADS].astype(BF16), w_out_f=w_out[0, D_HEADS:].astype(BF16),
        ln1_g=row(ln1_g[0]), ln1_b=row(ln1_b[0]),
        peer_w_q=peer_w_q[0].astype(BF16), peer_keys1=peer_keys1[0], peer_keys2=peer_keys2[0],
        peer_uv=_pack_bf16_pairs(peer_u[0], peer_v[0]),
        ln2_g=row(ln2_g[0]), ln2_b=row(ln2_b[0]),
    )
    meta = jnp.broadcast_to(meta_tokens[None], (bsz, N_META, D_MODEL))
    hp0 = jnp.concatenate([meta, x_prompt], axis=1)
    zero_shift = jnp.zeros((bsz, 1, RWKV_IN), F32)
    zero_wkv = jnp.zeros((bsz, N_HEADS, HEAD, HEAD), F32)
    yp, p_shift, p_wkv, p_k, p_v, p_lf = _group(hp0, zero_shift, zero_wkv, None, p)
    ys, s_shift, s_wkv, s_k, s_v, s_lf = _group(
        x_sample, state_rwkv_shift[0], state_rwkv_wkv[0],
        (cache_fox_k[0], cache_fox_v[0], cache_fox_logf[0]), p)
    return (yp[:, N_META:], ys, p_shift, p_wkv, p_k, p_v, p_lf, s_shift, s_wkv, s_k, s_v, s_lf)
```

```python
import functools

import jax
import jax.numpy as jnp
from jax import lax
from jax.experimental import pallas as pl
from jax.experimental.pallas import tpu as pltpu
from jax.experimental.pallas import tpu_sc as plsc

F32 = jnp.float32
BF16 = jnp.bfloat16
I32 = jnp.int32

D_MODEL = 1024
N_HEADS = 8
HEAD = 64
D_HEADS = N_HEADS * HEAD
LORA_W, LORA_A, LORA_G = 64, 64, 128
RWKV_IN = 3 * D_HEADS + LORA_W + LORA_A + LORA_G
N_META = 16
LNX_EPS = 64e-5
LN_EPS = 1e-5
ALPHA = 2.0 ** 0.25
N_KEYS = 128
PEER_HEADS = 8
PEER_TOPK = 16
PEER_SLOTS = PEER_HEADS * PEER_TOPK
D_KEY_HALF = 128

LANES = 128
CHUNK = 64
TOKEN_BLOCK = 256
FOX_BLOCK = 256
FOX_KV_BLOCK = 512
GATHER_TOKENS = 8
SC_CORES, SC_SUBCORES = 2, 16
SC_ROWS = 32
SC_INDEX_STAGE = 64
PROMPT_PARTS = 4
VMEM_LIMIT = 48 * 1024 * 1024

HIGHEST = lax.Precision.HIGHEST


def _params(semantics):
    return pltpu.CompilerParams(dimension_semantics=semantics, vmem_limit_bytes=VMEM_LIMIT)


def _dot(a, b, precision=None):
    return jnp.dot(a, b, preferred_element_type=F32, precision=precision)


def _dot_nt(a, b, precision=None):
    return lax.dot_general(a, b, (((1,), (1,)), ((), ())), preferred_element_type=F32, precision=precision)


def _dot_tn(a, b, precision=None):
    return lax.dot_general(a, b, (((0,), (0,)), ((), ())), preferred_element_type=F32, precision=precision)


def _split3(a):
    a1 = a.astype(BF16)
    r1 = a - a1.astype(F32)
    a2 = r1.astype(BF16)
    a3 = (r1 - a2.astype(F32)).astype(BF16)
    return a1, a2, a3


def _split2(a):
    hi = a.astype(BF16)
    return hi, (a - hi.astype(F32)).astype(BF16)


def _mm3(dot, a, b):
    return dot(a[0], b[0]) + (dot(a[0], b[1]) + dot(a[1], b[0]))


def _dot_exact_rhs(a, m_bf16):
    a1, a2, a3 = _split3(a)
    return _dot(a1, m_bf16) + _dot(a2, m_bf16) + _dot(a3, m_bf16)


def _dot_exact_lhs(m_bf16, a):
    a1, a2, a3 = _split3(a)
    return _dot(m_bf16, a1) + _dot(m_bf16, a2) + _dot(m_bf16, a3)


def _layernorm(x, g, b):
    xc = x - jnp.mean(x, axis=-1, keepdims=True)
    var = jnp.mean(xc * xc, axis=-1, keepdims=True)
    return xc * lax.rsqrt(var + LN_EPS) * g + b


def _softplus(x):
    return jnp.maximum(x, 0.0) + jnp.log1p(jnp.exp(-jnp.abs(x)))


def _sigmoid(x):
    return 1.0 / (1.0 + jnp.exp(-x))


def _full(shape):
    return pl.BlockSpec(shape, lambda *_: (0,) * len(shape))


def _in_proj_body(x_ref, g_ref, b_ref, wr_ref, wqkv_ref, wf_ref, bf_ref,
                  h_ref, pr_ref, q_ref, k_ref, v_ref, lf_ref):
    h = _layernorm(x_ref[...], g_ref[...], b_ref[...])
    h_ref[...] = h
    hb = h.astype(BF16)
    pr_ref[...] = _dot(hb, wr_ref[...])
    qkv = _dot(hb, wqkv_ref[...])
    q_ref[...] = qkv[:, :D_HEADS]
    k_ref[...] = qkv[:, D_HEADS:2 * D_HEADS]
    v_ref[...] = qkv[:, 2 * D_HEADS:]
    fl = _dot(hb, wf_ref[...]) + bf_ref[...]
    lf = jnp.minimum(fl, 0.0) - jnp.log1p(jnp.exp(-jnp.abs(fl)))
    lf_ref[...] = lf[:, :N_HEADS]


def _in_proj(x, g, b, wr, wqkv, wf, bfp):
    n = x.shape[0]
    tb = TOKEN_BLOCK
    rows = lambda w: pl.BlockSpec((tb, w), lambda i: (i, 0))
    return pl.pallas_call(
        _in_proj_body,
        grid=(pl.cdiv(n, tb),),
        in_specs=[rows(D_MODEL), _full((1, D_MODEL)), _full((1, D_MODEL)), _full(wr.shape),
                  _full(wqkv.shape), _full(wf.shape), _full(bfp.shape)],
        out_specs=[rows(D_MODEL), rows(RWKV_IN), rows(D_HEADS), rows(D_HEADS), rows(D_HEADS), rows(N_HEADS)],
        out_shape=[jax.ShapeDtypeStruct((n, w), F32)
                   for w in (D_MODEL, RWKV_IN, D_HEADS, D_HEADS, D_HEADS, N_HEADS)],
        compiler_params=_params(("parallel",)),
        name="in_proj",
    )(x, g, b, wr, wqkv, wf, bfp)


def _rwkv_body(seq_len, n_chunks, pr_ref, shift0_ref, s0_ref, mix_ref, w0_ref, w2_ref, a0_ref, a2_ref,
               g2_ref, kk_ref, ka_ref, rk_ref, lng_ref, lnb_ref, seg_ref, tri_ref,
               y_ref, sout_ref, shift_sc, state_sc):
    c = pl.program_id(1)

    @pl.when(c == 0)
    def _():
        shift_sc[...] = shift0_ref[0]
        state_sc[...] = s0_ref[0]

    row = lax.broadcasted_iota(I32, (CHUNK, 1), 0)
    valid = (c * CHUNK + row) < seq_len
    prf = jnp.where(valid, pr_ref[0], 0.0)
    prev = jnp.where(row == 0, shift_sc[...], pltpu.roll(prf, 1, 0))
    shift_sc[...] = prf[CHUNK - 1:CHUNK, :]
    xm = prf + (prev - prf) * mix_ref[...]

    r = xm[:, :D_HEADS]
    k = xm[:, D_HEADS:2 * D_HEADS]
    v = xm[:, 2 * D_HEADS:3 * D_HEADS]
    o = 3 * D_HEADS
    xw = xm[:, o:o + LORA_W]
    xa = xm[:, o + LORA_W:o + LORA_W + LORA_A]
    xg = xm[:, o + LORA_W + LORA_A:]

    seg = seg_ref[...]
    wlog = -_softplus(-(w0_ref[...] + _dot(jnp.tanh(xw), w2_ref[...], HIGHEST))) - 0.5
    lw = jnp.where(valid, -jnp.exp(wlog), 0.0)
    a = _sigmoid(a0_ref[...] + _dot(xa, a2_ref[...], HIGHEST))
    g = _dot(_sigmoid(xg), g2_ref[...], HIGHEST)

    kk = k * kk_ref[...]
    kk_norm = jnp.sqrt(_dot_exact_rhs(kk * kk, seg))
    kk = jnp.where(valid, kk / jnp.maximum(kk_norm, 1e-12), 0.0)
    k = jnp.where(valid, k * (1.0 + (a - 1.0) * ka_ref[...]), 0.0)
    v = jnp.where(valid, v, 0.0)

    cl = _dot_exact_lhs(tri_ref[...], lw)
    dec_in = jnp.exp(cl)
    dec_out = jnp.exp(-cl)
    a_t = -kk * jnp.exp(cl - lw)
    b_t = kk * a * dec_out
    k_t = k * dec_out
    r_t = r * dec_in
    gamma = jnp.exp(cl[CHUNK - 1:CHUNK, :])

    ti = lax.broadcasted_iota(I32, (CHUNK, 2 * CHUNK), 0)
    si = lax.broadcasted_iota(I32, (CHUNK, 2 * CHUNK), 1)
    si = jnp.where(si >= CHUNK, si - CHUNK, si)
    strict = si < ti
    incl = si <= ti
    eye = (lax.broadcasted_iota(I32, (CHUNK, CHUNK), 0) == lax.broadcasted_iota(I32, (CHUNK, CHUNK), 1)).astype(F32)
    zeros = jnp.zeros((CHUNK, HEAD), F32)

    heads = range(N_HEADS)
    sls = [slice(h * HEAD, (h + 1) * HEAD) for h in heads]
    vh = [v[:, sl] for sl in sls]
    ar = [_split2(jnp.concatenate([a_t[:, sl], r_t[:, sl]], axis=0)) for sl in sls]
    bk = [_split2(jnp.concatenate([b_t[:, sl], k_t[:, sl]], axis=0)) for sl in sls]
    s0 = [state_sc[h] for h in heads]
    gram = [_mm3(_dot_nt, ar[h], bk[h]) for h in heads]
    top = [jnp.where(strict, gram[h][:CHUNK], 0.0) for h in heads]
    bot = [jnp.where(incl, gram[h][CHUNK:], 0.0) for h in heads]
    hs = [_mm3(_dot_nt, ar[h], _split2(s0[h])) for h in heads]
    rhs = [hs[h][:CHUNK] + _mm3(_dot, _split2(top[h]), _split2(jnp.concatenate([zeros, vh[h]], axis=0)))
           for h in heads]
    inv = [eye + top[h][:, :CHUNK] for h in heads]
    pws = [_split2(top[h][:, :CHUNK]) for h in heads]
    for _ in range(5):
        pws = [_split2(_mm3(_dot, pws[h], pws[h])) for h in heads]
        inv = [inv[h] + _mm3(_dot, _split2(inv[h]), pws[h]) for h in heads]
    u = [_mm3(_dot, _split2(inv[h]), _split2(rhs[h])) for h in heads]
    uv = [_split2(jnp.concatenate([u[h], vh[h]], axis=0)) for h in heads]
    ys = [hs[h][CHUNK:] + _dot(bot[h].astype(BF16), uv[h][0]) for h in heads]
    for h in heads:
        state_sc[h] = (s0[h] + _mm3(_dot_tn, uv[h], bk[h])) * gamma[:, sls[h]]

    y = jnp.concatenate(ys, axis=1)
    yc = y - _dot_exact_rhs(y, seg) * (1.0 / HEAD)
    var = _dot_exact_rhs(yc * yc, seg) * (1.0 / HEAD)
    y = yc * lax.rsqrt(var + LNX_EPS) * lng_ref[...] + lnb_ref[...]
    r = xm[:, :D_HEADS]
    bonus = _dot_exact_rhs(r * k * rk_ref[...], seg) * v
    y_ref[0] = (y + bonus) * g

    @pl.when(c == n_chunks - 1)
    def _():
        sout_ref[0] = state_sc[...]


def _rwkv(pr, shift0, wkv0, p):
    bsz, seq_len, _ = pr.shape
    n_chunks = pl.cdiv(seq_len, CHUNK)
    lane = jnp.arange(D_HEADS) // HEAD
    seg = (lane[:, None] == lane[None, :]).astype(BF16)
    tri = (jnp.arange(CHUNK)[None, :] <= jnp.arange(CHUNK)[:, None]).astype(BF16)
    vec = lambda a: a.reshape(1, -1)
    consts = [vec(p["rwkv_mix"]), vec(p["rwkv_w0"]), p["rwkv_w2"], vec(p["rwkv_a0"]), p["rwkv_a2"], p["rwkv_g2"],
              vec(p["rwkv_k_k"]), vec(p["rwkv_k_a"]), vec(p["rwkv_r_k"]), vec(p["rwkv_lnx_g"]),
              vec(p["rwkv_lnx_b"]), seg, tri]
    return pl.pallas_call(
        functools.partial(_rwkv_body, seq_len, n_chunks),
        grid=(bsz, n_chunks),
        in_specs=[pl.BlockSpec((1, CHUNK, RWKV_IN), lambda b, c: (b, c, 0)),
                  pl.BlockSpec((1, 1, RWKV_IN), lambda b, c: (b, 0, 0)),
                  pl.BlockSpec((1, N_HEADS, HEAD, HEAD), lambda b, c: (b, 0, 0, 0))]
                 + [_full(a.shape) for a in consts],
        out_specs=[pl.BlockSpec((1, CHUNK, D_HEADS), lambda b, c: (b, c, 0)),
                   pl.BlockSpec((1, N_HEADS, HEAD, HEAD), lambda b, c: (b, 0, 0, 0))],
        out_shape=[jax.ShapeDtypeStruct((bsz, seq_len, D_HEADS), F32),
                   jax.ShapeDtypeStruct((bsz, N_HEADS, HEAD, HEAD), F32)],
        scratch_shapes=[pltpu.VMEM((1, RWKV_IN), F32), pltpu.VMEM((N_HEADS, HEAD, HEAD), F32)],
        compiler_params=_params(("parallel", "arbitrary")),
        name="rwkv",
    )(pr, shift0, wkv0, *consts)


def _cumsum_body(seq_len, lf_ref, col_ref, row_ref):
    carry = jnp.zeros((1, N_HEADS), F32)
    eye = (lax.broadcasted_iota(I32, (N_HEADS, N_HEADS), 0)
           == lax.broadcasted_iota(I32, (N_HEADS, N_HEADS), 1)).astype(BF16)
    for start in range(0, seq_len, LANES):
        n = min(LANES, seq_len - start)
        tri = (lax.broadcasted_iota(I32, (n, n), 1) <= lax.broadcasted_iota(I32, (n, n), 0)).astype(BF16)
        cum = _dot_exact_lhs(tri, lf_ref[0, start:start + n, :]) + carry
        col_ref[0, start:start + n, :] = cum
        c1, c2, c3 = _split3(cum)
        row_ref[0, :, start:start + n] = _dot_nt(eye, c1) + _dot_nt(eye, c2) + _dot_nt(eye, c3)
        carry = cum[n - 1:n, :]


def _cumsum(lf):
    bsz, seq_len, _ = lf.shape
    return pl.pallas_call(
        functools.partial(_cumsum_body, seq_len),
        grid=(bsz,),
        in_specs=[pl.BlockSpec((1, seq_len, N_HEADS), lambda b: (b, 0, 0))],
        out_specs=[pl.BlockSpec((1, seq_len, N_HEADS), lambda b: (b, 0, 0)),
                   pl.BlockSpec((1, N_HEADS, seq_len), lambda b: (b, 0, 0))],
        out_shape=[jax.ShapeDtypeStruct((bsz, seq_len, N_HEADS), F32),
                   jax.ShapeDtypeStruct((bsz, N_HEADS, seq_len), F32)],
        compiler_params=_params(("parallel",)),
        name="cumsum",
    )(lf)


def _fox_body(kv_len, q_off, tq, tk, n_kv, q_ref, k_ref, v_ref, cq_ref, ck_ref, g_ref, o_ref, m_sc, l_sc, acc_sc):
    qi = pl.program_id(1)
    ki = pl.program_id(2)

    @pl.when(ki == 0)
    def _():
        m_sc[...] = jnp.full(m_sc.shape, -jnp.inf, F32)
        l_sc[...] = jnp.zeros(l_sc.shape, F32)
        acc_sc[...] = jnp.zeros(acc_sc.shape, F32)

    q_start = q_off + qi * tq
    k_start = ki * tk

    @pl.when(k_start <= q_start + tq - 1)
    def _():
        rows = q_start + lax.broadcasted_iota(I32, (tq, 1), 0)
        cols = k_start + lax.broadcasted_iota(I32, (1, tk), 1)
        mask = cols <= rows
        k_valid = (k_start + lax.broadcasted_iota(I32, (tk, 1), 0)) < kv_len
        q = q_ref[0] * (HEAD ** -0.5)
        kb = k_ref[0].astype(BF16)
        vb = jnp.where(k_valid, v_ref[0], 0.0).astype(BF16)
        cq = cq_ref[0]
        ck = ck_ref[0]
        ones = jnp.ones((tk, LANES), BF16)
        heads = range(N_HEADS)
        sls = [slice(h * HEAD, (h + 1) * HEAD) for h in heads]
        qb = q.astype(BF16)
        s = [_dot_nt(qb[:, sl], kb[:, sl]) for sl in sls]
        s = [jnp.where(mask, s[h] + (cq[:, h:h + 1] - ck[h:h + 1, :]), -jnp.inf) for h in heads]
        m_prev = [m_sc[h] for h in heads]
        m_new = [jnp.maximum(m_prev[h], jnp.max(s[h], axis=-1, keepdims=True)) for h in heads]
        scale = [jnp.exp(m_prev[h] - m_new[h]) for h in heads]
        pexp = [jnp.exp(s[h] - m_new[h]).astype(BF16) for h in heads]
        psum = [_dot(pexp[h], ones)[:, :1] for h in heads]
        pv = [_dot(pexp[h], vb[:, sls[h]]) for h in heads]
        for h in heads:
            l_sc[h] = scale[h] * l_sc[h] + psum[h]
            acc_sc[:, sls[h]] = scale[h] * acc_sc[:, sls[h]] + pv[h]
            m_sc[h] = m_new[h]

    @pl.when(ki == n_kv - 1)
    def _():
        for h in range(N_HEADS):
            sl = slice(h * HEAD, (h + 1) * HEAD)
            o = acc_sc[:, sl] / l_sc[h]
            o = o * lax.rsqrt(jnp.mean(o * o, axis=-1, keepdims=True) + 1e-6)
            acc_sc[:, sl] = o
        o_ref[0] = acc_sc[...] * g_ref[...]


def _fox(q, k_all, v_all, c_col, c_row, norm_g, q_off, tq, tk):
    bsz, q_len, _ = q.shape
    kv_len = k_all.shape[1]
    n_q = pl.cdiv(q_len, tq)
    n_kv = pl.cdiv(kv_len, tk)
    assert q_off % tq == 0
    last_kv = lambda qi: (q_off + (qi + 1) * tq - 1) // tk
    kv_map = lambda b, qi, ki: (b, jnp.minimum(ki, last_kv(qi)), 0)
    return pl.pallas_call(
        functools.partial(_fox_body, kv_len, q_off, tq, tk, n_kv),
        grid=(bsz, n_q, n_kv),
        in_specs=[pl.BlockSpec((1, tq, D_HEADS), lambda b, qi, ki: (b, qi, 0)),
                  pl.BlockSpec((1, tk, D_HEADS), kv_map),
                  pl.BlockSpec((1, tk, D_HEADS), kv_map),
                  pl.BlockSpec((1, tq, N_HEADS), lambda b, qi, ki: (b, q_off // tq + qi, 0)),
                  pl.BlockSpec((1, N_HEADS, tk), lambda b, qi, ki: (b, 0, jnp.minimum(ki, last_kv(qi)))),
                  _full((1, D_HEADS))],
        out_specs=pl.BlockSpec((1, tq, D_HEADS), lambda b, qi, ki: (b, qi, 0)),
        out_shape=jax.ShapeDtypeStruct((bsz, q_len, D_HEADS), F32),
        scratch_shapes=[pltpu.VMEM((N_HEADS, tq, 1), F32), pltpu.VMEM((N_HEADS, tq, 1), F32),
                        pltpu.VMEM((tq, D_HEADS), F32)],
        compiler_params=_params(("parallel", "parallel", "arbitrary")),
        name="fox",
    )(q, k_all, v_all, c_col, c_row, norm_g)


def _out_proj_body(yr_ref, yf_ref, h_ref, wr_ref, wf_ref, g_ref, b_ref, o_ref):
    mix = _dot(yr_ref[...].astype(BF16), wr_ref[...]) + _dot(yf_ref[...].astype(BF16), wf_ref[...])
    o_ref[...] = _layernorm(ALPHA * h_ref[...] + mix, g_ref[...], b_ref[...])


def _out_proj(yr, yf, h, wr, wf, g, b):
    n = h.shape[0]
    tb = TOKEN_BLOCK
    rows = lambda w: pl.BlockSpec((tb, w), lambda i: (i, 0))
    return pl.pallas_call(
        _out_proj_body,
        grid=(pl.cdiv(n, tb),),
        in_specs=[rows(D_HEADS), rows(D_HEADS), rows(D_MODEL), _full(wr.shape), _full(wf.shape),
                  _full((1, D_MODEL)), _full((1, D_MODEL))],
        out_specs=rows(D_MODEL),
        out_shape=jax.ShapeDtypeStruct((n, D_MODEL), F32),
        compiler_params=_params(("parallel",)),
        name="out_proj",
    )(yr, yf, h, wr, wf, g, b)


_BIG = 2 ** 30
_CAND_COUNTS = tuple(PEER_TOPK // (i + 1) for i in range(PEER_TOPK))
_N_CAND = sum(_CAND_COUNTS)
_CAND_ROWS = -(-_N_CAND // 8) * 8


def _top_rows(s, order, k, payload=None):
    vals, outs = [], []
    for _ in range(k):
        m = jnp.max(s, axis=0, keepdims=True)
        first = jnp.min(jnp.where(s == m, order, _BIG), axis=0, keepdims=True)
        sel = order == first
        vals.append(m)
        outs.append(first if payload is None else jnp.max(jnp.where(sel, payload, -1), axis=0, keepdims=True))
        s = jnp.where(sel, -jnp.inf, s)
    return jnp.concatenate(vals, axis=0), jnp.concatenate(outs, axis=0)


def _topk_body(h_ref, wq_ref, k1_ref, k2_ref, pos_ref, idx_ref, gate_ref, cand_s_sc, cand_i_sc):
    tb = h_ref.shape[0]
    q = _dot(h_ref[...].astype(BF16), wq_ref[...]).astype(BF16)
    key_id = lax.broadcasted_iota(I32, (N_KEYS, LANES), 0)
    cand_pos = pos_ref[...]
    k1 = k1_ref[...].astype(BF16)
    k2 = k2_ref[...].astype(BF16)
    cand_s_sc[_N_CAND:, :] = jnp.full((_CAND_ROWS - _N_CAND, LANES), -jnp.inf, F32)
    cand_i_sc[_N_CAND:, :] = jnp.zeros((_CAND_ROWS - _N_CAND, LANES), I32)
    for part in range(tb // LANES):
        qp = q[part * LANES:(part + 1) * LANES]
        ids, gates = [], []
        for h in range(PEER_HEADS):
            o = h * 2 * D_KEY_HALF
            s1 = _dot_nt(k1, qp[:, o:o + D_KEY_HALF])
            s2 = _dot_nt(k2, qp[:, o + D_KEY_HALF:o + 2 * D_KEY_HALF])
            v1, i1 = _top_rows(s1, key_id, PEER_TOPK)
            v2, i2 = _top_rows(s2, key_id, PEER_TOPK)
            off = 0
            for i, cnt in enumerate(_CAND_COUNTS):
                cand_s_sc[off:off + cnt, :] = v1[i:i + 1] + v2[:cnt]
                cand_i_sc[off:off + cnt, :] = i1[i:i + 1] * N_KEYS + i2[:cnt]
                off += cnt
            best_s, best_i = _top_rows(cand_s_sc[...], cand_pos, PEER_TOPK, payload=cand_i_sc[...])
            e = jnp.exp(best_s - best_s[0:1])
            ids.append(best_i)
            gates.append(e / jnp.sum(e, axis=0, keepdims=True))
        rows = slice(part * LANES, (part + 1) * LANES)
        idx_ref[rows, :] = jnp.concatenate(ids, axis=0).T
        gate_ref[rows, :] = jnp.concatenate(gates, axis=0).T


def _topk(h1, wq, keys1, keys2):
    n = h1.shape[0]
    tb = TOKEN_BLOCK
    pos = [i * PEER_TOPK + j for i, cnt in enumerate(_CAND_COUNTS) for j in range(cnt)]
    pos = jnp.asarray(pos + [_BIG] * (_CAND_ROWS - _N_CAND), I32)
    pos = jnp.broadcast_to(pos[:, None], (_CAND_ROWS, LANES))
    return pl.pallas_call(
        _topk_body,
        grid=(pl.cdiv(n, tb),),
        in_specs=[pl.BlockSpec((tb, D_MODEL), lambda i: (i, 0)), _full(wq.shape), _full(keys1.shape),
                  _full(keys2.shape), _full(pos.shape)],
        out_specs=[pl.BlockSpec((tb, PEER_SLOTS), lambda i: (i, 0)), pl.BlockSpec((tb, PEER_SLOTS), lambda i: (i, 0))],
        out_shape=[jax.ShapeDtypeStruct((n, PEER_SLOTS), I32), jax.ShapeDtypeStruct((n, PEER_SLOTS), F32)],
        scratch_shapes=[pltpu.VMEM((_CAND_ROWS, LANES), F32), pltpu.VMEM((_CAND_ROWS, LANES), I32)],
        compiler_params=_params(("parallel",)),
        name="peer_topk",
    )(h1, wq, keys1, keys2, pos)


def _expert_rows(table, idx):
    n_rows = idx.shape[0]
    width = table.shape[1]
    n_workers = SC_CORES * SC_SUBCORES
    assert n_rows % (SC_ROWS * n_workers) == 0
    chunks_per_worker = n_rows // (SC_ROWS * n_workers)
    stage = max(d for d in range(2, SC_INDEX_STAGE + 1, 2) if chunks_per_worker % d == 0)
    mesh = plsc.VectorSubcoreMesh(core_axis_name="c", subcore_axis_name="s",
                                  num_cores=SC_CORES, num_subcores=SC_SUBCORES)

    @functools.partial(
        pl.kernel, mesh=mesh,
        out_type=jax.ShapeDtypeStruct((n_rows, width), I32),
        scratch_types=[pltpu.VMEM((stage, SC_ROWS), I32),
                       pltpu.VMEM((SC_ROWS, width), I32), pltpu.VMEM((SC_ROWS, width), I32),
                       pltpu.SemaphoreType.DMA, pltpu.SemaphoreType.DMA,
                       pltpu.SemaphoreType.DMA, pltpu.SemaphoreType.DMA],
    )
    def gather_kernel(table_hbm, idx_hbm, out_hbm, idx_v, buf_a, buf_b, sem_ga, sem_gb, sem_wa, sem_wb):
        worker = lax.axis_index("s") * SC_CORES + lax.axis_index("c")
        first_chunk = worker * chunks_per_worker

        def gather(j, buf, sem):
            return pltpu.make_async_copy(table_hbm.at[idx_v.at[j]], buf, sem)

        def write(c, buf, sem):
            return pltpu.make_async_copy(buf, out_hbm.at[pl.ds(c * SC_ROWS, SC_ROWS)], sem)

        @pl.loop(0, chunks_per_worker // stage)
        def _(o):
            c0 = first_chunk + o * stage
            pltpu.sync_copy(idx_hbm.at[pl.ds(c0, stage)], idx_v)
            gather(0, buf_a, sem_ga).start()

            @pl.loop(0, stage, step=2)
            def _(j):
                gather(j + 1, buf_b, sem_gb).start()
                gather(j, buf_a, sem_ga).wait()
                write(c0 + j, buf_a, sem_wa).start()
                gather(j + 1, buf_b, sem_gb).wait()
                write(c0 + j + 1, buf_b, sem_wb).start()
                write(c0 + j, buf_a, sem_wa).wait()

                @pl.when(j + 2 < stage)
                def _():
                    gather(j + 2, buf_a, sem_ga).start()

                write(c0 + j + 1, buf_b, sem_wb).wait()

    return gather_kernel(table, idx.reshape(n_rows // SC_ROWS, SC_ROWS))


def _mix_body(rows_ref, x_ref, gate_ref, g_ref, b_ref, o_ref):
    tg = GATHER_TOKENS
    n_rows = tg * PEER_SLOTS
    x = x_ref[...]
    packed = rows_ref[...]
    ub = lax.bitcast_convert_type(packed << 16, F32).astype(BF16)
    vb = lax.bitcast_convert_type(packed & jnp.int32(-65536), F32).astype(BF16)
    res = _dot_nt(x.astype(BF16), ub)
    tok = lax.broadcasted_iota(I32, (tg, PEER_SLOTS), 0)
    act = jnp.zeros((tg, PEER_SLOTS), F32)
    for t in range(tg):
        act = act + jnp.where(tok == t, res[:, t * PEER_SLOTS:(t + 1) * PEER_SLOTS], 0.0)
    coef = gate_ref[...] * (act * 0.5 * (1.0 + lax.erf(act * (2.0 ** -0.5))))
    col_tok = lax.broadcasted_iota(I32, (tg, n_rows), 1) // PEER_SLOTS
    row_tok = lax.broadcasted_iota(I32, (tg, n_rows), 0)
    wide = jnp.where(col_tok == row_tok, jnp.concatenate([coef] * tg, axis=1), 0.0)
    out = _dot(wide.astype(BF16), vb)
    o_ref[...] = _layernorm(ALPHA * x + out, g_ref[...], b_ref[...])


def _mix(rows, h1, gate, g, b):
    tg = GATHER_TOKENS
    n = h1.shape[0]
    token_rows = lambda w: pl.BlockSpec((tg, w), lambda i: (i, 0))
    return pl.pallas_call(
        _mix_body,
        grid=(n // tg,),
        in_specs=[pl.BlockSpec((tg * PEER_SLOTS, D_MODEL), lambda i: (i, 0)), token_rows(D_MODEL),
                  token_rows(PEER_SLOTS), _full((1, D_MODEL)), _full((1, D_MODEL))],
        out_specs=token_rows(D_MODEL),
        out_shape=jax.ShapeDtypeStruct((n, D_MODEL), F32),
        compiler_params=_params(("parallel",)),
        name="peer_mix",
    )(rows, h1, gate, g, b)


def _group(x, shift0, wkv0, past, p):
    bsz, seq_len, _ = x.shape
    n = bsz * seq_len
    h, pr, q, k, v, lf = _in_proj(x.reshape(n, D_MODEL), p["ln_in_g"], p["ln_in_b"], p["w_r"], p["w_qkv"],
                                  p["w_f"], p["b_f"])
    pr = pr.reshape(bsz, seq_len, RWKV_IN)
    y_r, wkv_new = _rwkv(pr, shift0, wkv0, p)

    seq3 = lambda a: a.reshape(bsz, seq_len, -1)
    k3, v3, lf3 = seq3(k), seq3(v), seq3(lf)
    if past is None:
        k_all, v_all, lf_all, q_off, tq = k3, v3, lf3, 0, FOX_BLOCK
    else:
        k_past, v_past, lf_past = past
        q_off = k_past.shape[1]
        k_all = jnp.concatenate([k_past.reshape(bsz, q_off, D_HEADS), k3], axis=1)
        v_all = jnp.concatenate([v_past.reshape(bsz, q_off, D_HEADS), v3], axis=1)
        lf_all = jnp.concatenate([lf_past, lf3], axis=1)
        tq = seq_len
    c_col, c_row = _cumsum(lf_all)
    y_f = _fox(seq3(q), k_all, v_all, c_col, c_row, p["fox_norm_g"], q_off, tq, FOX_KV_BLOCK)

    h1 = _out_proj(y_r.reshape(n, D_HEADS), y_f.reshape(n, D_HEADS), h, p["w_out_r"], p["w_out_f"],
                   p["ln1_g"], p["ln1_b"])
    idx, gate = _topk(h1, p["peer_w_q"], p["peer_keys1"], p["peer_keys2"])
    rows = _expert_rows(p["peer_uv"], idx.reshape(-1))
    y = _mix(rows, h1, gate, p["ln2_g"], p["ln2_b"])

    heads = lambda a: a.reshape(1, bsz, seq_len, N_HEADS, HEAD)
    return (y.reshape(bsz, seq_len, D_MODEL), pr[None, :, seq_len - 1:, :], wkv_new[None],
            heads(k), heads(v), lf3[None])


def _pack_bf16_pairs(lo, hi):
    bits = lambda a: lax.bitcast_convert_type(a.astype(BF16), jnp.uint16).astype(jnp.uint32)
    return lax.bitcast_convert_type(bits(lo) | (bits(hi) << 16), I32)


def kernel(x_prompt, x_sample, state_rwkv_shift, state_rwkv_wkv, cache_fox_k, cache_fox_v, cache_fox_logf,
           meta_tokens, ln_in_g, ln_in_b, w_in, rwkv_mix, rwkv_w0, rwkv_w2, rwkv_a0, rwkv_a2, rwkv_g2,
           rwkv_k_k, rwkv_k_a, rwkv_r_k, rwkv_lnx_g, rwkv_lnx_b, fox_b_f, fox_norm_g, w_out, ln1_g, ln1_b,
           peer_w_q, peer_keys1, peer_keys2, peer_u, peer_v, ln2_g, ln2_b):
    assert w_in.shape[0] == 1, "single-layer step"
    bsz = x_prompt.shape[0]
    row = lambda a: a.reshape(1, -1)
    w = w_in[0]
    fox0 = RWKV_IN
    pad_f = LANES - N_HEADS
    p = dict(
        ln_in_g=row(ln_in_g), ln_in_b=row(ln_in_b),
        w_r=w[:, :RWKV_IN].astype(BF16),
        w_qkv=w[:, fox0:fox0 + 3 * D_HEADS].astype(BF16),
        w_f=jnp.pad(w[:, fox0 + 3 * D_HEADS:], ((0, 0), (0, pad_f))).astype(BF16),
        b_f=jnp.pad(row(fox_b_f[0]), ((0, 0), (0, pad_f))),
        rwkv_mix=rwkv_mix[0], rwkv_w0=rwkv_w0[0], rwkv_w2=rwkv_w2[0], rwkv_a0=rwkv_a0[0], rwkv_a2=rwkv_a2[0],
        rwkv_g2=rwkv_g2[0], rwkv_k_k=rwkv_k_k[0], rwkv_k_a=rwkv_k_a[0], rwkv_r_k=rwkv_r_k[0],
        rwkv_lnx_g=rwkv_lnx_g[0], rwkv_lnx_b=rwkv_lnx_b[0],
        fox_norm_g=row(fox_norm_g[0]),
        w_out_r=w_out[0, :D_HEADS].astype(BF16), w_out_f=w_out[0, D_HEADS:].astype(BF16),
        ln1_g=row(ln1_g[0]), ln1_b=row(ln1_b[0]),
        peer_w_q=peer_w_q[0].astype(BF16), peer_keys1=peer_keys1[0], peer_keys2=peer_keys2[0],
        peer_uv=_pack_bf16_pairs(peer_u[0], peer_v[0]),
        ln2_g=row(ln2_g[0]), ln2_b=row(ln2_b[0]),
    )
    n_parts = PROMPT_PARTS if bsz % PROMPT_PARTS == 0 else 1
    pb = bsz // n_parts
    meta = jnp.broadcast_to(meta_tokens[None], (pb, N_META, D_MODEL))
    zero_shift = jnp.zeros((pb, 1, RWKV_IN), F32)
    zero_wkv = jnp.zeros((pb, N_HEADS, HEAD, HEAD), F32)
    parts = []
    for i in range(n_parts):
        hp0 = jnp.concatenate([meta, x_prompt[i * pb:(i + 1) * pb]], axis=1)
        yp, *state = _group(hp0, zero_shift, zero_wkv, None, p)
        parts.append((yp[:, N_META:], *state))
    ys, s_shift, s_wkv, s_k, s_v, s_lf = _group(
        x_sample, state_rwkv_shift[0], state_rwkv_wkv[0],
        (cache_fox_k[0], cache_fox_v[0], cache_fox_logf[0]), p)
    yp = jnp.concatenate([t[0] for t in parts], axis=0)
    p_shift, p_wkv, p_k, p_v, p_lf = (jnp.concatenate([t[j] for t in parts], axis=1) for j in range(1, 6))
    return (yp, ys, p_shift, p_wkv, p_k, p_v, p_lf, s_shift, s_wkv, s_k, s_v, s_lf)
```

```python
import functools

import jax
import jax.numpy as jnp
from jax import lax
from jax.experimental import pallas as pl
from jax.experimental.pallas import tpu as pltpu
from jax.experimental.pallas import tpu_sc as plsc

F32 = jnp.float32
BF16 = jnp.bfloat16
I32 = jnp.int32

D_MODEL = 1024
N_HEADS = 8
HEAD = 64
D_HEADS = N_HEADS * HEAD
LORA_W, LORA_A, LORA_G = 64, 64, 128
RWKV_IN = 3 * D_HEADS + LORA_W + LORA_A + LORA_G
N_META = 16
LNX_EPS = 64e-5
LN_EPS = 1e-5
ALPHA = 2.0 ** 0.25
N_KEYS = 128
PEER_HEADS = 8
PEER_TOPK = 16
PEER_SLOTS = PEER_HEADS * PEER_TOPK
D_KEY_HALF = 128

LANES = 128
CHUNK = 64
TOKEN_BLOCK = 256
FOX_BLOCK = 256
FOX_KV_BLOCK = 512
GATHER_TOKENS = 8
SC_CORES, SC_SUBCORES = 2, 16
SC_ROWS_CHOICES = (32, 16)
SC_INDEX_STAGE = 64
PROMPT_PARTS = 8
PIPELINE_LAG = 3
VMEM_LIMIT = 48 * 1024 * 1024

HIGHEST = lax.Precision.HIGHEST


def _params(semantics):
    return pltpu.CompilerParams(dimension_semantics=semantics, vmem_limit_bytes=VMEM_LIMIT)


def _dot(a, b, precision=None):
    return jnp.dot(a, b, preferred_element_type=F32, precision=precision)


def _dot_nt(a, b, precision=None):
    return lax.dot_general(a, b, (((1,), (1,)), ((), ())), preferred_element_type=F32, precision=precision)


def _dot_tn(a, b, precision=None):
    return lax.dot_general(a, b, (((0,), (0,)), ((), ())), preferred_element_type=F32, precision=precision)


def _split3(a):
    a1 = a.astype(BF16)
    r1 = a - a1.astype(F32)
    a2 = r1.astype(BF16)
    a3 = (r1 - a2.astype(F32)).astype(BF16)
    return a1, a2, a3


def _split2(a):
    hi = a.astype(BF16)
    return hi, (a - hi.astype(F32)).astype(BF16)


def _mm3(dot, a, b):
    return dot(a[0], b[0]) + (dot(a[0], b[1]) + dot(a[1], b[0]))


def _dot_exact_rhs(a, m_bf16):
    a1, a2, a3 = _split3(a)
    return _dot(a1, m_bf16) + _dot(a2, m_bf16) + _dot(a3, m_bf16)


def _dot_exact_lhs(m_bf16, a):
    a1, a2, a3 = _split3(a)
    return _dot(m_bf16, a1) + _dot(m_bf16, a2) + _dot(m_bf16, a3)


def _layernorm(x, g, b):
    xc = x - jnp.mean(x, axis=-1, keepdims=True)
    var = jnp.mean(xc * xc, axis=-1, keepdims=True)
    return xc * lax.rsqrt(var + LN_EPS) * g + b


def _softplus(x):
    return jnp.maximum(x, 0.0) + jnp.log1p(jnp.exp(-jnp.abs(x)))


def _sigmoid(x):
    return 1.0 / (1.0 + jnp.exp(-x))


def _full(shape):
    return pl.BlockSpec(shape, lambda *_: (0,) * len(shape))


def _in_proj_body(x_ref, g_ref, b_ref, wr_ref, wqkv_ref, wf_ref, bf_ref,
                  h_ref, pr_ref, q_ref, k_ref, v_ref, lf_ref):
    h = _layernorm(x_ref[...], g_ref[...], b_ref[...])
    h_ref[...] = h
    hb = h.astype(BF16)
    pr_ref[...] = _dot(hb, wr_ref[...])
    qkv = _dot(hb, wqkv_ref[...])
    q_ref[...] = qkv[:, :D_HEADS]
    k_ref[...] = qkv[:, D_HEADS:2 * D_HEADS]
    v_ref[...] = qkv[:, 2 * D_HEADS:]
    fl = _dot(hb, wf_ref[...]) + bf_ref[...]
    lf = jnp.minimum(fl, 0.0) - jnp.log1p(jnp.exp(-jnp.abs(fl)))
    lf_ref[...] = lf[:, :N_HEADS]


def _in_proj(x, g, b, wr, wqkv, wf, bfp):
    n = x.shape[0]
    tb = TOKEN_BLOCK
    rows = lambda w: pl.BlockSpec((tb, w), lambda i: (i, 0))
    return pl.pallas_call(
        _in_proj_body,
        grid=(pl.cdiv(n, tb),),
        in_specs=[rows(D_MODEL), _full((1, D_MODEL)), _full((1, D_MODEL)), _full(wr.shape),
                  _full(wqkv.shape), _full(wf.shape), _full(bfp.shape)],
        out_specs=[rows(D_MODEL), rows(RWKV_IN), rows(D_HEADS), rows(D_HEADS), rows(D_HEADS), rows(N_HEADS)],
        out_shape=[jax.ShapeDtypeStruct((n, w), F32)
                   for w in (D_MODEL, RWKV_IN, D_HEADS, D_HEADS, D_HEADS, N_HEADS)],
        compiler_params=_params(("parallel",)),
        name="in_proj",
    )(x, g, b, wr, wqkv, wf, bfp)


def _rwkv_body(seq_len, n_chunks, pr_ref, shift0_ref, s0_ref, mix_ref, w0_ref, w2_ref, a0_ref, a2_ref,
               g2_ref, kk_ref, ka_ref, rk_ref, lng_ref, lnb_ref, seg_ref, tri_ref,
               y_ref, sout_ref, shift_sc, state_sc):
    c = pl.program_id(1)

    @pl.when(c == 0)
    def _():
        shift_sc[...] = shift0_ref[0]
        state_sc[...] = s0_ref[0]

    row = lax.broadcasted_iota(I32, (CHUNK, 1), 0)
    valid = (c * CHUNK + row) < seq_len
    prf = jnp.where(valid, pr_ref[0], 0.0)
    prev = jnp.where(row == 0, shift_sc[...], pltpu.roll(prf, 1, 0))
    shift_sc[...] = prf[CHUNK - 1:CHUNK, :]
    xm = prf + (prev - prf) * mix_ref[...]

    r = xm[:, :D_HEADS]
    k = xm[:, D_HEADS:2 * D_HEADS]
    v = xm[:, 2 * D_HEADS:3 * D_HEADS]
    o = 3 * D_HEADS
    xw = xm[:, o:o + LORA_W]
    xa = xm[:, o + LORA_W:o + LORA_W + LORA_A]
    xg = xm[:, o + LORA_W + LORA_A:]

    seg = seg_ref[...]
    wlog = -_softplus(-(w0_ref[...] + _dot(jnp.tanh(xw), w2_ref[...], HIGHEST))) - 0.5
    lw = jnp.where(valid, -jnp.exp(wlog), 0.0)
    a = _sigmoid(a0_ref[...] + _dot(xa, a2_ref[...], HIGHEST))
    g = _dot(_sigmoid(xg), g2_ref[...], HIGHEST)

    kk = k * kk_ref[...]
    kk_norm = jnp.sqrt(_dot_exact_rhs(kk * kk, seg))
    kk = jnp.where(valid, kk / jnp.maximum(kk_norm, 1e-12), 0.0)
    k = jnp.where(valid, k * (1.0 + (a - 1.0) * ka_ref[...]), 0.0)
    v = jnp.where(valid, v, 0.0)

    cl = _dot_exact_lhs(tri_ref[...], lw)
    dec_in = jnp.exp(cl)
    dec_out = jnp.exp(-cl)
    a_t = -kk * jnp.exp(cl - lw)
    b_t = kk * a * dec_out
    k_t = k * dec_out
    r_t = r * dec_in
    gamma = jnp.exp(cl[CHUNK - 1:CHUNK, :])

    ti = lax.broadcasted_iota(I32, (CHUNK, 2 * CHUNK), 0)
    si = lax.broadcasted_iota(I32, (CHUNK, 2 * CHUNK), 1)
    si = jnp.where(si >= CHUNK, si - CHUNK, si)
    strict = si < ti
    incl = si <= ti
    eye = (lax.broadcasted_iota(I32, (CHUNK, CHUNK), 0) == lax.broadcasted_iota(I32, (CHUNK, CHUNK), 1)).astype(F32)
    zeros = jnp.zeros((CHUNK, HEAD), F32)

    heads = range(N_HEADS)
    sls = [slice(h * HEAD, (h + 1) * HEAD) for h in heads]
    vh = [v[:, sl] for sl in sls]
    ar = [_split2(jnp.concatenate([a_t[:, sl], r_t[:, sl]], axis=0)) for sl in sls]
    bk = [_split2(jnp.concatenate([b_t[:, sl], k_t[:, sl]], axis=0)) for sl in sls]
    s0 = [state_sc[h] for h in heads]
    gram = [_mm3(_dot_nt, ar[h], bk[h]) for h in heads]
    top = [jnp.where(strict, gram[h][:CHUNK], 0.0) for h in heads]
    bot = [jnp.where(incl, gram[h][CHUNK:], 0.0) for h in heads]
    hs = [_mm3(_dot_nt, ar[h], _split2(s0[h])) for h in heads]
    rhs = [hs[h][:CHUNK] + _mm3(_dot, _split2(top[h]), _split2(jnp.concatenate([zeros, vh[h]], axis=0)))
           for h in heads]
    inv = [eye + top[h][:, :CHUNK] for h in heads]
    pws = [_split2(top[h][:, :CHUNK]) for h in heads]
    for _ in range(5):
        pws = [_split2(_mm3(_dot, pws[h], pws[h])) for h in heads]
        inv = [inv[h] + _mm3(_dot, _split2(inv[h]), pws[h]) for h in heads]
    u = [_mm3(_dot, _split2(inv[h]), _split2(rhs[h])) for h in heads]
    uv = [_split2(jnp.concatenate([u[h], vh[h]], axis=0)) for h in heads]
    ys = [hs[h][CHUNK:] + _dot(bot[h].astype(BF16), uv[h][0]) for h in heads]
    for h in heads:
        state_sc[h] = (s0[h] + _mm3(_dot_tn, uv[h], bk[h])) * gamma[:, sls[h]]

    y = jnp.concatenate(ys, axis=1)
    yc = y - _dot_exact_rhs(y, seg) * (1.0 / HEAD)
    var = _dot_exact_rhs(yc * yc, seg) * (1.0 / HEAD)
    y = yc * lax.rsqrt(var + LNX_EPS) * lng_ref[...] + lnb_ref[...]
    r = xm[:, :D_HEADS]
    bonus = _dot_exact_rhs(r * k * rk_ref[...], seg) * v
    y_ref[0] = (y + bonus) * g

    @pl.when(c == n_chunks - 1)
    def _():
        sout_ref[0] = state_sc[...]


def _rwkv(pr, shift0, wkv0, p):
    bsz, seq_len, _ = pr.shape
    n_chunks = pl.cdiv(seq_len, CHUNK)
    lane = jnp.arange(D_HEADS) // HEAD
    seg = (lane[:, None] == lane[None, :]).astype(BF16)
    tri = (jnp.arange(CHUNK)[None, :] <= jnp.arange(CHUNK)[:, None]).astype(BF16)
    vec = lambda a: a.reshape(1, -1)
    consts = [vec(p["rwkv_mix"]), vec(p["rwkv_w0"]), p["rwkv_w2"], vec(p["rwkv_a0"]), p["rwkv_a2"], p["rwkv_g2"],
              vec(p["rwkv_k_k"]), vec(p["rwkv_k_a"]), vec(p["rwkv_r_k"]), vec(p["rwkv_lnx_g"]),
              vec(p["rwkv_lnx_b"]), seg, tri]
    return pl.pallas_call(
        functools.partial(_rwkv_body, seq_len, n_chunks),
        grid=(bsz, n_chunks),
        in_specs=[pl.BlockSpec((1, CHUNK, RWKV_IN), lambda b, c: (b, c, 0)),
                  pl.BlockSpec((1, 1, RWKV_IN), lambda b, c: (b, 0, 0)),
                  pl.BlockSpec((1, N_HEADS, HEAD, HEAD), lambda b, c: (b, 0, 0, 0))]
                 + [_full(a.shape) for a in consts],
        out_specs=[pl.BlockSpec((1, CHUNK, D_HEADS), lambda b, c: (b, c, 0)),
                   pl.BlockSpec((1, N_HEADS, HEAD, HEAD), lambda b, c: (b, 0, 0, 0))],
        out_shape=[jax.ShapeDtypeStruct((bsz, seq_len, D_HEADS), F32),
                   jax.ShapeDtypeStruct((bsz, N_HEADS, HEAD, HEAD), F32)],
        scratch_shapes=[pltpu.VMEM((1, RWKV_IN), F32), pltpu.VMEM((N_HEADS, HEAD, HEAD), F32)],
        compiler_params=_params(("parallel", "arbitrary")),
        name="rwkv",
    )(pr, shift0, wkv0, *consts)


def _cumsum_body(seq_len, lf_ref, col_ref, row_ref):
    carry = jnp.zeros((1, N_HEADS), F32)
    eye = (lax.broadcasted_iota(I32, (N_HEADS, N_HEADS), 0)
           == lax.broadcasted_iota(I32, (N_HEADS, N_HEADS), 1)).astype(BF16)
    for start in range(0, seq_len, LANES):
        n = min(LANES, seq_len - start)
        tri = (lax.broadcasted_iota(I32, (n, n), 1) <= lax.broadcasted_iota(I32, (n, n), 0)).astype(BF16)
        cum = _dot_exact_lhs(tri, lf_ref[0, start:start + n, :]) + carry
        col_ref[0, start:start + n, :] = cum
        c1, c2, c3 = _split3(cum)
        row_ref[0, :, start:start + n] = _dot_nt(eye, c1) + _dot_nt(eye, c2) + _dot_nt(eye, c3)
        carry = cum[n - 1:n, :]


def _cumsum(lf):
    bsz, seq_len, _ = lf.shape
    return pl.pallas_call(
        functools.partial(_cumsum_body, seq_len),
        grid=(bsz,),
        in_specs=[pl.BlockSpec((1, seq_len, N_HEADS), lambda b: (b, 0, 0))],
        out_specs=[pl.BlockSpec((1, seq_len, N_HEADS), lambda b: (b, 0, 0)),
                   pl.BlockSpec((1, N_HEADS, seq_len), lambda b: (b, 0, 0))],
        out_shape=[jax.ShapeDtypeStruct((bsz, seq_len, N_HEADS), F32),
                   jax.ShapeDtypeStruct((bsz, N_HEADS, seq_len), F32)],
        compiler_params=_params(("parallel",)),
        name="cumsum",
    )(lf)


def _fox_body(kv_len, q_off, tq, tk, n_kv, q_ref, k_ref, v_ref, cq_ref, ck_ref, g_ref, o_ref, m_sc, l_sc, acc_sc):
    qi = pl.program_id(1)
    ki = pl.program_id(2)

    @pl.when(ki == 0)
    def _():
        m_sc[...] = jnp.full(m_sc.shape, -jnp.inf, F32)
        l_sc[...] = jnp.zeros(l_sc.shape, F32)
        acc_sc[...] = jnp.zeros(acc_sc.shape, F32)

    q_start = q_off + qi * tq
    k_start = ki * tk

    @pl.when(k_start <= q_start + tq - 1)
    def _():
        rows = q_start + lax.broadcasted_iota(I32, (tq, 1), 0)
        cols = k_start + lax.broadcasted_iota(I32, (1, tk), 1)
        mask = cols <= rows
        k_valid = (k_start + lax.broadcasted_iota(I32, (tk, 1), 0)) < kv_len
        q = q_ref[0] * (HEAD ** -0.5)
        kb = k_ref[0].astype(BF16)
        vb = jnp.where(k_valid, v_ref[0], 0.0).astype(BF16)
        cq = cq_ref[0]
        ck = ck_ref[0]
        ones = jnp.ones((tk, LANES), BF16)
        heads = range(N_HEADS)
        sls = [slice(h * HEAD, (h + 1) * HEAD) for h in heads]
        qb = q.astype(BF16)
        s = [_dot_nt(qb[:, sl], kb[:, sl]) for sl in sls]
        s = [jnp.where(mask, s[h] + (cq[:, h:h + 1] - ck[h:h + 1, :]), -jnp.inf) for h in heads]
        m_prev = [m_sc[h] for h in heads]
        m_new = [jnp.maximum(m_prev[h], jnp.max(s[h], axis=-1, keepdims=True)) for h in heads]
        scale = [jnp.exp(m_prev[h] - m_new[h]) for h in heads]
        pexp = [jnp.exp(s[h] - m_new[h]).astype(BF16) for h in heads]
        psum = [_dot(pexp[h], ones)[:, :1] for h in heads]
        pv = [_dot(pexp[h], vb[:, sls[h]]) for h in heads]
        for h in heads:
            l_sc[h] = scale[h] * l_sc[h] + psum[h]
            acc_sc[:, sls[h]] = scale[h] * acc_sc[:, sls[h]] + pv[h]
            m_sc[h] = m_new[h]

    @pl.when(ki == n_kv - 1)
    def _():
        for h in range(N_HEADS):
            sl = slice(h * HEAD, (h + 1) * HEAD)
            o = acc_sc[:, sl] / l_sc[h]
            o = o * lax.rsqrt(jnp.mean(o * o, axis=-1, keepdims=True) + 1e-6)
            acc_sc[:, sl] = o
        o_ref[0] = acc_sc[...] * g_ref[...]


def _fox(q, k_all, v_all, c_col, c_row, norm_g, q_off, tq, tk):
    bsz, q_len, _ = q.shape
    kv_len = k_all.shape[1]
    n_q = pl.cdiv(q_len, tq)
    n_kv = pl.cdiv(kv_len, tk)
    assert q_off % tq == 0
    last_kv = lambda qi: (q_off + (qi + 1) * tq - 1) // tk
    kv_map = lambda b, qi, ki: (b, jnp.minimum(ki, last_kv(qi)), 0)
    return pl.pallas_call(
        functools.partial(_fox_body, kv_len, q_off, tq, tk, n_kv),
        grid=(bsz, n_q, n_kv),
        in_specs=[pl.BlockSpec((1, tq, D_HEADS), lambda b, qi, ki: (b, qi, 0)),
                  pl.BlockSpec((1, tk, D_HEADS), kv_map),
                  pl.BlockSpec((1, tk, D_HEADS), kv_map),
                  pl.BlockSpec((1, tq, N_HEADS), lambda b, qi, ki: (b, q_off // tq + qi, 0)),
                  pl.BlockSpec((1, N_HEADS, tk), lambda b, qi, ki: (b, 0, jnp.minimum(ki, last_kv(qi)))),
                  _full((1, D_HEADS))],
        out_specs=pl.BlockSpec((1, tq, D_HEADS), lambda b, qi, ki: (b, qi, 0)),
        out_shape=jax.ShapeDtypeStruct((bsz, q_len, D_HEADS), F32),
        scratch_shapes=[pltpu.VMEM((N_HEADS, tq, 1), F32), pltpu.VMEM((N_HEADS, tq, 1), F32),
                        pltpu.VMEM((tq, D_HEADS), F32)],
        compiler_params=_params(("parallel", "parallel", "arbitrary")),
        name="fox",
    )(q, k_all, v_all, c_col, c_row, norm_g)


def _out_proj_body(yr_ref, yf_ref, h_ref, wr_ref, wf_ref, g_ref, b_ref, o_ref):
    mix = _dot(yr_ref[...].astype(BF16), wr_ref[...]) + _dot(yf_ref[...].astype(BF16), wf_ref[...])
    o_ref[...] = _layernorm(ALPHA * h_ref[...] + mix, g_ref[...], b_ref[...])


def _out_proj(yr, yf, h, wr, wf, g, b):
    n = h.shape[0]
    tb = TOKEN_BLOCK
    rows = lambda w: pl.BlockSpec((tb, w), lambda i: (i, 0))
    return pl.pallas_call(
        _out_proj_body,
        grid=(pl.cdiv(n, tb),),
        in_specs=[rows(D_HEADS), rows(D_HEADS), rows(D_MODEL), _full(wr.shape), _full(wf.shape),
                  _full((1, D_MODEL)), _full((1, D_MODEL))],
        out_specs=rows(D_MODEL),
        out_shape=jax.ShapeDtypeStruct((n, D_MODEL), F32),
        compiler_params=_params(("parallel",)),
        name="out_proj",
    )(yr, yf, h, wr, wf, g, b)


_BIG = 2 ** 30
_CAND_COUNTS = tuple(PEER_TOPK // (i + 1) for i in range(PEER_TOPK))
_N_CAND = sum(_CAND_COUNTS)
_CAND_ROWS = -(-_N_CAND // 8) * 8


def _top_rows(s, order, k, payload=None):
    vals, outs = [], []
    for _ in range(k):
        m = jnp.max(s, axis=0, keepdims=True)
        first = jnp.min(jnp.where(s == m, order, _BIG), axis=0, keepdims=True)
        sel = order == first
        vals.append(m)
        outs.append(first if payload is None else jnp.max(jnp.where(sel, payload, -1), axis=0, keepdims=True))
        s = jnp.where(sel, -jnp.inf, s)
    return jnp.concatenate(vals, axis=0), jnp.concatenate(outs, axis=0)


def _topk_body(h_ref, wq_ref, k1_ref, k2_ref, pos_ref, idx_ref, gate_ref, cand_s_sc, cand_i_sc):
    tb = h_ref.shape[0]
    q = _dot(h_ref[...].astype(BF16), wq_ref[...]).astype(BF16)
    key_id = lax.broadcasted_iota(I32, (N_KEYS, LANES), 0)
    cand_pos = pos_ref[...]
    k1 = k1_ref[...].astype(BF16)
    k2 = k2_ref[...].astype(BF16)
    cand_s_sc[_N_CAND:, :] = jnp.full((_CAND_ROWS - _N_CAND, LANES), -jnp.inf, F32)
    cand_i_sc[_N_CAND:, :] = jnp.zeros((_CAND_ROWS - _N_CAND, LANES), I32)
    for part in range(tb // LANES):
        qp = q[part * LANES:(part + 1) * LANES]
        ids, gates = [], []
        for h in range(PEER_HEADS):
            o = h * 2 * D_KEY_HALF
            s1 = _dot_nt(k1, qp[:, o:o + D_KEY_HALF])
            s2 = _dot_nt(k2, qp[:, o + D_KEY_HALF:o + 2 * D_KEY_HALF])
            v1, i1 = _top_rows(s1, key_id, PEER_TOPK)
            v2, i2 = _top_rows(s2, key_id, PEER_TOPK)
            off = 0
            for i, cnt in enumerate(_CAND_COUNTS):
                cand_s_sc[off:off + cnt, :] = v1[i:i + 1] + v2[:cnt]
                cand_i_sc[off:off + cnt, :] = i1[i:i + 1] * N_KEYS + i2[:cnt]
                off += cnt
            best_s, best_i = _top_rows(cand_s_sc[...], cand_pos, PEER_TOPK, payload=cand_i_sc[...])
            e = jnp.exp(best_s - best_s[0:1])
            ids.append(best_i)
            gates.append(e / jnp.sum(e, axis=0, keepdims=True))
        rows = slice(part * LANES, (part + 1) * LANES)
        idx_ref[rows, :] = jnp.concatenate(ids, axis=0).T
        gate_ref[rows, :] = jnp.concatenate(gates, axis=0).T


def _topk(h1, wq, keys1, keys2):
    n = h1.shape[0]
    tb = TOKEN_BLOCK
    pos = [i * PEER_TOPK + j for i, cnt in enumerate(_CAND_COUNTS) for j in range(cnt)]
    pos = jnp.asarray(pos + [_BIG] * (_CAND_ROWS - _N_CAND), I32)
    pos = jnp.broadcast_to(pos[:, None], (_CAND_ROWS, LANES))
    return pl.pallas_call(
        _topk_body,
        grid=(pl.cdiv(n, tb),),
        in_specs=[pl.BlockSpec((tb, D_MODEL), lambda i: (i, 0)), _full(wq.shape), _full(keys1.shape),
                  _full(keys2.shape), _full(pos.shape)],
        out_specs=[pl.BlockSpec((tb, PEER_SLOTS), lambda i: (i, 0)), pl.BlockSpec((tb, PEER_SLOTS), lambda i: (i, 0))],
        out_shape=[jax.ShapeDtypeStruct((n, PEER_SLOTS), I32), jax.ShapeDtypeStruct((n, PEER_SLOTS), F32)],
        scratch_shapes=[pltpu.VMEM((_CAND_ROWS, LANES), F32), pltpu.VMEM((_CAND_ROWS, LANES), I32)],
        compiler_params=_params(("parallel",)),
        name="peer_topk",
    )(h1, wq, keys1, keys2, pos)


def _expert_rows(table, idx):
    n_rows = idx.shape[0]
    width = table.shape[1]
    n_workers = SC_CORES * SC_SUBCORES
    sc_rows = next(r for r in SC_ROWS_CHOICES if n_rows % (r * n_workers * 8) == 0)
    chunks_per_worker = n_rows // (sc_rows * n_workers)
    stage = max(d for d in range(8, SC_INDEX_STAGE + 1, 8) if chunks_per_worker % d == 0)
    mesh = plsc.VectorSubcoreMesh(core_axis_name="c", subcore_axis_name="s",
                                  num_cores=SC_CORES, num_subcores=SC_SUBCORES)

    @functools.partial(
        pl.kernel, mesh=mesh,
        out_type=jax.ShapeDtypeStruct((n_rows, width), I32),
        scratch_types=[pltpu.VMEM((stage, sc_rows), I32),
                       pltpu.VMEM((sc_rows, width), I32), pltpu.VMEM((sc_rows, width), I32),
                       pltpu.SemaphoreType.DMA, pltpu.SemaphoreType.DMA,
                       pltpu.SemaphoreType.DMA, pltpu.SemaphoreType.DMA],
    )
    def gather_kernel(table_hbm, idx_hbm, out_hbm, idx_v, buf_a, buf_b, sem_ga, sem_gb, sem_wa, sem_wb):
        worker = lax.axis_index("s") * SC_CORES + lax.axis_index("c")
        first_chunk = worker * chunks_per_worker

        def gather(j, buf, sem):
            return pltpu.make_async_copy(table_hbm.at[idx_v.at[j]], buf, sem)

        def write(c, buf, sem):
            return pltpu.make_async_copy(buf, out_hbm.at[pl.ds(c * sc_rows, sc_rows)], sem)

        @pl.loop(0, chunks_per_worker // stage)
        def _(o):
            c0 = pl.multiple_of(first_chunk + o * stage, 8)
            pltpu.sync_copy(idx_hbm.at[pl.ds(c0, stage)], idx_v)
            gather(0, buf_a, sem_ga).start()

            @pl.loop(0, stage, step=2)
            def _(j):
                gather(j + 1, buf_b, sem_gb).start()
                gather(j, buf_a, sem_ga).wait()
                write(c0 + j, buf_a, sem_wa).start()
                gather(j + 1, buf_b, sem_gb).wait()
                write(c0 + j + 1, buf_b, sem_wb).start()
                write(c0 + j, buf_a, sem_wa).wait()

                @pl.when(j + 2 < stage)
                def _():
                    gather(j + 2, buf_a, sem_ga).start()

                write(c0 + j + 1, buf_b, sem_wb).wait()

    return gather_kernel(table, idx.reshape(n_rows // sc_rows, sc_rows))


def _mix_body(rows_ref, x_ref, gate_ref, g_ref, b_ref, o_ref):
    tg = GATHER_TOKENS
    n_rows = tg * PEER_SLOTS
    x = x_ref[...]
    packed = rows_ref[...]
    ub = lax.bitcast_convert_type(packed << 16, F32).astype(BF16)
    vb = lax.bitcast_convert_type(packed & jnp.int32(-65536), F32).astype(BF16)
    res = _dot_nt(x.astype(BF16), ub)
    tok = lax.broadcasted_iota(I32, (tg, PEER_SLOTS), 0)
    act = jnp.zeros((tg, PEER_SLOTS), F32)
    for t in range(tg):
        act = act + jnp.where(tok == t, res[:, t * PEER_SLOTS:(t + 1) * PEER_SLOTS], 0.0)
    coef = gate_ref[...] * (act * 0.5 * (1.0 + lax.erf(act * (2.0 ** -0.5))))
    col_tok = lax.broadcasted_iota(I32, (tg, n_rows), 1) // PEER_SLOTS
    row_tok = lax.broadcasted_iota(I32, (tg, n_rows), 0)
    wide = jnp.where(col_tok == row_tok, jnp.concatenate([coef] * tg, axis=1), 0.0)
    out = _dot(wide.astype(BF16), vb)
    o_ref[...] = _layernorm(ALPHA * x + out, g_ref[...], b_ref[...])


def _mix(rows, h1, gate, g, b):
    tg = GATHER_TOKENS
    n = h1.shape[0]
    token_rows = lambda w: pl.BlockSpec((tg, w), lambda i: (i, 0))
    return pl.pallas_call(
        _mix_body,
        grid=(n // tg,),
        in_specs=[pl.BlockSpec((tg * PEER_SLOTS, D_MODEL), lambda i: (i, 0)), token_rows(D_MODEL),
                  token_rows(PEER_SLOTS), _full((1, D_MODEL)), _full((1, D_MODEL))],
        out_specs=token_rows(D_MODEL),
        out_shape=jax.ShapeDtypeStruct((n, D_MODEL), F32),
        compiler_params=_params(("parallel",)),
        name="peer_mix",
    )(rows, h1, gate, g, b)


def _group(x, shift0, wkv0, past, p):
    bsz, seq_len, _ = x.shape
    n = bsz * seq_len
    h, pr, q, k, v, lf = _in_proj(x.reshape(n, D_MODEL), p["ln_in_g"], p["ln_in_b"], p["w_r"], p["w_qkv"],
                                  p["w_f"], p["b_f"])
    pr = pr.reshape(bsz, seq_len, RWKV_IN)
    y_r, wkv_new = _rwkv(pr, shift0, wkv0, p)

    seq3 = lambda a: a.reshape(bsz, seq_len, -1)
    k3, v3, lf3 = seq3(k), seq3(v), seq3(lf)
    if past is None:
        k_all, v_all, lf_all, q_off, tq = k3, v3, lf3, 0, FOX_BLOCK
    else:
        k_past, v_past, lf_past = past
        q_off = k_past.shape[1]
        k_all = jnp.concatenate([k_past.reshape(bsz, q_off, D_HEADS), k3], axis=1)
        v_all = jnp.concatenate([v_past.reshape(bsz, q_off, D_HEADS), v3], axis=1)
        lf_all = jnp.concatenate([lf_past, lf3], axis=1)
        tq = seq_len
    c_col, c_row = _cumsum(lf_all)
    y_f = _fox(seq3(q), k_all, v_all, c_col, c_row, p["fox_norm_g"], q_off, tq, FOX_KV_BLOCK)

    h1 = _out_proj(y_r.reshape(n, D_HEADS), y_f.reshape(n, D_HEADS), h, p["w_out_r"], p["w_out_f"],
                   p["ln1_g"], p["ln1_b"])
    idx, gate = _topk(h1, p["peer_w_q"], p["peer_keys1"], p["peer_keys2"])
    rows = _expert_rows(p["peer_uv"], idx.reshape(-1))
    y = _mix(rows, h1, gate, p["ln2_g"], p["ln2_b"])

    heads = lambda a: a.reshape(1, bsz, seq_len, N_HEADS, HEAD)
    return (y.reshape(bsz, seq_len, D_MODEL), pr[None, :, seq_len - 1:, :], wkv_new[None],
            heads(k), heads(v), lf3[None])


def _pack_bf16_pairs(lo, hi):
    bits = lambda a: lax.bitcast_convert_type(a.astype(BF16), jnp.uint16).astype(jnp.uint32)
    return lax.bitcast_convert_type(bits(lo) | (bits(hi) << 16), I32)


def kernel(x_prompt, x_sample, state_rwkv_shift, state_rwkv_wkv, cache_fox_k, cache_fox_v, cache_fox_logf,
           meta_tokens, ln_in_g, ln_in_b, w_in, rwkv_mix, rwkv_w0, rwkv_w2, rwkv_a0, rwkv_a2, rwkv_g2,
           rwkv_k_k, rwkv_k_a, rwkv_r_k, rwkv_lnx_g, rwkv_lnx_b, fox_b_f, fox_norm_g, w_out, ln1_g, ln1_b,
           peer_w_q, peer_keys1, peer_keys2, peer_u, peer_v, ln2_g, ln2_b):
    assert w_in.shape[0] == 1, "single-layer step"
    bsz = x_prompt.shape[0]
    row = lambda a: a.reshape(1, -1)
    w = w_in[0]
    fox0 = RWKV_IN
    pad_f = LANES - N_HEADS
    p = dict(
        ln_in_g=row(ln_in_g), ln_in_b=row(ln_in_b),
        w_r=w[:, :RWKV_IN].astype(BF16),
        w_qkv=w[:, fox0:fox0 + 3 * D_HEADS].astype(BF16),
        w_f=jnp.pad(w[:, fox0 + 3 * D_HEADS:], ((0, 0), (0, pad_f))).astype(BF16),
        b_f=jnp.pad(row(fox_b_f[0]), ((0, 0), (0, pad_f))),
        rwkv_mix=rwkv_mix[0], rwkv_w0=rwkv_w0[0], rwkv_w2=rwkv_w2[0], rwkv_a0=rwkv_a0[0], rwkv_a2=rwkv_a2[0],
        rwkv_g2=rwkv_g2[0], rwkv_k_k=rwkv_k_k[0], rwkv_k_a=rwkv_k_a[0], rwkv_r_k=rwkv_r_k[0],
        rwkv_lnx_g=rwkv_lnx_g[0], rwkv_lnx_b=rwkv_lnx_b[0],
        fox_norm_g=row(fox_norm_g[0]),
        w_out_r=w_out[0, :D_HEADS].astype(BF16), w_out_f=w_out[0, D_HEADS:].astype(BF16),
        ln1_g=row(ln1_g[0]), ln1_b=row(ln1_b[0]),
        peer_w_q=peer_w_q[0].astype(BF16), peer_keys1=peer_keys1[0], peer_keys2=peer_keys2[0],
        peer_uv=_pack_bf16_pairs(peer_u[0], peer_v[0]),
        ln2_g=row(ln2_g[0]), ln2_b=row(ln2_b[0]),
    )
    n_parts = PROMPT_PARTS if bsz % PROMPT_PARTS == 0 else 1
    pb = bsz // n_parts
    meta = jnp.broadcast_to(meta_tokens[None], (pb, N_META, D_MODEL))
    zero_shift = jnp.zeros((pb, 1, RWKV_IN), F32)
    zero_wkv = jnp.zeros((pb, N_HEADS, HEAD, HEAD), F32)
    parts = []
    for i in range(n_parts):
        x_i = x_prompt[i * pb:(i + 1) * pb]
        if i >= PIPELINE_LAG:
            j = i - PIPELINE_LAG
            x_i, y_done = lax.optimization_barrier((x_i, parts[j][0]))
            parts[j] = (y_done, *parts[j][1:])
        hp0 = jnp.concatenate([meta, x_i], axis=1)
        yp, *state = _group(hp0, zero_shift, zero_wkv, None, p)
        parts.append((yp[:, N_META:], *state))
    ys, s_shift, s_wkv, s_k, s_v, s_lf = _group(
        x_sample, state_rwkv_shift[0], state_rwkv_wkv[0],
        (cache_fox_k[0], cache_fox_v[0], cache_fox_logf[0]), p)
    yp = jnp.concatenate([t[0] for t in parts], axis=0)
    p_shift, p_wkv, p_k, p_v, p_lf = (jnp.concatenate([t[j] for t in parts], axis=1) for j in range(1, 6))
    return (yp, ys, p_shift, p_wkv, p_k, p_v, p_lf, s_shift, s_wkv, s_k, s_v, s_lf)
```

```python
import functools

import jax
import jax.numpy as jnp
from jax import lax
from jax.experimental import pallas as pl
from jax.experimental.pallas import tpu as pltpu
from jax.experimental.pallas import tpu_sc as plsc

F32 = jnp.float32
BF16 = jnp.bfloat16
I32 = jnp.int32

D_MODEL = 1024
N_HEADS = 8
HEAD = 64
D_HEADS = N_HEADS * HEAD
LORA_W, LORA_A, LORA_G = 64, 64, 128
RWKV_IN = 3 * D_HEADS + LORA_W + LORA_A + LORA_G
N_META = 16
LNX_EPS = 64e-5
LN_EPS = 1e-5
ALPHA = 2.0 ** 0.25
N_KEYS = 128
PEER_HEADS = 8
PEER_TOPK = 16
PEER_SLOTS = PEER_HEADS * PEER_TOPK
D_KEY_HALF = 128

LANES = 128
CHUNK = 64
TOKEN_BLOCK = 256
FOX_BLOCK = 256
FOX_KV_BLOCK = 512
SC_CORES, SC_SUBCORES, SC_LANES = 2, 16, 16
SC_HALF = PEER_SLOTS // 2
SC_UNIT = 8
PROMPT_PARTS = 8
VMEM_LIMIT = 48 * 1024 * 1024

HIGHEST = lax.Precision.HIGHEST


def _params(semantics):
    return pltpu.CompilerParams(dimension_semantics=semantics, vmem_limit_bytes=VMEM_LIMIT)


def _dot(a, b, precision=None):
    return jnp.dot(a, b, preferred_element_type=F32, precision=precision)


def _dot_nt(a, b, precision=None):
    return lax.dot_general(a, b, (((1,), (1,)), ((), ())), preferred_element_type=F32, precision=precision)


def _dot_tn(a, b, precision=None):
    return lax.dot_general(a, b, (((0,), (0,)), ((), ())), preferred_element_type=F32, precision=precision)


def _split3(a):
    a1 = a.astype(BF16)
    r1 = a - a1.astype(F32)
    a2 = r1.astype(BF16)
    a3 = (r1 - a2.astype(F32)).astype(BF16)
    return a1, a2, a3


def _split2(a):
    hi = a.astype(BF16)
    return hi, (a - hi.astype(F32)).astype(BF16)


def _mm3(dot, a, b):
    return dot(a[0], b[0]) + (dot(a[0], b[1]) + dot(a[1], b[0]))


def _dot_exact_rhs(a, m_bf16):
    a1, a2, a3 = _split3(a)
    return _dot(a1, m_bf16) + _dot(a2, m_bf16) + _dot(a3, m_bf16)


def _dot_exact_lhs(m_bf16, a):
    a1, a2, a3 = _split3(a)
    return _dot(m_bf16, a1) + _dot(m_bf16, a2) + _dot(m_bf16, a3)


def _layernorm(x, g, b):
    xc = x - jnp.mean(x, axis=-1, keepdims=True)
    var = jnp.mean(xc * xc, axis=-1, keepdims=True)
    return xc * lax.rsqrt(var + LN_EPS) * g + b


def _softplus(x):
    return jnp.maximum(x, 0.0) + jnp.log1p(jnp.exp(-jnp.abs(x)))


def _sigmoid(x):
    return 1.0 / (1.0 + jnp.exp(-x))


def _full(shape):
    return pl.BlockSpec(shape, lambda *_: (0,) * len(shape))


def _in_proj_body(x_ref, g_ref, b_ref, wr_ref, wqkv_ref, wf_ref, bf_ref,
                  h_ref, pr_ref, q_ref, k_ref, v_ref, lf_ref):
    h = _layernorm(x_ref[...], g_ref[...], b_ref[...])
    h_ref[...] = h
    hb = h.astype(BF16)
    pr_ref[...] = _dot(hb, wr_ref[...])
    qkv = _dot(hb, wqkv_ref[...])
    q_ref[...] = qkv[:, :D_HEADS]
    k_ref[...] = qkv[:, D_HEADS:2 * D_HEADS]
    v_ref[...] = qkv[:, 2 * D_HEADS:]
    fl = _dot(hb, wf_ref[...]) + bf_ref[...]
    lf = jnp.minimum(fl, 0.0) - jnp.log1p(jnp.exp(-jnp.abs(fl)))
    lf_ref[...] = lf[:, :N_HEADS]


def _in_proj(x, g, b, wr, wqkv, wf, bfp):
    n = x.shape[0]
    tb = TOKEN_BLOCK
    rows = lambda w: pl.BlockSpec((tb, w), lambda i: (i, 0))
    return pl.pallas_call(
        _in_proj_body,
        grid=(pl.cdiv(n, tb),),
        in_specs=[rows(D_MODEL), _full((1, D_MODEL)), _full((1, D_MODEL)), _full(wr.shape),
                  _full(wqkv.shape), _full(wf.shape), _full(bfp.shape)],
        out_specs=[rows(D_MODEL), rows(RWKV_IN), rows(D_HEADS), rows(D_HEADS), rows(D_HEADS), rows(N_HEADS)],
        out_shape=[jax.ShapeDtypeStruct((n, w), F32)
                   for w in (D_MODEL, RWKV_IN, D_HEADS, D_HEADS, D_HEADS, N_HEADS)],
        compiler_params=_params(("parallel",)),
        name="in_proj",
    )(x, g, b, wr, wqkv, wf, bfp)


def _rwkv_body(seq_len, n_chunks, pr_ref, shift0_ref, s0_ref, mix_ref, w0_ref, w2_ref, a0_ref, a2_ref,
               g2_ref, kk_ref, ka_ref, rk_ref, lng_ref, lnb_ref, seg_ref, tri_ref,
               y_ref, sout_ref, shift_sc, state_sc):
    c = pl.program_id(1)

    @pl.when(c == 0)
    def _():
        shift_sc[...] = shift0_ref[0]
        state_sc[...] = s0_ref[0]

    row = lax.broadcasted_iota(I32, (CHUNK, 1), 0)
    valid = (c * CHUNK + row) < seq_len
    prf = jnp.where(valid, pr_ref[0], 0.0)
    prev = jnp.where(row == 0, shift_sc[...], pltpu.roll(prf, 1, 0))
    shift_sc[...] = prf[CHUNK - 1:CHUNK, :]
    xm = prf + (prev - prf) * mix_ref[...]

    r = xm[:, :D_HEADS]
    k = xm[:, D_HEADS:2 * D_HEADS]
    v = xm[:, 2 * D_HEADS:3 * D_HEADS]
    o = 3 * D_HEADS
    xw = xm[:, o:o + LORA_W]
    xa = xm[:, o + LORA_W:o + LORA_W + LORA_A]
    xg = xm[:, o + LORA_W + LORA_A:]

    seg = seg_ref[...]
    wlog = -_softplus(-(w0_ref[...] + _dot(jnp.tanh(xw), w2_ref[...], HIGHEST))) - 0.5
    lw = jnp.where(valid, -jnp.exp(wlog), 0.0)
    a = _sigmoid(a0_ref[...] + _dot(xa, a2_ref[...], HIGHEST))
    g = _dot(_sigmoid(xg), g2_ref[...], HIGHEST)

    kk = k * kk_ref[...]
    kk_norm = jnp.sqrt(_dot_exact_rhs(kk * kk, seg))
    kk = jnp.where(valid, kk / jnp.maximum(kk_norm, 1e-12), 0.0)
    k = jnp.where(valid, k * (1.0 + (a - 1.0) * ka_ref[...]), 0.0)
    v = jnp.where(valid, v, 0.0)

    cl = _dot_exact_lhs(tri_ref[...], lw)
    dec_in = jnp.exp(cl)
    dec_out = jnp.exp(-cl)
    a_t = -kk * jnp.exp(cl - lw)
    b_t = kk * a * dec_out
    k_t = k * dec_out
    r_t = r * dec_in
    gamma = jnp.exp(cl[CHUNK - 1:CHUNK, :])

    ti = lax.broadcasted_iota(I32, (CHUNK, 2 * CHUNK), 0)
    si = lax.broadcasted_iota(I32, (CHUNK, 2 * CHUNK), 1)
    si = jnp.where(si >= CHUNK, si - CHUNK, si)
    strict = si < ti
    incl = si <= ti
    eye = (lax.broadcasted_iota(I32, (CHUNK, CHUNK), 0) == lax.broadcasted_iota(I32, (CHUNK, CHUNK), 1)).astype(F32)
    zeros = jnp.zeros((CHUNK, HEAD), F32)

    heads = range(N_HEADS)
    sls = [slice(h * HEAD, (h + 1) * HEAD) for h in heads]
    vh = [v[:, sl] for sl in sls]
    ar = [_split2(jnp.concatenate([a_t[:, sl], r_t[:, sl]], axis=0)) for sl in sls]
    bk = [_split2(jnp.concatenate([b_t[:, sl], k_t[:, sl]], axis=0)) for sl in sls]
    s0 = [state_sc[h] for h in heads]
    gram = [_mm3(_dot_nt, ar[h], bk[h]) for h in heads]
    top = [jnp.where(strict, gram[h][:CHUNK], 0.0) for h in heads]
    bot = [jnp.where(incl, gram[h][CHUNK:], 0.0) for h in heads]
    hs = [_mm3(_dot_nt, ar[h], _split2(s0[h])) for h in heads]
    rhs = [hs[h][:CHUNK] + _mm3(_dot, _split2(top[h]), _split2(jnp.concatenate([zeros, vh[h]], axis=0)))
           for h in heads]
    inv = [eye + top[h][:, :CHUNK] for h in heads]
    pws = [_split2(top[h][:, :CHUNK]) for h in heads]
    for _ in range(5):
        pws = [_split2(_mm3(_dot, pws[h], pws[h])) for h in heads]
        inv = [inv[h] + _mm3(_dot, _split2(inv[h]), pws[h]) for h in heads]
    u = [_mm3(_dot, _split2(inv[h]), _split2(rhs[h])) for h in heads]
    uv = [_split2(jnp.concatenate([u[h], vh[h]], axis=0)) for h in heads]
    ys = [hs[h][CHUNK:] + _dot(bot[h].astype(BF16), uv[h][0]) for h in heads]
    for h in heads:
        state_sc[h] = (s0[h] + _mm3(_dot_tn, uv[h], bk[h])) * gamma[:, sls[h]]

    y = jnp.concatenate(ys, axis=1)
    yc = y - _dot_exact_rhs(y, seg) * (1.0 / HEAD)
    var = _dot_exact_rhs(yc * yc, seg) * (1.0 / HEAD)
    y = yc * lax.rsqrt(var + LNX_EPS) * lng_ref[...] + lnb_ref[...]
    r = xm[:, :D_HEADS]
    bonus = _dot_exact_rhs(r * k * rk_ref[...], seg) * v
    y_ref[0] = (y + bonus) * g

    @pl.when(c == n_chunks - 1)
    def _():
        sout_ref[0] = state_sc[...]


def _rwkv(pr, shift0, wkv0, p):
    bsz, seq_len, _ = pr.shape
    n_chunks = pl.cdiv(seq_len, CHUNK)
    lane = jnp.arange(D_HEADS) // HEAD
    seg = (lane[:, None] == lane[None, :]).astype(BF16)
    tri = (jnp.arange(CHUNK)[None, :] <= jnp.arange(CHUNK)[:, None]).astype(BF16)
    vec = lambda a: a.reshape(1, -1)
    consts = [vec(p["rwkv_mix"]), vec(p["rwkv_w0"]), p["rwkv_w2"], vec(p["rwkv_a0"]), p["rwkv_a2"], p["rwkv_g2"],
              vec(p["rwkv_k_k"]), vec(p["rwkv_k_a"]), vec(p["rwkv_r_k"]), vec(p["rwkv_lnx_g"]),
              vec(p["rwkv_lnx_b"]), seg, tri]
    return pl.pallas_call(
        functools.partial(_rwkv_body, seq_len, n_chunks),
        grid=(bsz, n_chunks),
        in_specs=[pl.BlockSpec((1, CHUNK, RWKV_IN), lambda b, c: (b, c, 0)),
                  pl.BlockSpec((1, 1, RWKV_IN), lambda b, c: (b, 0, 0)),
                  pl.BlockSpec((1, N_HEADS, HEAD, HEAD), lambda b, c: (b, 0, 0, 0))]
                 + [_full(a.shape) for a in consts],
        out_specs=[pl.BlockSpec((1, CHUNK, D_HEADS), lambda b, c: (b, c, 0)),
                   pl.BlockSpec((1, N_HEADS, HEAD, HEAD), lambda b, c: (b, 0, 0, 0))],
        out_shape=[jax.ShapeDtypeStruct((bsz, seq_len, D_HEADS), F32),
                   jax.ShapeDtypeStruct((bsz, N_HEADS, HEAD, HEAD), F32)],
        scratch_shapes=[pltpu.VMEM((1, RWKV_IN), F32), pltpu.VMEM((N_HEADS, HEAD, HEAD), F32)],
        compiler_params=_params(("parallel", "arbitrary")),
        name="rwkv",
    )(pr, shift0, wkv0, *consts)


def _cumsum_body(seq_len, lf_ref, col_ref, row_ref):
    carry = jnp.zeros((1, N_HEADS), F32)
    eye = (lax.broadcasted_iota(I32, (N_HEADS, N_HEADS), 0)
           == lax.broadcasted_iota(I32, (N_HEADS, N_HEADS), 1)).astype(BF16)
    for start in range(0, seq_len, LANES):
        n = min(LANES, seq_len - start)
        tri = (lax.broadcasted_iota(I32, (n, n), 1) <= lax.broadcasted_iota(I32, (n, n), 0)).astype(BF16)
        cum = _dot_exact_lhs(tri, lf_ref[0, start:start + n, :]) + carry
        col_ref[0, start:start + n, :] = cum
        c1, c2, c3 = _split3(cum)
        row_ref[0, :, start:start + n] = _dot_nt(eye, c1) + _dot_nt(eye, c2) + _dot_nt(eye, c3)
        carry = cum[n - 1:n, :]


def _cumsum(lf):
    bsz, seq_len, _ = lf.shape
    return pl.pallas_call(
        functools.partial(_cumsum_body, seq_len),
        grid=(bsz,),
        in_specs=[pl.BlockSpec((1, seq_len, N_HEADS), lambda b: (b, 0, 0))],
        out_specs=[pl.BlockSpec((1, seq_len, N_HEADS), lambda b: (b, 0, 0)),
                   pl.BlockSpec((1, N_HEADS, seq_len), lambda b: (b, 0, 0))],
        out_shape=[jax.ShapeDtypeStruct((bsz, seq_len, N_HEADS), F32),
                   jax.ShapeDtypeStruct((bsz, N_HEADS, seq_len), F32)],
        compiler_params=_params(("parallel",)),
        name="cumsum",
    )(lf)


def _fox_body(kv_len, q_off, tq, tk, n_kv, q_ref, k_ref, v_ref, cq_ref, ck_ref, g_ref, o_ref, m_sc, l_sc, acc_sc):
    qi = pl.program_id(1)
    ki = pl.program_id(2)

    @pl.when(ki == 0)
    def _():
        m_sc[...] = jnp.full(m_sc.shape, -jnp.inf, F32)
        l_sc[...] = jnp.zeros(l_sc.shape, F32)
        acc_sc[...] = jnp.zeros(acc_sc.shape, F32)

    q_start = q_off + qi * tq
    k_start = ki * tk

    @pl.when(k_start <= q_start + tq - 1)
    def _():
        rows = q_start + lax.broadcasted_iota(I32, (tq, 1), 0)
        cols = k_start + lax.broadcasted_iota(I32, (1, tk), 1)
        mask = cols <= rows
        k_valid = (k_start + lax.broadcasted_iota(I32, (tk, 1), 0)) < kv_len
        q = q_ref[0] * (HEAD ** -0.5)
        kb = k_ref[0].astype(BF16)
        vb = jnp.where(k_valid, v_ref[0], 0.0).astype(BF16)
        cq = cq_ref[0]
        ck = ck_ref[0]
        ones = jnp.ones((tk, LANES), BF16)
        heads = range(N_HEADS)
        sls = [slice(h * HEAD, (h + 1) * HEAD) for h in heads]
        qb = q.astype(BF16)
        s = [_dot_nt(qb[:, sl], kb[:, sl]) for sl in sls]
        s = [jnp.where(mask, s[h] + (cq[:, h:h + 1] - ck[h:h + 1, :]), -jnp.inf) for h in heads]
        m_prev = [m_sc[h] for h in heads]
        m_new = [jnp.maximum(m_prev[h], jnp.max(s[h], axis=-1, keepdims=True)) for h in heads]
        scale = [jnp.exp(m_prev[h] - m_new[h]) for h in heads]
        pexp = [jnp.exp(s[h] - m_new[h]).astype(BF16) for h in heads]
        psum = [_dot(pexp[h], ones)[:, :1] for h in heads]
        pv = [_dot(pexp[h], vb[:, sls[h]]) for h in heads]
        for h in heads:
            l_sc[h] = scale[h] * l_sc[h] + psum[h]
            acc_sc[:, sls[h]] = scale[h] * acc_sc[:, sls[h]] + pv[h]
            m_sc[h] = m_new[h]

    @pl.when(ki == n_kv - 1)
    def _():
        for h in range(N_HEADS):
            sl = slice(h * HEAD, (h + 1) * HEAD)
            o = acc_sc[:, sl] / l_sc[h]
            o = o * lax.rsqrt(jnp.mean(o * o, axis=-1, keepdims=True) + 1e-6)
            acc_sc[:, sl] = o
        o_ref[0] = acc_sc[...] * g_ref[...]


def _fox(q, k_all, v_all, c_col, c_row, norm_g, q_off, tq, tk):
    bsz, q_len, _ = q.shape
    kv_len = k_all.shape[1]
    n_q = pl.cdiv(q_len, tq)
    n_kv = pl.cdiv(kv_len, tk)
    assert q_off % tq == 0
    last_kv = lambda qi: (q_off + (qi + 1) * tq - 1) // tk
    kv_map = lambda b, qi, ki: (b, jnp.minimum(ki, last_kv(qi)), 0)
    return pl.pallas_call(
        functools.partial(_fox_body, kv_len, q_off, tq, tk, n_kv),
        grid=(bsz, n_q, n_kv),
        in_specs=[pl.BlockSpec((1, tq, D_HEADS), lambda b, qi, ki: (b, qi, 0)),
                  pl.BlockSpec((1, tk, D_HEADS), kv_map),
                  pl.BlockSpec((1, tk, D_HEADS), kv_map),
                  pl.BlockSpec((1, tq, N_HEADS), lambda b, qi, ki: (b, q_off // tq + qi, 0)),
                  pl.BlockSpec((1, N_HEADS, tk), lambda b, qi, ki: (b, 0, jnp.minimum(ki, last_kv(qi)))),
                  _full((1, D_HEADS))],
        out_specs=pl.BlockSpec((1, tq, D_HEADS), lambda b, qi, ki: (b, qi, 0)),
        out_shape=jax.ShapeDtypeStruct((bsz, q_len, D_HEADS), F32),
        scratch_shapes=[pltpu.VMEM((N_HEADS, tq, 1), F32), pltpu.VMEM((N_HEADS, tq, 1), F32),
                        pltpu.VMEM((tq, D_HEADS), F32)],
        compiler_params=_params(("parallel", "parallel", "arbitrary")),
        name="fox",
    )(q, k_all, v_all, c_col, c_row, norm_g)


def _out_proj_body(yr_ref, yf_ref, h_ref, wr_ref, wf_ref, g_ref, b_ref, o_ref):
    mix = _dot(yr_ref[...].astype(BF16), wr_ref[...]) + _dot(yf_ref[...].astype(BF16), wf_ref[...])
    o_ref[...] = _layernorm(ALPHA * h_ref[...] + mix, g_ref[...], b_ref[...])


def _out_proj(yr, yf, h, wr, wf, g, b):
    n = h.shape[0]
    tb = TOKEN_BLOCK
    rows = lambda w: pl.BlockSpec((tb, w), lambda i: (i, 0))
    return pl.pallas_call(
        _out_proj_body,
        grid=(pl.cdiv(n, tb),),
        in_specs=[rows(D_HEADS), rows(D_HEADS), rows(D_MODEL), _full(wr.shape), _full(wf.shape),
                  _full((1, D_MODEL)), _full((1, D_MODEL))],
        out_specs=rows(D_MODEL),
        out_shape=jax.ShapeDtypeStruct((n, D_MODEL), F32),
        compiler_params=_params(("parallel",)),
        name="out_proj",
    )(yr, yf, h, wr, wf, g, b)


_BIG = 2 ** 30
_CAND_COUNTS = tuple(PEER_TOPK // (i + 1) for i in range(PEER_TOPK))
_N_CAND = sum(_CAND_COUNTS)
_CAND_ROWS = -(-_N_CAND // 8) * 8


def _top_rows(s, order, k, payload=None):
    vals, outs = [], []
    for _ in range(k):
        m = jnp.max(s, axis=0, keepdims=True)
        first = jnp.min(jnp.where(s == m, order, _BIG), axis=0, keepdims=True)
        sel = order == first
        vals.append(m)
        outs.append(first if payload is None else jnp.max(jnp.where(sel, payload, -1), axis=0, keepdims=True))
        s = jnp.where(sel, -jnp.inf, s)
    return jnp.concatenate(vals, axis=0), jnp.concatenate(outs, axis=0)


def _topk_body(h_ref, wq_ref, k1_ref, k2_ref, pos_ref, idx_ref, gate_ref, cand_s_sc, cand_i_sc):
    tb = h_ref.shape[0]
    q = _dot(h_ref[...].astype(BF16), wq_ref[...]).astype(BF16)
    key_id = lax.broadcasted_iota(I32, (N_KEYS, LANES), 0)
    cand_pos = pos_ref[...]
    k1 = k1_ref[...].astype(BF16)
    k2 = k2_ref[...].astype(BF16)
    cand_s_sc[_N_CAND:, :] = jnp.full((_CAND_ROWS - _N_CAND, LANES), -jnp.inf, F32)
    cand_i_sc[_N_CAND:, :] = jnp.zeros((_CAND_ROWS - _N_CAND, LANES), I32)
    for part in range(tb // LANES):
        qp = q[part * LANES:(part + 1) * LANES]
        ids, gates = [], []
        for h in range(PEER_HEADS):
            o = h * 2 * D_KEY_HALF
            s1 = _dot_nt(k1, qp[:, o:o + D_KEY_HALF])
            s2 = _dot_nt(k2, qp[:, o + D_KEY_HALF:o + 2 * D_KEY_HALF])
            v1, i1 = _top_rows(s1, key_id, PEER_TOPK)
            v2, i2 = _top_rows(s2, key_id, PEER_TOPK)
            off = 0
            for i, cnt in enumerate(_CAND_COUNTS):
                cand_s_sc[off:off + cnt, :] = v1[i:i + 1] + v2[:cnt]
                cand_i_sc[off:off + cnt, :] = i1[i:i + 1] * N_KEYS + i2[:cnt]
                off += cnt
            best_s, best_i = _top_rows(cand_s_sc[...], cand_pos, PEER_TOPK, payload=cand_i_sc[...])
            e = jnp.exp(best_s - best_s[0:1])
            ids.append(best_i)
            gates.append(e / jnp.sum(e, axis=0, keepdims=True))
        rows = slice(part * LANES, (part + 1) * LANES)
        idx_ref[rows, :] = jnp.concatenate(ids, axis=0).T
        gate_ref[rows, :] = jnp.concatenate(gates, axis=0).T


def _topk(h1, wq, keys1, keys2):
    n = h1.shape[0]
    tb = TOKEN_BLOCK
    pos = [i * PEER_TOPK + j for i, cnt in enumerate(_CAND_COUNTS) for j in range(cnt)]
    pos = jnp.asarray(pos + [_BIG] * (_CAND_ROWS - _N_CAND), I32)
    pos = jnp.broadcast_to(pos[:, None], (_CAND_ROWS, LANES))
    return pl.pallas_call(
        _topk_body,
        grid=(pl.cdiv(n, tb),),
        in_specs=[pl.BlockSpec((tb, D_MODEL), lambda i: (i, 0)), _full(wq.shape), _full(keys1.shape),
                  _full(keys2.shape), _full(pos.shape)],
        out_specs=[pl.BlockSpec((tb, PEER_SLOTS), lambda i: (i, 0)), pl.BlockSpec((tb, PEER_SLOTS), lambda i: (i, 0))],
        out_shape=[jax.ShapeDtypeStruct((n, PEER_SLOTS), I32), jax.ShapeDtypeStruct((n, PEER_SLOTS), F32)],
        scratch_shapes=[pltpu.VMEM((_CAND_ROWS, LANES), F32), pltpu.VMEM((_CAND_ROWS, LANES), I32)],
        compiler_params=_params(("parallel",)),
        name="peer_topk",
    )(h1, wq, keys1, keys2, pos)


def _sc_token_driver(n_tok, out_width, table, idx, per_token, compute):
    n_workers = SC_CORES * SC_SUBCORES
    assert n_tok % SC_UNIT == 0
    n_units = n_tok // SC_UNIT
    base, extra = divmod(n_units, n_workers)
    words = table.shape[1]
    mesh = plsc.VectorSubcoreMesh(core_axis_name="c", subcore_axis_name="s",
                                  num_cores=SC_CORES, num_subcores=SC_SUBCORES)

    @functools.partial(
        pl.kernel, mesh=mesh, compiler_params=pltpu.CompilerParams(needs_layout_passes=False),
        out_type=jax.ShapeDtypeStruct((n_tok, out_width), F32),
        scratch_types=[pltpu.VMEM((2 * SC_UNIT, SC_HALF), I32),
                       pltpu.VMEM((SC_HALF, words), I32), pltpu.VMEM((SC_HALF, words), I32),
                       pltpu.VMEM((SC_UNIT, per_token.shape[1]), F32),
                       pltpu.VMEM((SC_UNIT, out_width), F32),
                       pltpu.SemaphoreType.DMA, pltpu.SemaphoreType.DMA],
    )
    def sc_kernel(table_hbm, idx_hbm, per_hbm, out_hbm, idx_v, buf_a, buf_b, per_v, out_v, sem_a, sem_b):
        worker = lax.axis_index("s") * SC_CORES + lax.axis_index("c")
        first_unit = worker * base + jnp.minimum(worker, extra)
        my_units = base + (worker < extra).astype(I32)

        def gather(h, buf, sem):
            return pltpu.make_async_copy(table_hbm.at[idx_v.at[h]], buf, sem)

        def unit_body(ui, carry):
            tok0 = pl.multiple_of((first_unit + ui) * SC_UNIT, SC_UNIT)
            pltpu.sync_copy(idx_hbm.at[pl.ds(pl.multiple_of(tok0 * 2, 2 * SC_UNIT), 2 * SC_UNIT)], idx_v)
            pltpu.sync_copy(per_hbm.at[pl.ds(tok0, SC_UNIT)], per_v)
            gather(0, buf_a, sem_a).start()

            @pl.loop(0, SC_UNIT)
            def _(t):
                gather(2 * t + 1, buf_b, sem_b).start()
                gather(2 * t, buf_a, sem_a).wait()
                compute(buf_a, per_v, out_v, t, 0)

                @pl.when(t + 1 < SC_UNIT)
                def _():
                    gather(2 * t + 2, buf_a, sem_a).start()

                gather(2 * t + 1, buf_b, sem_b).wait()
                compute(buf_b, per_v, out_v, t, 1)

            pltpu.sync_copy(out_v, out_hbm.at[pl.ds(tok0, SC_UNIT)])
            return carry

        lax.fori_loop(0, my_units, unit_body, 0)

    return sc_kernel(table, idx.reshape(n_tok * 2, SC_HALF), per_token)


def _unpack_pair(w):
    return lax.bitcast_convert_type(w << 16, F32), lax.bitcast_convert_type(w & jnp.int32(-65536), F32)


def _expert_dot(table, idx, x):
    n_tok, words = x.shape[0], table.shape[1]
    lanes = SC_LANES

    def compute(buf, x_v, act_v, t, half):
        lane = lax.iota(I32, lanes)

        @pl.loop(0, SC_HALF // lanes)
        def _(g):
            def chunk(j, acc):
                x_lo = x_v[t, pl.ds(j * lanes, lanes)]
                x_hi = x_v[t, pl.ds(words + j * lanes, lanes)]
                out = []
                for r in range(lanes):
                    lo, hi = _unpack_pair(buf[g * lanes + r, pl.ds(j * lanes, lanes)])
                    out.append(acc[r] + lo * x_lo + hi * x_hi)
                return tuple(out)

            acc = lax.fori_loop(0, words // lanes, chunk, tuple(jnp.zeros((lanes,), F32) for _ in range(lanes)))
            res = jnp.zeros((lanes,), F32)
            for r in range(lanes):
                res = jnp.where(lane == r, jnp.sum(acc[r]), res)
            act_v[t, pl.ds(half * SC_HALF + g * lanes, lanes)] = res

    return _sc_token_driver(n_tok, PEER_SLOTS, table, idx, x, compute)


def _expert_sum(table, idx, coef):
    n_tok, words = coef.shape[0], table.shape[1]
    lanes = SC_LANES
    pass_chunks = words // lanes // 2

    def compute(buf, coef_v, out_v, t, half):
        for p in range(2):
            cols = [(p * pass_chunks + jj) * lanes for jj in range(pass_chunks)]
            if half == 0:
                init = tuple(jnp.zeros((lanes,), F32) for _ in range(2 * pass_chunks))
            else:
                init = (tuple(out_v[t, pl.ds(c, lanes)] for c in cols)
                        + tuple(out_v[t, pl.ds(words + c, lanes)] for c in cols))

            def row(r, acc):
                slot = jnp.full((lanes,), half * SC_HALF, I32) + r
                c = plsc.load_gather(coef_v, [jnp.full((lanes,), 0, I32) + t, slot])
                lo_acc, hi_acc = list(acc[:pass_chunks]), list(acc[pass_chunks:])
                for jj in range(pass_chunks):
                    lo, hi = _unpack_pair(buf[r, pl.ds(cols[jj], lanes)])
                    lo_acc[jj] = lo_acc[jj] + lo * c
                    hi_acc[jj] = hi_acc[jj] + hi * c
                return tuple(lo_acc) + tuple(hi_acc)

            acc = lax.fori_loop(0, SC_HALF, row, init)
            for jj in range(pass_chunks):
                out_v[t, pl.ds(cols[jj], lanes)] = acc[jj]
                out_v[t, pl.ds(words + cols[jj], lanes)] = acc[pass_chunks + jj]

    return _sc_token_driver(n_tok, 2 * words, table, idx, coef, compute)


def _coef_body(act_ref, gate_ref, o_ref):
    act = act_ref[...]
    o_ref[...] = gate_ref[...] * (act * 0.5 * (1.0 + lax.erf(act * (2.0 ** -0.5))))


def _final_body(x_ref, mix_ref, g_ref, b_ref, o_ref):
    o_ref[...] = _layernorm(ALPHA * x_ref[...] + mix_ref[...], g_ref[...], b_ref[...])


def _rowwise(body, name, n, widths_in, width_out, *args):
    tb = TOKEN_BLOCK
    rows = lambda w: pl.BlockSpec((tb, w), lambda i: (i, 0))
    n_rows_args = len(widths_in)
    return pl.pallas_call(
        body,
        grid=(pl.cdiv(n, tb),),
        in_specs=[rows(w) for w in widths_in] + [_full(a.shape) for a in args[n_rows_args:]],
        out_specs=rows(width_out),
        out_shape=jax.ShapeDtypeStruct((n, width_out), F32),
        compiler_params=_params(("parallel",)),
        name=name,
    )(*args)


def _group(x, shift0, wkv0, past, p):
    bsz, seq_len, _ = x.shape
    n = bsz * seq_len
    h, pr, q, k, v, lf = _in_proj(x.reshape(n, D_MODEL), p["ln_in_g"], p["ln_in_b"], p["w_r"], p["w_qkv"],
                                  p["w_f"], p["b_f"])
    pr = pr.reshape(bsz, seq_len, RWKV_IN)
    y_r, wkv_new = _rwkv(pr, shift0, wkv0, p)

    seq3 = lambda a: a.reshape(bsz, seq_len, -1)
    k3, v3, lf3 = seq3(k), seq3(v), seq3(lf)
    if past is None:
        k_all, v_all, lf_all, q_off, tq = k3, v3, lf3, 0, FOX_BLOCK
    else:
        k_past, v_past, lf_past = past
        q_off = k_past.shape[1]
        k_all = jnp.concatenate([k_past.reshape(bsz, q_off, D_HEADS), k3], axis=1)
        v_all = jnp.concatenate([v_past.reshape(bsz, q_off, D_HEADS), v3], axis=1)
        lf_all = jnp.concatenate([lf_past, lf3], axis=1)
        tq = seq_len
    c_col, c_row = _cumsum(lf_all)
    y_f = _fox(seq3(q), k_all, v_all, c_col, c_row, p["fox_norm_g"], q_off, tq, FOX_KV_BLOCK)

    h1 = _out_proj(y_r.reshape(n, D_HEADS), y_f.reshape(n, D_HEADS), h, p["w_out_r"], p["w_out_f"],
                   p["ln1_g"], p["ln1_b"])
    idx, gate = _topk(h1, p["peer_w_q"], p["peer_keys1"], p["peer_keys2"])
    act = _expert_dot(p["peer_u"], idx, h1)
    coef = _rowwise(_coef_body, "peer_coef", n, (PEER_SLOTS, PEER_SLOTS), PEER_SLOTS, act, gate)
    mix = _expert_sum(p["peer_v"], idx, coef)
    y = _rowwise(_final_body, "final_ln", n, (D_MODEL, D_MODEL), D_MODEL, h1, mix, p["ln2_g"], p["ln2_b"])

    heads = lambda a: a.reshape(1, bsz, seq_len, N_HEADS, HEAD)
    return (y.reshape(bsz, seq_len, D_MODEL), pr[None, :, seq_len - 1:, :], wkv_new[None],
            heads(k), heads(v), lf3[None])


def _pack_halves(a):
    bits = lax.bitcast_convert_type(a.astype(BF16), jnp.uint16).astype(jnp.uint32)
    w = a.shape[1] // 2
    return lax.bitcast_convert_type(bits[:, :w] | (bits[:, w:] << 16), I32)


def kernel(x_prompt, x_sample, state_rwkv_shift, state_rwkv_wkv, cache_fox_k, cache_fox_v, cache_fox_logf,
           meta_tokens, ln_in_g, ln_in_b, w_in, rwkv_mix, rwkv_w0, rwkv_w2, rwkv_a0, rwkv_a2, rwkv_g2,
           rwkv_k_k, rwkv_k_a, rwkv_r_k, rwkv_lnx_g, rwkv_lnx_b, fox_b_f, fox_norm_g, w_out, ln1_g, ln1_b,
           peer_w_q, peer_keys1, peer_keys2, peer_u, peer_v, ln2_g, ln2_b):
    assert w_in.shape[0] == 1, "single-layer step"
    bsz = x_prompt.shape[0]
    row = lambda a: a.reshape(1, -1)
    w = w_in[0]
    fox0 = RWKV_IN
    pad_f = LANES - N_HEADS
    p = dict(
        ln_in_g=row(ln_in_g), ln_in_b=row(ln_in_b),
        w_r=w[:, :RWKV_IN].astype(BF16),
        w_qkv=w[:, fox0:fox0 + 3 * D_HEADS].astype(BF16),
        w_f=jnp.pad(w[:, fox0 + 3 * D_HEADS:], ((0, 0), (0, pad_f))).astype(BF16),
        b_f=jnp.pad(row(fox_b_f[0]), ((0, 0), (0, pad_f))),
        rwkv_mix=rwkv_mix[0], rwkv_w0=rwkv_w0[0], rwkv_w2=rwkv_w2[0], rwkv_a0=rwkv_a0[0], rwkv_a2=rwkv_a2[0],
        rwkv_g2=rwkv_g2[0], rwkv_k_k=rwkv_k_k[0], rwkv_k_a=rwkv_k_a[0], rwkv_r_k=rwkv_r_k[0],
        rwkv_lnx_g=rwkv_lnx_g[0], rwkv_lnx_b=rwkv_lnx_b[0],
        fox_norm_g=row(fox_norm_g[0]),
        w_out_r=w_out[0, :D_HEADS].astype(BF16), w_out_f=w_out[0, D_HEADS:].astype(BF16),
        ln1_g=row(ln1_g[0]), ln1_b=row(ln1_b[0]),
        peer_w_q=peer_w_q[0].astype(BF16), peer_keys1=peer_keys1[0], peer_keys2=peer_keys2[0],
        peer_u=_pack_halves(peer_u[0]), peer_v=_pack_halves(peer_v[0]),
        ln2_g=row(ln2_g[0]), ln2_b=row(ln2_b[0]),
    )
    n_parts = PROMPT_PARTS if bsz % PROMPT_PARTS == 0 else 1
    pb = bsz // n_parts
    meta = jnp.broadcast_to(meta_tokens[None], (pb, N_META, D_MODEL))
    zero_shift = jnp.zeros((pb, 1, RWKV_IN), F32)
    zero_wkv = jnp.zeros((pb, N_HEADS, HEAD, HEAD), F32)
    parts = []
    for i in range(n_parts):
        hp0 = jnp.concatenate([meta, x_prompt[i * pb:(i + 1) * pb]], axis=1)
        yp, *state = _group(hp0, zero_shift, zero_wkv, None, p)
        parts.append((yp[:, N_META:], *state))
    ys, s_shift, s_wkv, s_k, s_v, s_lf = _group(
        x_sample, state_rwkv_shift[0], state_rwkv_wkv[0],
        (cache_fox_k[0], cache_fox_v[0], cache_fox_logf[0]), p)
    yp = jnp.concatenate([t[0] for t in parts], axis=0)
    p_shift, p_wkv, p_k, p_v, p_lf = (jnp.concatenate([t[j] for t in parts], axis=1) for j in range(1, 6))
    return (yp, ys, p_shift, p_wkv, p_k, p_v, p_lf, s_shift, s_wkv, s_k, s_v, s_lf)
```

```python
import functools

import jax
import jax.numpy as jnp
from jax import lax
from jax.experimental import pallas as pl
from jax.experimental.pallas import tpu as pltpu
from jax.experimental.pallas import tpu_sc as plsc

F32 = jnp.float32
BF16 = jnp.bfloat16
I32 = jnp.int32

D_MODEL = 1024
N_HEADS = 8
HEAD = 64
D_HEADS = N_HEADS * HEAD
LORA_W, LORA_A, LORA_G = 64, 64, 128
RWKV_IN = 3 * D_HEADS + LORA_W + LORA_A + LORA_G
N_META = 16
LNX_EPS = 64e-5
LN_EPS = 1e-5
ALPHA = 2.0 ** 0.25
N_KEYS = 128
PEER_HEADS = 8
PEER_TOPK = 16
PEER_SLOTS = PEER_HEADS * PEER_TOPK
D_KEY_HALF = 128

LANES = 128
CHUNK = 64
TOKEN_BLOCK = 256
FOX_BLOCK = 256
FOX_KV_BLOCK = 512
SC_CORES, SC_SUBCORES, SC_LANES = 2, 16, 16
SC_HALF = PEER_SLOTS // 2
SC_UNIT = 8
PROMPT_SLICES = (1, 1, 2, 2, 2, 2, 2, 2, 2)
VMEM_LIMIT = 48 * 1024 * 1024

HIGHEST = lax.Precision.HIGHEST


def _params(semantics):
    return pltpu.CompilerParams(dimension_semantics=semantics, vmem_limit_bytes=VMEM_LIMIT)


def _dot(a, b, precision=None):
    return jnp.dot(a, b, preferred_element_type=F32, precision=precision)


def _dot_nt(a, b, precision=None):
    return lax.dot_general(a, b, (((1,), (1,)), ((), ())), preferred_element_type=F32, precision=precision)


def _dot_tn(a, b, precision=None):
    return lax.dot_general(a, b, (((0,), (0,)), ((), ())), preferred_element_type=F32, precision=precision)


def _split3(a):
    a1 = a.astype(BF16)
    r1 = a - a1.astype(F32)
    a2 = r1.astype(BF16)
    a3 = (r1 - a2.astype(F32)).astype(BF16)
    return a1, a2, a3


def _split2(a):
    hi = a.astype(BF16)
    return hi, (a - hi.astype(F32)).astype(BF16)


def _mm3(dot, a, b):
    return dot(a[0], b[0]) + (dot(a[0], b[1]) + dot(a[1], b[0]))


def _dot_exact_rhs(a, m_bf16):
    a1, a2, a3 = _split3(a)
    return _dot(a1, m_bf16) + _dot(a2, m_bf16) + _dot(a3, m_bf16)


def _dot_exact_lhs(m_bf16, a):
    a1, a2, a3 = _split3(a)
    return _dot(m_bf16, a1) + _dot(m_bf16, a2) + _dot(m_bf16, a3)


def _layernorm(x, g, b):
    xc = x - jnp.mean(x, axis=-1, keepdims=True)
    var = jnp.mean(xc * xc, axis=-1, keepdims=True)
    return xc * lax.rsqrt(var + LN_EPS) * g + b


def _softplus(x):
    return jnp.maximum(x, 0.0) + jnp.log1p(jnp.exp(-jnp.abs(x)))


def _sigmoid(x):
    return 1.0 / (1.0 + jnp.exp(-x))


def _full(shape):
    return pl.BlockSpec(shape, lambda *_: (0,) * len(shape))


def _in_proj_body(x_ref, g_ref, b_ref, wr_ref, wqkv_ref, wf_ref, bf_ref,
                  h_ref, pr_ref, q_ref, k_ref, v_ref, lf_ref):
    h = _layernorm(x_ref[...], g_ref[...], b_ref[...])
    h_ref[...] = h
    hb = h.astype(BF16)
    pr_ref[...] = _dot(hb, wr_ref[...])
    qkv = _dot(hb, wqkv_ref[...])
    q_ref[...] = qkv[:, :D_HEADS]
    k_ref[...] = qkv[:, D_HEADS:2 * D_HEADS]
    v_ref[...] = qkv[:, 2 * D_HEADS:]
    fl = _dot(hb, wf_ref[...]) + bf_ref[...]
    lf = jnp.minimum(fl, 0.0) - jnp.log1p(jnp.exp(-jnp.abs(fl)))
    lf_ref[...] = lf[:, :N_HEADS]


def _in_proj(x, g, b, wr, wqkv, wf, bfp):
    n = x.shape[0]
    tb = TOKEN_BLOCK
    rows = lambda w: pl.BlockSpec((tb, w), lambda i: (i, 0))
    return pl.pallas_call(
        _in_proj_body,
        grid=(pl.cdiv(n, tb),),
        in_specs=[rows(D_MODEL), _full((1, D_MODEL)), _full((1, D_MODEL)), _full(wr.shape),
                  _full(wqkv.shape), _full(wf.shape), _full(bfp.shape)],
        out_specs=[rows(D_MODEL), rows(RWKV_IN), rows(D_HEADS), rows(D_HEADS), rows(D_HEADS), rows(N_HEADS)],
        out_shape=[jax.ShapeDtypeStruct((n, w), F32)
                   for w in (D_MODEL, RWKV_IN, D_HEADS, D_HEADS, D_HEADS, N_HEADS)],
        compiler_params=_params(("parallel",)),
        name="in_proj",
    )(x, g, b, wr, wqkv, wf, bfp)


def _rwkv_body(seq_len, n_chunks, pr_ref, shift0_ref, s0_ref, mix_ref, w0_ref, w2_ref, a0_ref, a2_ref,
               g2_ref, kk_ref, ka_ref, rk_ref, lng_ref, lnb_ref, seg_ref, tri_ref,
               y_ref, sout_ref, shift_sc, state_sc):
    c = pl.program_id(1)

    @pl.when(c == 0)
    def _():
        shift_sc[...] = shift0_ref[0]
        state_sc[...] = s0_ref[0]

    row = lax.broadcasted_iota(I32, (CHUNK, 1), 0)
    valid = (c * CHUNK + row) < seq_len
    prf = jnp.where(valid, pr_ref[0], 0.0)
    prev = jnp.where(row == 0, shift_sc[...], pltpu.roll(prf, 1, 0))
    shift_sc[...] = prf[CHUNK - 1:CHUNK, :]
    xm = prf + (prev - prf) * mix_ref[...]

    r = xm[:, :D_HEADS]
    k = xm[:, D_HEADS:2 * D_HEADS]
    v = xm[:, 2 * D_HEADS:3 * D_HEADS]
    o = 3 * D_HEADS
    xw = xm[:, o:o + LORA_W]
    xa = xm[:, o + LORA_W:o + LORA_W + LORA_A]
    xg = xm[:, o + LORA_W + LORA_A:]

    seg = seg_ref[...]
    wlog = -_softplus(-(w0_ref[...] + _dot(jnp.tanh(xw), w2_ref[...], HIGHEST))) - 0.5
    lw = jnp.where(valid, -jnp.exp(wlog), 0.0)
    a = _sigmoid(a0_ref[...] + _dot(xa, a2_ref[...], HIGHEST))
    g = _dot(_sigmoid(xg), g2_ref[...], HIGHEST)

    kk = k * kk_ref[...]
    kk_norm = jnp.sqrt(_dot_exact_rhs(kk * kk, seg))
    kk = jnp.where(valid, kk / jnp.maximum(kk_norm, 1e-12), 0.0)
    k = jnp.where(valid, k * (1.0 + (a - 1.0) * ka_ref[...]), 0.0)
    v = jnp.where(valid, v, 0.0)

    cl = _dot_exact_lhs(tri_ref[...], lw)
    dec_in = jnp.exp(cl)
    dec_out = jnp.exp(-cl)
    a_t = -kk * jnp.exp(cl - lw)
    b_t = kk * a * dec_out
    k_t = k * dec_out
    r_t = r * dec_in
    gamma = jnp.exp(cl[CHUNK - 1:CHUNK, :])

    ti = lax.broadcasted_iota(I32, (CHUNK, 2 * CHUNK), 0)
    si = lax.broadcasted_iota(I32, (CHUNK, 2 * CHUNK), 1)
    si = jnp.where(si >= CHUNK, si - CHUNK, si)
    strict = si < ti
    incl = si <= ti
    eye = (lax.broadcasted_iota(I32, (CHUNK, CHUNK), 0) == lax.broadcasted_iota(I32, (CHUNK, CHUNK), 1)).astype(F32)
    zeros = jnp.zeros((CHUNK, HEAD), F32)

    heads = range(N_HEADS)
    sls = [slice(h * HEAD, (h + 1) * HEAD) for h in heads]
    vh = [v[:, sl] for sl in sls]
    ar = [_split2(jnp.concatenate([a_t[:, sl], r_t[:, sl]], axis=0)) for sl in sls]
    bk = [_split2(jnp.concatenate([b_t[:, sl], k_t[:, sl]], axis=0)) for sl in sls]
    s0 = [state_sc[h] for h in heads]
    gram = [_mm3(_dot_nt, ar[h], bk[h]) for h in heads]
    top = [jnp.where(strict, gram[h][:CHUNK], 0.0) for h in heads]
    bot = [jnp.where(incl, gram[h][CHUNK:], 0.0) for h in heads]
    hs = [_mm3(_dot_nt, ar[h], _split2(s0[h])) for h in heads]
    rhs = [hs[h][:CHUNK] + _mm3(_dot, _split2(top[h]), _split2(jnp.concatenate([zeros, vh[h]], axis=0)))
           for h in heads]
    inv = [eye + top[h][:, :CHUNK] for h in heads]
    pws = [_split2(top[h][:, :CHUNK]) for h in heads]
    for _ in range(5):
        pws = [_split2(_mm3(_dot, pws[h], pws[h])) for h in heads]
        inv = [inv[h] + _mm3(_dot, _split2(inv[h]), pws[h]) for h in heads]
    u = [_mm3(_dot, _split2(inv[h]), _split2(rhs[h])) for h in heads]
    uv = [_split2(jnp.concatenate([u[h], vh[h]], axis=0)) for h in heads]
    ys = [hs[h][CHUNK:] + _dot(bot[h].astype(BF16), uv[h][0]) for h in heads]
    for h in heads:
        state_sc[h] = (s0[h] + _mm3(_dot_tn, uv[h], bk[h])) * gamma[:, sls[h]]

    y = jnp.concatenate(ys, axis=1)
    yc = y - _dot_exact_rhs(y, seg) * (1.0 / HEAD)
    var = _dot_exact_rhs(yc * yc, seg) * (1.0 / HEAD)
    y = yc * lax.rsqrt(var + LNX_EPS) * lng_ref[...] + lnb_ref[...]
    r = xm[:, :D_HEADS]
    bonus = _dot_exact_rhs(r * k * rk_ref[...], seg) * v
    y_ref[0] = (y + bonus) * g

    @pl.when(c == n_chunks - 1)
    def _():
        sout_ref[0] = state_sc[...]


def _rwkv(pr, shift0, wkv0, p):
    bsz, seq_len, _ = pr.shape
    n_chunks = pl.cdiv(seq_len, CHUNK)
    lane = jnp.arange(D_HEADS) // HEAD
    seg = (lane[:, None] == lane[None, :]).astype(BF16)
    tri = (jnp.arange(CHUNK)[None, :] <= jnp.arange(CHUNK)[:, None]).astype(BF16)
    vec = lambda a: a.reshape(1, -1)
    consts = [vec(p["rwkv_mix"]), vec(p["rwkv_w0"]), p["rwkv_w2"], vec(p["rwkv_a0"]), p["rwkv_a2"], p["rwkv_g2"],
              vec(p["rwkv_k_k"]), vec(p["rwkv_k_a"]), vec(p["rwkv_r_k"]), vec(p["rwkv_lnx_g"]),
              vec(p["rwkv_lnx_b"]), seg, tri]
    return pl.pallas_call(
        functools.partial(_rwkv_body, seq_len, n_chunks),
        grid=(bsz, n_chunks),
        in_specs=[pl.BlockSpec((1, CHUNK, RWKV_IN), lambda b, c: (b, c, 0)),
                  pl.BlockSpec((1, 1, RWKV_IN), lambda b, c: (b, 0, 0)),
                  pl.BlockSpec((1, N_HEADS, HEAD, HEAD), lambda b, c: (b, 0, 0, 0))]
                 + [_full(a.shape) for a in consts],
        out_specs=[pl.BlockSpec((1, CHUNK, D_HEADS), lambda b, c: (b, c, 0)),
                   pl.BlockSpec((1, N_HEADS, HEAD, HEAD), lambda b, c: (b, 0, 0, 0))],
        out_shape=[jax.ShapeDtypeStruct((bsz, seq_len, D_HEADS), F32),
                   jax.ShapeDtypeStruct((bsz, N_HEADS, HEAD, HEAD), F32)],
        scratch_shapes=[pltpu.VMEM((1, RWKV_IN), F32), pltpu.VMEM((N_HEADS, HEAD, HEAD), F32)],
        compiler_params=_params(("parallel", "arbitrary")),
        name="rwkv",
    )(pr, shift0, wkv0, *consts)


def _cumsum_body(seq_len, lf_ref, col_ref, row_ref):
    carry = jnp.zeros((1, N_HEADS), F32)
    eye = (lax.broadcasted_iota(I32, (N_HEADS, N_HEADS), 0)
           == lax.broadcasted_iota(I32, (N_HEADS, N_HEADS), 1)).astype(BF16)
    for start in range(0, seq_len, LANES):
        n = min(LANES, seq_len - start)
        tri = (lax.broadcasted_iota(I32, (n, n), 1) <= lax.broadcasted_iota(I32, (n, n), 0)).astype(BF16)
        cum = _dot_exact_lhs(tri, lf_ref[0, start:start + n, :]) + carry
        col_ref[0, start:start + n, :] = cum
        c1, c2, c3 = _split3(cum)
        row_ref[0, :, start:start + n] = _dot_nt(eye, c1) + _dot_nt(eye, c2) + _dot_nt(eye, c3)
        carry = cum[n - 1:n, :]


def _cumsum(lf):
    bsz, seq_len, _ = lf.shape
    return pl.pallas_call(
        functools.partial(_cumsum_body, seq_len),
        grid=(bsz,),
        in_specs=[pl.BlockSpec((1, seq_len, N_HEADS), lambda b: (b, 0, 0))],
        out_specs=[pl.BlockSpec((1, seq_len, N_HEADS), lambda b: (b, 0, 0)),
                   pl.BlockSpec((1, N_HEADS, seq_len), lambda b: (b, 0, 0))],
        out_shape=[jax.ShapeDtypeStruct((bsz, seq_len, N_HEADS), F32),
                   jax.ShapeDtypeStruct((bsz, N_HEADS, seq_len), F32)],
        compiler_params=_params(("parallel",)),
        name="cumsum",
    )(lf)


def _fox_body(kv_len, q_off, tq, tk, n_kv, q_ref, k_ref, v_ref, cq_ref, ck_ref, g_ref, o_ref, m_sc, l_sc, acc_sc):
    qi = pl.program_id(1)
    ki = pl.program_id(2)

    @pl.when(ki == 0)
    def _():
        m_sc[...] = jnp.full(m_sc.shape, -jnp.inf, F32)
        l_sc[...] = jnp.zeros(l_sc.shape, F32)
        acc_sc[...] = jnp.zeros(acc_sc.shape, F32)

    q_start = q_off + qi * tq
    k_start = ki * tk

    @pl.when(k_start <= q_start + tq - 1)
    def _():
        rows = q_start + lax.broadcasted_iota(I32, (tq, 1), 0)
        cols = k_start + lax.broadcasted_iota(I32, (1, tk), 1)
        mask = cols <= rows
        k_valid = (k_start + lax.broadcasted_iota(I32, (tk, 1), 0)) < kv_len
        q = q_ref[0] * (HEAD ** -0.5)
        kb = k_ref[0].astype(BF16)
        vb = jnp.where(k_valid, v_ref[0], 0.0).astype(BF16)
        cq = cq_ref[0]
        ck = ck_ref[0]
        ones = jnp.ones((tk, LANES), BF16)
        heads = range(N_HEADS)
        sls = [slice(h * HEAD, (h + 1) * HEAD) for h in heads]
        qb = q.astype(BF16)
        s = [_dot_nt(qb[:, sl], kb[:, sl]) for sl in sls]
        s = [jnp.where(mask, s[h] + (cq[:, h:h + 1] - ck[h:h + 1, :]), -jnp.inf) for h in heads]
        m_prev = [m_sc[h] for h in heads]
        m_new = [jnp.maximum(m_prev[h], jnp.max(s[h], axis=-1, keepdims=True)) for h in heads]
        scale = [jnp.exp(m_prev[h] - m_new[h]) for h in heads]
        pexp = [jnp.exp(s[h] - m_new[h]).astype(BF16) for h in heads]
        psum = [_dot(pexp[h], ones)[:, :1] for h in heads]
        pv = [_dot(pexp[h], vb[:, sls[h]]) for h in heads]
        for h in heads:
            l_sc[h] = scale[h] * l_sc[h] + psum[h]
            acc_sc[:, sls[h]] = scale[h] * acc_sc[:, sls[h]] + pv[h]
            m_sc[h] = m_new[h]

    @pl.when(ki == n_kv - 1)
    def _():
        for h in range(N_HEADS):
            sl = slice(h * HEAD, (h + 1) * HEAD)
            o = acc_sc[:, sl] / l_sc[h]
            o = o * lax.rsqrt(jnp.mean(o * o, axis=-1, keepdims=True) + 1e-6)
            acc_sc[:, sl] = o
        o_ref[0] = acc_sc[...] * g_ref[...]


def _fox(q, k_all, v_all, c_col, c_row, norm_g, q_off, tq, tk):
    bsz, q_len, _ = q.shape
    kv_len = k_all.shape[1]
    n_q = pl.cdiv(q_len, tq)
    n_kv = pl.cdiv(kv_len, tk)
    assert q_off % tq == 0
    last_kv = lambda qi: (q_off + (qi + 1) * tq - 1) // tk
    kv_map = lambda b, qi, ki: (b, jnp.minimum(ki, last_kv(qi)), 0)
    return pl.pallas_call(
        functools.partial(_fox_body, kv_len, q_off, tq, tk, n_kv),
        grid=(bsz, n_q, n_kv),
        in_specs=[pl.BlockSpec((1, tq, D_HEADS), lambda b, qi, ki: (b, qi, 0)),
                  pl.BlockSpec((1, tk, D_HEADS), kv_map),
                  pl.BlockSpec((1, tk, D_HEADS), kv_map),
                  pl.BlockSpec((1, tq, N_HEADS), lambda b, qi, ki: (b, q_off // tq + qi, 0)),
                  pl.BlockSpec((1, N_HEADS, tk), lambda b, qi, ki: (b, 0, jnp.minimum(ki, last_kv(qi)))),
                  _full((1, D_HEADS))],
        out_specs=pl.BlockSpec((1, tq, D_HEADS), lambda b, qi, ki: (b, qi, 0)),
        out_shape=jax.ShapeDtypeStruct((bsz, q_len, D_HEADS), F32),
        scratch_shapes=[pltpu.VMEM((N_HEADS, tq, 1), F32), pltpu.VMEM((N_HEADS, tq, 1), F32),
                        pltpu.VMEM((tq, D_HEADS), F32)],
        compiler_params=_params(("parallel", "parallel", "arbitrary")),
        name="fox",
    )(q, k_all, v_all, c_col, c_row, norm_g)


def _out_proj_body(yr_ref, yf_ref, h_ref, wr_ref, wf_ref, g_ref, b_ref, o_ref):
    mix = _dot(yr_ref[...].astype(BF16), wr_ref[...]) + _dot(yf_ref[...].astype(BF16), wf_ref[...])
    o_ref[...] = _layernorm(ALPHA * h_ref[...] + mix, g_ref[...], b_ref[...])


def _out_proj(yr, yf, h, wr, wf, g, b):
    n = h.shape[0]
    tb = TOKEN_BLOCK
    rows = lambda w: pl.BlockSpec((tb, w), lambda i: (i, 0))
    return pl.pallas_call(
        _out_proj_body,
        grid=(pl.cdiv(n, tb),),
        in_specs=[rows(D_HEADS), rows(D_HEADS), rows(D_MODEL), _full(wr.shape), _full(wf.shape),
                  _full((1, D_MODEL)), _full((1, D_MODEL))],
        out_specs=rows(D_MODEL),
        out_shape=jax.ShapeDtypeStruct((n, D_MODEL), F32),
        compiler_params=_params(("parallel",)),
        name="out_proj",
    )(yr, yf, h, wr, wf, g, b)


_BIG = 2 ** 30
_CAND_COUNTS = tuple(PEER_TOPK // (i + 1) for i in range(PEER_TOPK))
_N_CAND = sum(_CAND_COUNTS)
_CAND_ROWS = -(-_N_CAND // 8) * 8


def _top_rows(s, order, k, payload=None):
    vals, outs = [], []
    for _ in range(k):
        m = jnp.max(s, axis=0, keepdims=True)
        first = jnp.min(jnp.where(s == m, order, _BIG), axis=0, keepdims=True)
        sel = order == first
        vals.append(m)
        outs.append(first if payload is None else jnp.max(jnp.where(sel, payload, -1), axis=0, keepdims=True))
        s = jnp.where(sel, -jnp.inf, s)
    return jnp.concatenate(vals, axis=0), jnp.concatenate(outs, axis=0)


def _topk_body(h_ref, wq_ref, k1_ref, k2_ref, pos_ref, idx_ref, gate_ref, cand_s_sc, cand_i_sc):
    tb = h_ref.shape[0]
    q = _dot(h_ref[...].astype(BF16), wq_ref[...]).astype(BF16)
    key_id = lax.broadcasted_iota(I32, (N_KEYS, LANES), 0)
    cand_pos = pos_ref[...]
    k1 = k1_ref[...].astype(BF16)
    k2 = k2_ref[...].astype(BF16)
    cand_s_sc[_N_CAND:, :] = jnp.full((_CAND_ROWS - _N_CAND, LANES), -jnp.inf, F32)
    cand_i_sc[_N_CAND:, :] = jnp.zeros((_CAND_ROWS - _N_CAND, LANES), I32)
    for part in range(tb // LANES):
        qp = q[part * LANES:(part + 1) * LANES]
        ids, gates = [], []
        for h in range(PEER_HEADS):
            o = h * 2 * D_KEY_HALF
            s1 = _dot_nt(k1, qp[:, o:o + D_KEY_HALF])
            s2 = _dot_nt(k2, qp[:, o + D_KEY_HALF:o + 2 * D_KEY_HALF])
            v1, i1 = _top_rows(s1, key_id, PEER_TOPK)
            v2, i2 = _top_rows(s2, key_id, PEER_TOPK)
            off = 0
            for i, cnt in enumerate(_CAND_COUNTS):
                cand_s_sc[off:off + cnt, :] = v1[i:i + 1] + v2[:cnt]
                cand_i_sc[off:off + cnt, :] = i1[i:i + 1] * N_KEYS + i2[:cnt]
                off += cnt
            best_s, best_i = _top_rows(cand_s_sc[...], cand_pos, PEER_TOPK, payload=cand_i_sc[...])
            e = jnp.exp(best_s - best_s[0:1])
            ids.append(best_i)
            gates.append(e / jnp.sum(e, axis=0, keepdims=True))
        rows = slice(part * LANES, (part + 1) * LANES)
        idx_ref[rows, :] = jnp.concatenate(ids, axis=0).T
        gate_ref[rows, :] = jnp.concatenate(gates, axis=0).T


def _topk(h1, wq, keys1, keys2):
    n = h1.shape[0]
    tb = TOKEN_BLOCK
    pos = [i * PEER_TOPK + j for i, cnt in enumerate(_CAND_COUNTS) for j in range(cnt)]
    pos = jnp.asarray(pos + [_BIG] * (_CAND_ROWS - _N_CAND), I32)
    pos = jnp.broadcast_to(pos[:, None], (_CAND_ROWS, LANES))
    return pl.pallas_call(
        _topk_body,
        grid=(pl.cdiv(n, tb),),
        in_specs=[pl.BlockSpec((tb, D_MODEL), lambda i: (i, 0)), _full(wq.shape), _full(keys1.shape),
                  _full(keys2.shape), _full(pos.shape)],
        out_specs=[pl.BlockSpec((tb, PEER_SLOTS), lambda i: (i, 0)), pl.BlockSpec((tb, PEER_SLOTS), lambda i: (i, 0))],
        out_shape=[jax.ShapeDtypeStruct((n, PEER_SLOTS), I32), jax.ShapeDtypeStruct((n, PEER_SLOTS), F32)],
        scratch_shapes=[pltpu.VMEM((_CAND_ROWS, LANES), F32), pltpu.VMEM((_CAND_ROWS, LANES), I32)],
        compiler_params=_params(("parallel",)),
        name="peer_topk",
    )(h1, wq, keys1, keys2, pos)


def _sc_token_driver(n_tok, out_width, table, idx, per_token, compute):
    n_workers = SC_CORES * SC_SUBCORES
    assert n_tok % SC_UNIT == 0
    n_units = n_tok // SC_UNIT
    base, extra = divmod(n_units, n_workers)
    words = table.shape[1]
    mesh = plsc.VectorSubcoreMesh(core_axis_name="c", subcore_axis_name="s",
                                  num_cores=SC_CORES, num_subcores=SC_SUBCORES)

    @functools.partial(
        pl.kernel, mesh=mesh, compiler_params=pltpu.CompilerParams(needs_layout_passes=False),
        out_type=jax.ShapeDtypeStruct((n_tok, out_width), F32),
        scratch_types=[pltpu.VMEM((2 * 2 * SC_UNIT, SC_HALF), I32),
                       pltpu.VMEM((SC_HALF, words), I32), pltpu.VMEM((SC_HALF, words), I32),
                       pltpu.VMEM((2 * SC_UNIT, per_token.shape[1]), F32),
                       pltpu.VMEM((2 * SC_UNIT, out_width), F32),
                       pltpu.SemaphoreType.DMA, pltpu.SemaphoreType.DMA, pltpu.SemaphoreType.DMA,
                       pltpu.SemaphoreType.DMA((2,))],
    )
    def sc_kernel(table_hbm, idx_hbm, per_hbm, out_hbm, idx_v, buf_a, buf_b, per_v, out_v,
                  sem_a, sem_b, sem_in, sem_out):
        worker = lax.axis_index("s") * SC_CORES + lax.axis_index("c")
        first_unit = worker * base + jnp.minimum(worker, extra)
        my_units = base + (worker < extra).astype(I32)
        idx_rows = 2 * SC_UNIT

        def first_token(ui):
            return pl.multiple_of((first_unit + ui) * SC_UNIT, SC_UNIT)

        def stage_in(ui, slot):
            tok0 = first_token(ui)
            rows = pl.ds(pl.multiple_of(tok0 * 2, idx_rows), idx_rows)
            return (pltpu.make_async_copy(idx_hbm.at[rows], idx_v.at[pl.ds(slot * idx_rows, idx_rows)], sem_in),
                    pltpu.make_async_copy(per_hbm.at[pl.ds(tok0, SC_UNIT)],
                                          per_v.at[pl.ds(slot * SC_UNIT, SC_UNIT)], sem_in))

        def write_out(ui, slot):
            return pltpu.make_async_copy(out_v.at[pl.ds(slot * SC_UNIT, SC_UNIT)],
                                         out_hbm.at[pl.ds(first_token(ui), SC_UNIT)], sem_out.at[slot])

        def gather(row, buf, sem):
            return pltpu.make_async_copy(table_hbm.at[idx_v.at[row]], buf, sem)

        def unit_body(ui, carry):
            slot = lax.rem(ui, 2)
            has_next = ui + 1 < my_units

            @pl.when(has_next)
            def _():
                for c in stage_in(ui + 1, 1 - slot):
                    c.start()

            @pl.when(ui >= 2)
            def _():
                write_out(ui - 2, slot).wait()

            @pl.loop(0, SC_UNIT)
            def _(t):
                row = slot * idx_rows + 2 * t
                gather(row + 1, buf_b, sem_b).start()
                gather(row, buf_a, sem_a).wait()
                compute(buf_a, per_v, out_v, slot * SC_UNIT + t, 0)

                @pl.when(t + 1 < SC_UNIT)
                def _():
                    gather(row + 2, buf_a, sem_a).start()

                @pl.when((t + 1 == SC_UNIT) & has_next)
                def _():
                    for c in stage_in(ui + 1, 1 - slot):
                        c.wait()
                    gather((1 - slot) * idx_rows, buf_a, sem_a).start()

                gather(row + 1, buf_b, sem_b).wait()
                compute(buf_b, per_v, out_v, slot * SC_UNIT + t, 1)

            write_out(ui, slot).start()
            return carry

        @pl.when(my_units > 0)
        def _():
            for c in stage_in(0, 0):
                c.start()
            for c in stage_in(0, 0):
                c.wait()
            gather(0, buf_a, sem_a).start()
            lax.fori_loop(0, my_units, unit_body, 0)

            @pl.when(my_units >= 2)
            def _():
                write_out(my_units - 2, lax.rem(my_units, 2)).wait()

            write_out(my_units - 1, lax.rem(my_units - 1, 2)).wait()

    return sc_kernel(table, idx.reshape(n_tok * 2, SC_HALF), per_token)


def _unpack_pair(w):
    return lax.bitcast_convert_type(w << 16, F32), lax.bitcast_convert_type(w & jnp.int32(-65536), F32)


def _expert_dot(table, idx, x):
    n_tok, words = x.shape[0], table.shape[1]
    lanes = SC_LANES

    def compute(buf, x_v, act_v, t, half):
        lane = lax.iota(I32, lanes)

        @pl.loop(0, SC_HALF // lanes)
        def _(g):
            def chunk(j, acc):
                x_lo = x_v[t, pl.ds(j * lanes, lanes)]
                x_hi = x_v[t, pl.ds(words + j * lanes, lanes)]
                out = []
                for r in range(lanes):
                    lo, hi = _unpack_pair(buf[g * lanes + r, pl.ds(j * lanes, lanes)])
                    out.append(acc[r] + lo * x_lo + hi * x_hi)
                return tuple(out)

            acc = lax.fori_loop(0, words // lanes, chunk, tuple(jnp.zeros((lanes,), F32) for _ in range(lanes)))
            res = jnp.zeros((lanes,), F32)
            for r in range(lanes):
                res = jnp.where(lane == r, jnp.sum(acc[r]), res)
            act_v[t, pl.ds(half * SC_HALF + g * lanes, lanes)] = res

    return _sc_token_driver(n_tok, PEER_SLOTS, table, idx, x, compute)


def _expert_sum(table, idx, coef):
    n_tok, words = coef.shape[0], table.shape[1]
    lanes = SC_LANES
    pass_chunks = words // lanes // 2

    def compute(buf, coef_v, out_v, t, half):
        for p in range(2):
            cols = [(p * pass_chunks + jj) * lanes for jj in range(pass_chunks)]
            if half == 0:
                init = tuple(jnp.zeros((lanes,), F32) for _ in range(2 * pass_chunks))
            else:
                init = (tuple(out_v[t, pl.ds(c, lanes)] for c in cols)
                        + tuple(out_v[t, pl.ds(words + c, lanes)] for c in cols))

            def row(r, acc):
                slot = jnp.full((lanes,), half * SC_HALF, I32) + r
                c = plsc.load_gather(coef_v, [jnp.full((lanes,), 0, I32) + t, slot])
                lo_acc, hi_acc = list(acc[:pass_chunks]), list(acc[pass_chunks:])
                for jj in range(pass_chunks):
                    lo, hi = _unpack_pair(buf[r, pl.ds(cols[jj], lanes)])
                    lo_acc[jj] = lo_acc[jj] + lo * c
                    hi_acc[jj] = hi_acc[jj] + hi * c
                return tuple(lo_acc) + tuple(hi_acc)

            acc = lax.fori_loop(0, SC_HALF, row, init)
            for jj in range(pass_chunks):
                out_v[t, pl.ds(cols[jj], lanes)] = acc[jj]
                out_v[t, pl.ds(words + cols[jj], lanes)] = acc[pass_chunks + jj]

    return _sc_token_driver(n_tok, 2 * words, table, idx, coef, compute)


def _coef_body(act_ref, gate_ref, o_ref):
    act = act_ref[...]
    o_ref[...] = gate_ref[...] * (act * 0.5 * (1.0 + lax.erf(act * (2.0 ** -0.5))))


def _final_body(x_ref, mix_ref, g_ref, b_ref, o_ref):
    o_ref[...] = _layernorm(ALPHA * x_ref[...] + mix_ref[...], g_ref[...], b_ref[...])


def _rowwise(body, name, n, widths_in, width_out, *args):
    tb = TOKEN_BLOCK
    rows = lambda w: pl.BlockSpec((tb, w), lambda i: (i, 0))
    n_rows_args = len(widths_in)
    return pl.pallas_call(
        body,
        grid=(pl.cdiv(n, tb),),
        in_specs=[rows(w) for w in widths_in] + [_full(a.shape) for a in args[n_rows_args:]],
        out_specs=rows(width_out),
        out_shape=jax.ShapeDtypeStruct((n, width_out), F32),
        compiler_params=_params(("parallel",)),
        name=name,
    )(*args)


def _group(x, shift0, wkv0, past, p):
    bsz, seq_len, _ = x.shape
    n = bsz * seq_len
    h, pr, q, k, v, lf = _in_proj(x.reshape(n, D_MODEL), p["ln_in_g"], p["ln_in_b"], p["w_r"], p["w_qkv"],
                                  p["w_f"], p["b_f"])
    pr = pr.reshape(bsz, seq_len, RWKV_IN)
    y_r, wkv_new = _rwkv(pr, shift0, wkv0, p)

    seq3 = lambda a: a.reshape(bsz, seq_len, -1)
    k3, v3, lf3 = seq3(k), seq3(v), seq3(lf)
    if past is None:
        k_all, v_all, lf_all, q_off, tq = k3, v3, lf3, 0, FOX_BLOCK
    else:
        k_past, v_past, lf_past = past
        q_off = k_past.shape[1]
        k_all = jnp.concatenate([k_past.reshape(bsz, q_off, D_HEADS), k3], axis=1)
        v_all = jnp.concatenate([v_past.reshape(bsz, q_off, D_HEADS), v3], axis=1)
        lf_all = jnp.concatenate([lf_past, lf3], axis=1)
        tq = seq_len
    c_col, c_row = _cumsum(lf_all)
    y_f = _fox(seq3(q), k_all, v_all, c_col, c_row, p["fox_norm_g"], q_off, tq, FOX_KV_BLOCK)

    h1 = _out_proj(y_r.reshape(n, D_HEADS), y_f.reshape(n, D_HEADS), h, p["w_out_r"], p["w_out_f"],
                   p["ln1_g"], p["ln1_b"])
    idx, gate = _topk(h1, p["peer_w_q"], p["peer_keys1"], p["peer_keys2"])
    act = _expert_dot(p["peer_u"], idx, h1)
    coef = _rowwise(_coef_body, "peer_coef", n, (PEER_SLOTS, PEER_SLOTS), PEER_SLOTS, act, gate)
    mix = _expert_sum(p["peer_v"], idx, coef)
    y = _rowwise(_final_body, "final_ln", n, (D_MODEL, D_MODEL), D_MODEL, h1, mix, p["ln2_g"], p["ln2_b"])

    heads = lambda a: a.reshape(1, bsz, seq_len, N_HEADS, HEAD)
    return (y.reshape(bsz, seq_len, D_MODEL), pr[None, :, seq_len - 1:, :], wkv_new[None],
            heads(k), heads(v), lf3[None])


def _pack_halves(a):
    bits = lax.bitcast_convert_type(a.astype(BF16), jnp.uint16).astype(jnp.uint32)
    w = a.shape[1] // 2
    return lax.bitcast_convert_type(bits[:, :w] | (bits[:, w:] << 16), I32)


def kernel(x_prompt, x_sample, state_rwkv_shift, state_rwkv_wkv, cache_fox_k, cache_fox_v, cache_fox_logf,
           meta_tokens, ln_in_g, ln_in_b, w_in, rwkv_mix, rwkv_w0, rwkv_w2, rwkv_a0, rwkv_a2, rwkv_g2,
           rwkv_k_k, rwkv_k_a, rwkv_r_k, rwkv_lnx_g, rwkv_lnx_b, fox_b_f, fox_norm_g, w_out, ln1_g, ln1_b,
           peer_w_q, peer_keys1, peer_keys2, peer_u, peer_v, ln2_g, ln2_b):
    assert w_in.shape[0] == 1, "single-layer step"
    bsz = x_prompt.shape[0]
    row = lambda a: a.reshape(1, -1)
    w = w_in[0]
    fox0 = RWKV_IN
    pad_f = LANES - N_HEADS
    p = dict(
        ln_in_g=row(ln_in_g), ln_in_b=row(ln_in_b),
        w_r=w[:, :RWKV_IN].astype(BF16),
        w_qkv=w[:, fox0:fox0 + 3 * D_HEADS].astype(BF16),
        w_f=jnp.pad(w[:, fox0 + 3 * D_HEADS:], ((0, 0), (0, pad_f))).astype(BF16),
        b_f=jnp.pad(row(fox_b_f[0]), ((0, 0), (0, pad_f))),
        rwkv_mix=rwkv_mix[0], rwkv_w0=rwkv_w0[0], rwkv_w2=rwkv_w2[0], rwkv_a0=rwkv_a0[0], rwkv_a2=rwkv_a2[0],
        rwkv_g2=rwkv_g2[0], rwkv_k_k=rwkv_k_k[0], rwkv_k_a=rwkv_k_a[0], rwkv_r_k=rwkv_r_k[0],
        rwkv_lnx_g=rwkv_lnx_g[0], rwkv_lnx_b=rwkv_lnx_b[0],
        fox_norm_g=row(fox_norm_g[0]),
        w_out_r=w_out[0, :D_HEADS].astype(BF16), w_out_f=w_out[0, D_HEADS:].astype(BF16),
        ln1_g=row(ln1_g[0]), ln1_b=row(ln1_b[0]),
        peer_w_q=peer_w_q[0].astype(BF16), peer_keys1=peer_keys1[0], peer_keys2=peer_keys2[0],
        peer_u=_pack_halves(peer_u[0]), peer_v=_pack_halves(peer_v[0]),
        ln2_g=row(ln2_g[0]), ln2_b=row(ln2_b[0]),
    )
    sizes = PROMPT_SLICES if sum(PROMPT_SLICES) == bsz else (bsz,)
    parts = []
    start = 0
    for pb in sizes:
        meta = jnp.broadcast_to(meta_tokens[None], (pb, N_META, D_MODEL))
        hp0 = jnp.concatenate([meta, x_prompt[start:start + pb]], axis=1)
        yp, *state = _group(hp0, jnp.zeros((pb, 1, RWKV_IN), F32), jnp.zeros((pb, N_HEADS, HEAD, HEAD), F32), None, p)
        parts.append((yp[:, N_META:], *state))
        start += pb
    ys, s_shift, s_wkv, s_k, s_v, s_lf = _group(
        x_sample, state_rwkv_shift[0], state_rwkv_wkv[0],
        (cache_fox_k[0], cache_fox_v[0], cache_fox_logf[0]), p)
    yp = jnp.concatenate([t[0] for t in parts], axis=0)
    p_shift, p_wkv, p_k, p_v, p_lf = (jnp.concatenate([t[j] for t in parts], axis=1) for j in range(1, 6))
    return (yp, ys, p_shift, p_wkv, p_k, p_v, p_lf, s_shift, s_wkv, s_k, s_v, s_lf)
```

```python
import functools

import jax
import jax.numpy as jnp
from jax import lax
from jax.experimental import pallas as pl
from jax.experimental.pallas import tpu as pltpu
from jax.experimental.pallas import tpu_sc as plsc

F32 = jnp.float32
BF16 = jnp.bfloat16
I32 = jnp.int32

D_MODEL = 1024
N_HEADS = 8
HEAD = 64
D_HEADS = N_HEADS * HEAD
LORA_W, LORA_A, LORA_G = 64, 64, 128
RWKV_IN = 3 * D_HEADS + LORA_W + LORA_A + LORA_G
N_META = 16
LNX_EPS = 64e-5
LN_EPS = 1e-5
ALPHA = 2.0 ** 0.25
N_KEYS = 128
PEER_HEADS = 8
PEER_TOPK = 16
PEER_SLOTS = PEER_HEADS * PEER_TOPK
D_KEY_HALF = 128

LANES = 128
CHUNK = 64
TOKEN_BLOCK = 256
FOX_BLOCK = 256
FOX_KV_BLOCK = 512
SC_CORES, SC_SUBCORES, SC_LANES = 2, 16, 16
SC_HALF = PEER_SLOTS // 2
SC_UNIT = 8
PROMPT_SLICES = (1, 1, 2, 2, 2, 2, 2, 2, 2)
VMEM_LIMIT = 48 * 1024 * 1024

HIGHEST = lax.Precision.HIGHEST


def _params(semantics):
    return pltpu.CompilerParams(dimension_semantics=semantics, vmem_limit_bytes=VMEM_LIMIT)


def _dot(a, b, precision=None):
    return jnp.dot(a, b, preferred_element_type=F32, precision=precision)


def _dot_nt(a, b, precision=None):
    return lax.dot_general(a, b, (((1,), (1,)), ((), ())), preferred_element_type=F32, precision=precision)


def _dot_tn(a, b, precision=None):
    return lax.dot_general(a, b, (((0,), (0,)), ((), ())), preferred_element_type=F32, precision=precision)


def _split3(a):
    a1 = a.astype(BF16)
    r1 = a - a1.astype(F32)
    a2 = r1.astype(BF16)
    a3 = (r1 - a2.astype(F32)).astype(BF16)
    return a1, a2, a3


def _split2(a):
    hi = a.astype(BF16)
    return hi, (a - hi.astype(F32)).astype(BF16)


def _mm3(dot, a, b):
    return dot(a[0], b[0]) + (dot(a[0], b[1]) + dot(a[1], b[0]))


def _dot_exact_rhs(a, m_bf16):
    a1, a2, a3 = _split3(a)
    return _dot(a1, m_bf16) + _dot(a2, m_bf16) + _dot(a3, m_bf16)


def _dot_exact_lhs(m_bf16, a):
    a1, a2, a3 = _split3(a)
    return _dot(m_bf16, a1) + _dot(m_bf16, a2) + _dot(m_bf16, a3)


def _layernorm(x, g, b):
    xc = x - jnp.mean(x, axis=-1, keepdims=True)
    var = jnp.mean(xc * xc, axis=-1, keepdims=True)
    return xc * lax.rsqrt(var + LN_EPS) * g + b


def _softplus(x):
    return jnp.maximum(x, 0.0) + jnp.log1p(jnp.exp(-jnp.abs(x)))


def _sigmoid(x):
    return 1.0 / (1.0 + jnp.exp(-x))


def _full(shape):
    return pl.BlockSpec(shape, lambda *_: (0,) * len(shape))


def _in_proj_body(x_ref, g_ref, b_ref, wr_ref, wqkv_ref, wf_ref, bf_ref,
                  h_ref, pr_ref, q_ref, k_ref, v_ref, lf_ref):
    h = _layernorm(x_ref[...], g_ref[...], b_ref[...])
    h_ref[...] = h
    hb = h.astype(BF16)
    pr_ref[...] = _dot(hb, wr_ref[...])
    qkv = _dot(hb, wqkv_ref[...])
    q_ref[...] = qkv[:, :D_HEADS]
    k_ref[...] = qkv[:, D_HEADS:2 * D_HEADS]
    v_ref[...] = qkv[:, 2 * D_HEADS:]
    fl = _dot(hb, wf_ref[...]) + bf_ref[...]
    lf = jnp.minimum(fl, 0.0) - jnp.log1p(jnp.exp(-jnp.abs(fl)))
    lf_ref[...] = lf[:, :N_HEADS]


def _in_proj(x, g, b, wr, wqkv, wf, bfp):
    n = x.shape[0]
    tb = TOKEN_BLOCK
    rows = lambda w: pl.BlockSpec((tb, w), lambda i: (i, 0))
    return pl.pallas_call(
        _in_proj_body,
        grid=(pl.cdiv(n, tb),),
        in_specs=[rows(D_MODEL), _full((1, D_MODEL)), _full((1, D_MODEL)), _full(wr.shape),
                  _full(wqkv.shape), _full(wf.shape), _full(bfp.shape)],
        out_specs=[rows(D_MODEL), rows(RWKV_IN), rows(D_HEADS), rows(D_HEADS), rows(D_HEADS), rows(N_HEADS)],
        out_shape=[jax.ShapeDtypeStruct((n, w), F32)
                   for w in (D_MODEL, RWKV_IN, D_HEADS, D_HEADS, D_HEADS, N_HEADS)],
        compiler_params=_params(("parallel",)),
        name="in_proj",
    )(x, g, b, wr, wqkv, wf, bfp)


def _rwkv_body(seq_len, n_chunks, pr_ref, shift0_ref, s0_ref, mix_ref, w0_ref, w2_ref, a0_ref, a2_ref,
               g2_ref, kk_ref, ka_ref, rk_ref, lng_ref, lnb_ref, seg_ref, tri_ref,
               y_ref, sout_ref, shift_sc, state_sc):
    c = pl.program_id(1)

    @pl.when(c == 0)
    def _():
        shift_sc[...] = shift0_ref[0]
        state_sc[...] = s0_ref[0]

    row = lax.broadcasted_iota(I32, (CHUNK, 1), 0)
    valid = (c * CHUNK + row) < seq_len
    prf = jnp.where(valid, pr_ref[0], 0.0)
    prev = jnp.where(row == 0, shift_sc[...], pltpu.roll(prf, 1, 0))
    shift_sc[...] = prf[CHUNK - 1:CHUNK, :]
    xm = prf + (prev - prf) * mix_ref[...]

    r = xm[:, :D_HEADS]
    k = xm[:, D_HEADS:2 * D_HEADS]
    v = xm[:, 2 * D_HEADS:3 * D_HEADS]
    o = 3 * D_HEADS
    xw = xm[:, o:o + LORA_W]
    xa = xm[:, o + LORA_W:o + LORA_W + LORA_A]
    xg = xm[:, o + LORA_W + LORA_A:]

    seg = seg_ref[...]
    wlog = -_softplus(-(w0_ref[...] + _dot(jnp.tanh(xw), w2_ref[...], HIGHEST))) - 0.5
    lw = jnp.where(valid, -jnp.exp(wlog), 0.0)
    a = _sigmoid(a0_ref[...] + _dot(xa, a2_ref[...], HIGHEST))
    g = _dot(_sigmoid(xg), g2_ref[...], HIGHEST)

    kk = k * kk_ref[...]
    kk_norm = jnp.sqrt(_dot_exact_rhs(kk * kk, seg))
    kk = jnp.where(valid, kk / jnp.maximum(kk_norm, 1e-12), 0.0)
    k = jnp.where(valid, k * (1.0 + (a - 1.0) * ka_ref[...]), 0.0)
    v = jnp.where(valid, v, 0.0)

    cl = _dot_exact_lhs(tri_ref[...], lw)
    dec_in = jnp.exp(cl)
    dec_out = jnp.exp(-cl)
    a_t = -kk * jnp.exp(cl - lw)
    b_t = kk * a * dec_out
    k_t = k * dec_out
    r_t = r * dec_in
    gamma = jnp.exp(cl[CHUNK - 1:CHUNK, :])

    ti = lax.broadcasted_iota(I32, (CHUNK, 2 * CHUNK), 0)
    si = lax.broadcasted_iota(I32, (CHUNK, 2 * CHUNK), 1)
    si = jnp.where(si >= CHUNK, si - CHUNK, si)
    strict = si < ti
    incl = si <= ti
    eye = (lax.broadcasted_iota(I32, (CHUNK, CHUNK), 0) == lax.broadcasted_iota(I32, (CHUNK, CHUNK), 1)).astype(F32)
    zeros = jnp.zeros((CHUNK, HEAD), F32)

    heads = range(N_HEADS)
    sls = [slice(h * HEAD, (h + 1) * HEAD) for h in heads]
    vh = [v[:, sl] for sl in sls]
    ar = [_split2(jnp.concatenate([a_t[:, sl], r_t[:, sl]], axis=0)) for sl in sls]
    bk = [_split2(jnp.concatenate([b_t[:, sl], k_t[:, sl]], axis=0)) for sl in sls]
    s0 = [state_sc[h] for h in heads]
    gram = [_mm3(_dot_nt, ar[h], bk[h]) for h in heads]
    top = [jnp.where(strict, gram[h][:CHUNK], 0.0) for h in heads]
    bot = [jnp.where(incl, gram[h][CHUNK:], 0.0) for h in heads]
    hs = [_mm3(_dot_nt, ar[h], _split2(s0[h])) for h in heads]
    rhs = [hs[h][:CHUNK] + _mm3(_dot, _split2(top[h]), _split2(jnp.concatenate([zeros, vh[h]], axis=0)))
           for h in heads]
    inv = [eye + top[h][:, :CHUNK] for h in heads]
    pws = [_split2(top[h][:, :CHUNK]) for h in heads]
    for _ in range(5):
        pws = [_split2(_mm3(_dot, pws[h], pws[h])) for h in heads]
        inv = [inv[h] + _mm3(_dot, _split2(inv[h]), pws[h]) for h in heads]
    u = [_mm3(_dot, _split2(inv[h]), _split2(rhs[h])) for h in heads]
    uv = [_split2(jnp.concatenate([u[h], vh[h]], axis=0)) for h in heads]
    ys = [hs[h][CHUNK:] + _dot(bot[h].astype(BF16), uv[h][0]) for h in heads]
    for h in heads:
        state_sc[h] = (s0[h] + _mm3(_dot_tn, uv[h], bk[h])) * gamma[:, sls[h]]

    y = jnp.concatenate(ys, axis=1)
    yc = y - _dot_exact_rhs(y, seg) * (1.0 / HEAD)
    var = _dot_exact_rhs(yc * yc, seg) * (1.0 / HEAD)
    y = yc * lax.rsqrt(var + LNX_EPS) * lng_ref[...] + lnb_ref[...]
    r = xm[:, :D_HEADS]
    bonus = _dot_exact_rhs(r * k * rk_ref[...], seg) * v
    y_ref[0] = (y + bonus) * g

    @pl.when(c == n_chunks - 1)
    def _():
        sout_ref[0] = state_sc[...]


def _rwkv(pr, shift0, wkv0, p):
    bsz, seq_len, _ = pr.shape
    n_chunks = pl.cdiv(seq_len, CHUNK)
    lane = jnp.arange(D_HEADS) // HEAD
    seg = (lane[:, None] == lane[None, :]).astype(BF16)
    tri = (jnp.arange(CHUNK)[None, :] <= jnp.arange(CHUNK)[:, None]).astype(BF16)
    vec = lambda a: a.reshape(1, -1)
    consts = [vec(p["rwkv_mix"]), vec(p["rwkv_w0"]), p["rwkv_w2"], vec(p["rwkv_a0"]), p["rwkv_a2"], p["rwkv_g2"],
              vec(p["rwkv_k_k"]), vec(p["rwkv_k_a"]), vec(p["rwkv_r_k"]), vec(p["rwkv_lnx_g"]),
              vec(p["rwkv_lnx_b"]), seg, tri]
    return pl.pallas_call(
        functools.partial(_rwkv_body, seq_len, n_chunks),
        grid=(bsz, n_chunks),
        in_specs=[pl.BlockSpec((1, CHUNK, RWKV_IN), lambda b, c: (b, c, 0)),
                  pl.BlockSpec((1, 1, RWKV_IN), lambda b, c: (b, 0, 0)),
                  pl.BlockSpec((1, N_HEADS, HEAD, HEAD), lambda b, c: (b, 0, 0, 0))]
                 + [_full(a.shape) for a in consts],
        out_specs=[pl.BlockSpec((1, CHUNK, D_HEADS), lambda b, c: (b, c, 0)),
                   pl.BlockSpec((1, N_HEADS, HEAD, HEAD), lambda b, c: (b, 0, 0, 0))],
        out_shape=[jax.ShapeDtypeStruct((bsz, seq_len, D_HEADS), F32),
                   jax.ShapeDtypeStruct((bsz, N_HEADS, HEAD, HEAD), F32)],
        scratch_shapes=[pltpu.VMEM((1, RWKV_IN), F32), pltpu.VMEM((N_HEADS, HEAD, HEAD), F32)],
        compiler_params=_params(("parallel", "arbitrary")),
        name="rwkv",
    )(pr, shift0, wkv0, *consts)


def _cumsum_body(seq_len, lf_ref, col_ref, row_ref):
    carry = jnp.zeros((1, N_HEADS), F32)
    eye = (lax.broadcasted_iota(I32, (N_HEADS, N_HEADS), 0)
           == lax.broadcasted_iota(I32, (N_HEADS, N_HEADS), 1)).astype(BF16)
    for start in range(0, seq_len, LANES):
        n = min(LANES, seq_len - start)
        tri = (lax.broadcasted_iota(I32, (n, n), 1) <= lax.broadcasted_iota(I32, (n, n), 0)).astype(BF16)
        cum = _dot_exact_lhs(tri, lf_ref[0, start:start + n, :]) + carry
        col_ref[0, start:start + n, :] = cum
        c1, c2, c3 = _split3(cum)
        row_ref[0, :, start:start + n] = _dot_nt(eye, c1) + _dot_nt(eye, c2) + _dot_nt(eye, c3)
        carry = cum[n - 1:n, :]


def _cumsum(lf):
    bsz, seq_len, _ = lf.shape
    return pl.pallas_call(
        functools.partial(_cumsum_body, seq_len),
        grid=(bsz,),
        in_specs=[pl.BlockSpec((1, seq_len, N_HEADS), lambda b: (b, 0, 0))],
        out_specs=[pl.BlockSpec((1, seq_len, N_HEADS), lambda b: (b, 0, 0)),
                   pl.BlockSpec((1, N_HEADS, seq_len), lambda b: (b, 0, 0))],
        out_shape=[jax.ShapeDtypeStruct((bsz, seq_len, N_HEADS), F32),
                   jax.ShapeDtypeStruct((bsz, N_HEADS, seq_len), F32)],
        compiler_params=_params(("parallel",)),
        name="cumsum",
    )(lf)


def _fox_body(kv_len, q_off, tq, tk, n_kv, q_ref, k_ref, v_ref, cq_ref, ck_ref, g_ref, o_ref, m_sc, l_sc, acc_sc):
    qi = pl.program_id(1)
    ki = pl.program_id(2)

    @pl.when(ki == 0)
    def _():
        m_sc[...] = jnp.full(m_sc.shape, -jnp.inf, F32)
        l_sc[...] = jnp.zeros(l_sc.shape, F32)
        acc_sc[...] = jnp.zeros(acc_sc.shape, F32)

    q_start = q_off + qi * tq
    k_start = ki * tk

    @pl.when(k_start <= q_start + tq - 1)
    def _():
        rows = q_start + lax.broadcasted_iota(I32, (tq, 1), 0)
        cols = k_start + lax.broadcasted_iota(I32, (1, tk), 1)
        mask = cols <= rows
        k_valid = (k_start + lax.broadcasted_iota(I32, (tk, 1), 0)) < kv_len
        q = q_ref[0] * (HEAD ** -0.5)
        kb = k_ref[0].astype(BF16)
        vb = jnp.where(k_valid, v_ref[0], 0.0).astype(BF16)
        cq = cq_ref[0]
        ck = ck_ref[0]
        ones = jnp.ones((tk, LANES), BF16)
        heads = range(N_HEADS)
        sls = [slice(h * HEAD, (h + 1) * HEAD) for h in heads]
        qb = q.astype(BF16)
        s = [_dot_nt(qb[:, sl], kb[:, sl]) for sl in sls]
        s = [jnp.where(mask, s[h] + (cq[:, h:h + 1] - ck[h:h + 1, :]), -jnp.inf) for h in heads]
        m_prev = [m_sc[h] for h in heads]
        m_new = [jnp.maximum(m_prev[h], jnp.max(s[h], axis=-1, keepdims=True)) for h in heads]
        scale = [jnp.exp(m_prev[h] - m_new[h]) for h in heads]
        pexp = [jnp.exp(s[h] - m_new[h]).astype(BF16) for h in heads]
        psum = [_dot(pexp[h], ones)[:, :1] for h in heads]
        pv = [_dot(pexp[h], vb[:, sls[h]]) for h in heads]
        for h in heads:
            l_sc[h] = scale[h] * l_sc[h] + psum[h]
            acc_sc[:, sls[h]] = scale[h] * acc_sc[:, sls[h]] + pv[h]
            m_sc[h] = m_new[h]

    @pl.when(ki == n_kv - 1)
    def _():
        for h in range(N_HEADS):
            sl = slice(h * HEAD, (h + 1) * HEAD)
            o = acc_sc[:, sl] / l_sc[h]
            o = o * lax.rsqrt(jnp.mean(o * o, axis=-1, keepdims=True) + 1e-6)
            acc_sc[:, sl] = o
        o_ref[0] = acc_sc[...] * g_ref[...]


def _fox(q, k_all, v_all, c_col, c_row, norm_g, q_off, tq, tk):
    bsz, q_len, _ = q.shape
    kv_len = k_all.shape[1]
    n_q = pl.cdiv(q_len, tq)
    n_kv = pl.cdiv(kv_len, tk)
    assert q_off % tq == 0
    last_kv = lambda qi: (q_off + (qi + 1) * tq - 1) // tk
    kv_map = lambda b, qi, ki: (b, jnp.minimum(ki, last_kv(qi)), 0)
    return pl.pallas_call(
        functools.partial(_fox_body, kv_len, q_off, tq, tk, n_kv),
        grid=(bsz, n_q, n_kv),
        in_specs=[pl.BlockSpec((1, tq, D_HEADS), lambda b, qi, ki: (b, qi, 0)),
                  pl.BlockSpec((1, tk, D_HEADS), kv_map),
                  pl.BlockSpec((1, tk, D_HEADS), kv_map),
                  pl.BlockSpec((1, tq, N_HEADS), lambda b, qi, ki: (b, q_off // tq + qi, 0)),
                  pl.BlockSpec((1, N_HEADS, tk), lambda b, qi, ki: (b, 0, jnp.minimum(ki, last_kv(qi)))),
                  _full((1, D_HEADS))],
        out_specs=pl.BlockSpec((1, tq, D_HEADS), lambda b, qi, ki: (b, qi, 0)),
        out_shape=jax.ShapeDtypeStruct((bsz, q_len, D_HEADS), F32),
        scratch_shapes=[pltpu.VMEM((N_HEADS, tq, 1), F32), pltpu.VMEM((N_HEADS, tq, 1), F32),
                        pltpu.VMEM((tq, D_HEADS), F32)],
        compiler_params=_params(("parallel", "parallel", "arbitrary")),
        name="fox",
    )(q, k_all, v_all, c_col, c_row, norm_g)


def _out_proj_body(yr_ref, yf_ref, h_ref, wr_ref, wf_ref, g_ref, b_ref, o_ref):
    mix = _dot(yr_ref[...].astype(BF16), wr_ref[...]) + _dot(yf_ref[...].astype(BF16), wf_ref[...])
    o_ref[...] = _layernorm(ALPHA * h_ref[...] + mix, g_ref[...], b_ref[...])


def _out_proj(yr, yf, h, wr, wf, g, b):
    n = h.shape[0]
    tb = TOKEN_BLOCK
    rows = lambda w: pl.BlockSpec((tb, w), lambda i: (i, 0))
    return pl.pallas_call(
        _out_proj_body,
        grid=(pl.cdiv(n, tb),),
        in_specs=[rows(D_HEADS), rows(D_HEADS), rows(D_MODEL), _full(wr.shape), _full(wf.shape),
                  _full((1, D_MODEL)), _full((1, D_MODEL))],
        out_specs=rows(D_MODEL),
        out_shape=jax.ShapeDtypeStruct((n, D_MODEL), F32),
        compiler_params=_params(("parallel",)),
        name="out_proj",
    )(yr, yf, h, wr, wf, g, b)


_BIG = 2 ** 30
_CAND_COUNTS = tuple(PEER_TOPK // (i + 1) for i in range(PEER_TOPK))
_N_CAND = sum(_CAND_COUNTS)
_CAND_ROWS = -(-_N_CAND // 8) * 8


def _top_rows(s, order, k, payload=None):
    vals, outs = [], []
    for _ in range(k):
        m = jnp.max(s, axis=0, keepdims=True)
        first = jnp.min(jnp.where(s == m, order, _BIG), axis=0, keepdims=True)
        sel = order == first
        vals.append(m)
        outs.append(first if payload is None else jnp.max(jnp.where(sel, payload, -1), axis=0, keepdims=True))
        s = jnp.where(sel, -jnp.inf, s)
    return jnp.concatenate(vals, axis=0), jnp.concatenate(outs, axis=0)


def _topk_body(h_ref, wq_ref, k1_ref, k2_ref, pos_ref, idx_ref, gate_ref, cand_s_sc, cand_i_sc):
    tb = h_ref.shape[0]
    q = _dot(h_ref[...].astype(BF16), wq_ref[...]).astype(BF16)
    key_id = lax.broadcasted_iota(I32, (N_KEYS, LANES), 0)
    cand_pos = pos_ref[...]
    k1 = k1_ref[...].astype(BF16)
    k2 = k2_ref[...].astype(BF16)
    cand_s_sc[_N_CAND:, :] = jnp.full((_CAND_ROWS - _N_CAND, LANES), -jnp.inf, F32)
    cand_i_sc[_N_CAND:, :] = jnp.zeros((_CAND_ROWS - _N_CAND, LANES), I32)
    for part in range(tb // LANES):
        qp = q[part * LANES:(part + 1) * LANES]
        ids, gates = [], []
        for h in range(PEER_HEADS):
            o = h * 2 * D_KEY_HALF
            s1 = _dot_nt(k1, qp[:, o:o + D_KEY_HALF])
            s2 = _dot_nt(k2, qp[:, o + D_KEY_HALF:o + 2 * D_KEY_HALF])
            v1, i1 = _top_rows(s1, key_id, PEER_TOPK)
            v2, i2 = _top_rows(s2, key_id, PEER_TOPK)
            off = 0
            for i, cnt in enumerate(_CAND_COUNTS):
                cand_s_sc[off:off + cnt, :] = v1[i:i + 1] + v2[:cnt]
                cand_i_sc[off:off + cnt, :] = i1[i:i + 1] * N_KEYS + i2[:cnt]
                off += cnt
            best_s, best_i = _top_rows(cand_s_sc[...], cand_pos, PEER_TOPK, payload=cand_i_sc[...])
            e = jnp.exp(best_s - best_s[0:1])
            ids.append(best_i)
            gates.append(e / jnp.sum(e, axis=0, keepdims=True))
        rows = slice(part * LANES, (part + 1) * LANES)
        idx_ref[rows, :] = jnp.concatenate(ids, axis=0).T
        gate_ref[rows, :] = jnp.concatenate(gates, axis=0).T


def _topk(h1, wq, keys1, keys2):
    n = h1.shape[0]
    tb = TOKEN_BLOCK
    pos = [i * PEER_TOPK + j for i, cnt in enumerate(_CAND_COUNTS) for j in range(cnt)]
    pos = jnp.asarray(pos + [_BIG] * (_CAND_ROWS - _N_CAND), I32)
    pos = jnp.broadcast_to(pos[:, None], (_CAND_ROWS, LANES))
    return pl.pallas_call(
        _topk_body,
        grid=(pl.cdiv(n, tb),),
        in_specs=[pl.BlockSpec((tb, D_MODEL), lambda i: (i, 0)), _full(wq.shape), _full(keys1.shape),
                  _full(keys2.shape), _full(pos.shape)],
        out_specs=[pl.BlockSpec((tb, PEER_SLOTS), lambda i: (i, 0)), pl.BlockSpec((tb, PEER_SLOTS), lambda i: (i, 0))],
        out_shape=[jax.ShapeDtypeStruct((n, PEER_SLOTS), I32), jax.ShapeDtypeStruct((n, PEER_SLOTS), F32)],
        scratch_shapes=[pltpu.VMEM((_CAND_ROWS, LANES), F32), pltpu.VMEM((_CAND_ROWS, LANES), I32)],
        compiler_params=_params(("parallel",)),
        name="peer_topk",
    )(h1, wq, keys1, keys2, pos)


def _unpack_pair(w):
    return lax.bitcast_convert_type(w << 16, F32), lax.bitcast_convert_type(w & jnp.int32(-65536), F32)


_ERF_NUM = (-2.72614225801306e-10, 2.77068142495902e-08, -2.10102402082508e-06, -5.69250639462346e-05,
            -7.34990630326855e-04, -2.95459980854025e-03, -1.60960333262415e-02)
_ERF_DEN = (-1.45660718464996e-05, -2.13374055278905e-04, -1.68282697438203e-03, -7.37332916720468e-03,
            -1.42647390514189e-02)


def _erf_f32(x):
    x = jnp.minimum(jnp.maximum(x, -4.0), 4.0)
    x2 = x * x
    num = x2 * _ERF_NUM[0] + _ERF_NUM[1]
    for c in _ERF_NUM[2:]:
        num = num * x2 + c
    den = x2 * _ERF_DEN[0] + _ERF_DEN[1]
    for c in _ERF_DEN[2:]:
        den = den * x2 + c
    return x * num / den


def _expert_mix(u_tab, v_tab, idx, x, gate):
    n_tok, words = x.shape[0], u_tab.shape[1]
    lanes = SC_LANES
    pass_chunks = words // lanes // 2
    n_workers = SC_CORES * SC_SUBCORES
    assert n_tok % SC_UNIT == 0
    base, extra = divmod(n_tok // SC_UNIT, n_workers)
    idx_rows = 2 * SC_UNIT
    mesh = plsc.VectorSubcoreMesh(core_axis_name="c", subcore_axis_name="s",
                                  num_cores=SC_CORES, num_subcores=SC_SUBCORES)

    @functools.partial(
        pl.kernel, mesh=mesh, compiler_params=pltpu.CompilerParams(needs_layout_passes=False),
        out_type=jax.ShapeDtypeStruct((n_tok, 2 * words), F32),
        scratch_types=[pltpu.VMEM((2 * idx_rows, SC_HALF), I32),
                       pltpu.VMEM((SC_HALF, words), I32), pltpu.VMEM((SC_HALF, words), I32),
                       pltpu.VMEM((2 * SC_UNIT, 2 * words), F32),
                       pltpu.VMEM((2 * SC_UNIT, PEER_SLOTS), F32),
                       pltpu.VMEM((SC_UNIT, PEER_SLOTS), F32),
                       pltpu.VMEM((2 * SC_UNIT, 2 * words), F32),
                       pltpu.SemaphoreType.DMA, pltpu.SemaphoreType.DMA, pltpu.SemaphoreType.DMA,
                       pltpu.SemaphoreType.DMA((2,))],
    )
    def sc_kernel(u_hbm, v_hbm, idx_hbm, x_hbm, gate_hbm, out_hbm, idx_v, buf_a, buf_b, x_v, gate_v, act_v, out_v,
                  sem_a, sem_b, sem_in, sem_out):
        worker = lax.axis_index("s") * SC_CORES + lax.axis_index("c")
        first_unit = worker * base + jnp.minimum(worker, extra)
        my_units = base + (worker < extra).astype(I32)
        lane = lax.iota(I32, lanes)

        def first_token(ui):
            return pl.multiple_of((first_unit + ui) * SC_UNIT, SC_UNIT)

        def stage_in(ui, slot):
            tok0 = first_token(ui)
            toks = pl.ds(tok0, SC_UNIT)
            dst = pl.ds(slot * SC_UNIT, SC_UNIT)
            rows = pl.ds(pl.multiple_of(tok0 * 2, idx_rows), idx_rows)
            return (pltpu.make_async_copy(idx_hbm.at[rows], idx_v.at[pl.ds(slot * idx_rows, idx_rows)], sem_in),
                    pltpu.make_async_copy(x_hbm.at[toks], x_v.at[dst], sem_in),
                    pltpu.make_async_copy(gate_hbm.at[toks], gate_v.at[dst], sem_in))

        def write_out(ui, slot):
            return pltpu.make_async_copy(out_v.at[pl.ds(slot * SC_UNIT, SC_UNIT)],
                                         out_hbm.at[pl.ds(first_token(ui), SC_UNIT)], sem_out.at[slot])

        def gather(table_hbm, row, buf, sem):
            return pltpu.make_async_copy(table_hbm.at[idx_v.at[row]], buf, sem)

        def dot(buf, t, xrow, half):
            @pl.loop(0, SC_HALF // lanes)
            def _(g):
                def chunk(j, acc):
                    x_lo = x_v[xrow, pl.ds(j * lanes, lanes)]
                    x_hi = x_v[xrow, pl.ds(words + j * lanes, lanes)]
                    out = []
                    for r in range(lanes):
                        lo, hi = _unpack_pair(buf[g * lanes + r, pl.ds(j * lanes, lanes)])
                        out.append(acc[r] + lo * x_lo + hi * x_hi)
                    return tuple(out)

                acc = lax.fori_loop(0, words // lanes, chunk, tuple(jnp.zeros((lanes,), F32) for _ in range(lanes)))
                res = jnp.zeros((lanes,), F32)
                for r in range(lanes):
                    res = jnp.where(lane == r, jnp.sum(acc[r]), res)
                act_v[t, pl.ds(half * SC_HALF + g * lanes, lanes)] = res

        def gate_gelu(t, xrow):
            for g in range(PEER_SLOTS // lanes):
                cols = pl.ds(g * lanes, lanes)
                a = act_v[t, cols]
                act_v[t, cols] = gate_v[xrow, cols] * (a * 0.5 * (1.0 + _erf_f32(a * (2.0 ** -0.5))))

        def vsum(buf, t, orow, half):
            for p in range(2):
                cols = [(p * pass_chunks + jj) * lanes for jj in range(pass_chunks)]
                if half == 0:
                    init = tuple(jnp.zeros((lanes,), F32) for _ in range(2 * pass_chunks))
                else:
                    init = (tuple(out_v[orow, pl.ds(c, lanes)] for c in cols)
                            + tuple(out_v[orow, pl.ds(words + c, lanes)] for c in cols))

                def row(r, acc):
                    slot = jnp.full((lanes,), half * SC_HALF, I32) + r
                    c = plsc.load_gather(act_v, [jnp.full((lanes,), 0, I32) + t, slot])
                    lo_acc, hi_acc = list(acc[:pass_chunks]), list(acc[pass_chunks:])
                    for jj in range(pass_chunks):
                        lo, hi = _unpack_pair(buf[r, pl.ds(cols[jj], lanes)])
                        lo_acc[jj] = lo_acc[jj] + lo * c
                        hi_acc[jj] = hi_acc[jj] + hi * c
                    return tuple(lo_acc) + tuple(hi_acc)

                acc = lax.fori_loop(0, SC_HALF, row, init)
                for jj in range(pass_chunks):
                    out_v[orow, pl.ds(cols[jj], lanes)] = acc[jj]
                    out_v[orow, pl.ds(words + cols[jj], lanes)] = acc[pass_chunks + jj]

        def unit_body(ui, carry):
            slot = lax.rem(ui, 2)
            has_next = ui + 1 < my_units

            @pl.when(has_next)
            def _():
                for c in stage_in(ui + 1, 1 - slot):
                    c.start()

            @pl.when(ui >= 2)
            def _():
                write_out(ui - 2, slot).wait()

            @pl.loop(0, SC_UNIT)
            def _(t):
                r0 = slot * idx_rows + 2 * t
                urow = slot * SC_UNIT + t
                gather(u_hbm, r0 + 1, buf_b, sem_b).start()
                gather(u_hbm, r0, buf_a, sem_a).wait()
                dot(buf_a, t, urow, 0)
                gather(v_hbm, r0, buf_a, sem_a).start()
                gather(u_hbm, r0 + 1, buf_b, sem_b).wait()
                dot(buf_b, t, urow, 1)
                gate_gelu(t, urow)
                gather(v_hbm, r0 + 1, buf_b, sem_b).start()
                gather(v_hbm, r0, buf_a, sem_a).wait()
                vsum(buf_a, t, urow, 0)

                @pl.when(t + 1 < SC_UNIT)
                def _():
                    gather(u_hbm, r0 + 2, buf_a, sem_a).start()

                @pl.when((t + 1 == SC_UNIT) & has_next)
                def _():
                    for c in stage_in(ui + 1, 1 - slot):
                        c.wait()
                    gather(u_hbm, (1 - slot) * idx_rows, buf_a, sem_a).start()

                gather(v_hbm, r0 + 1, buf_b, sem_b).wait()
                vsum(buf_b, t, urow, 1)

            write_out(ui, slot).start()
            return carry

        @pl.when(my_units > 0)
        def _():
            for c in stage_in(0, 0):
                c.start()
            for c in stage_in(0, 0):
                c.wait()
            gather(u_hbm, 0, buf_a, sem_a).start()
            lax.fori_loop(0, my_units, unit_body, 0)

            @pl.when(my_units >= 2)
            def _():
                write_out(my_units - 2, lax.rem(my_units, 2)).wait()

            write_out(my_units - 1, lax.rem(my_units - 1, 2)).wait()

    return sc_kernel(u_tab, v_tab, idx.reshape(n_tok * 2, SC_HALF), x, gate)


def _final_body(x_ref, mix_ref, g_ref, b_ref, o_ref):
    o_ref[...] = _layernorm(ALPHA * x_ref[...] + mix_ref[...], g_ref[...], b_ref[...])


def _rowwise(body, name, n, widths_in, width_out, *args):
    tb = TOKEN_BLOCK
    rows = lambda w: pl.BlockSpec((tb, w), lambda i: (i, 0))
    n_rows_args = len(widths_in)
    return pl.pallas_call(
        body,
        grid=(pl.cdiv(n, tb),),
        in_specs=[rows(w) for w in widths_in] + [_full(a.shape) for a in args[n_rows_args:]],
        out_specs=rows(width_out),
        out_shape=jax.ShapeDtypeStruct((n, width_out), F32),
        compiler_params=_params(("parallel",)),
        name=name,
    )(*args)


def _group(x, shift0, wkv0, past, p):
    bsz, seq_len, _ = x.shape
    n = bsz * seq_len
    h, pr, q, k, v, lf = _in_proj(x.reshape(n, D_MODEL), p["ln_in_g"], p["ln_in_b"], p["w_r"], p["w_qkv"],
                                  p["w_f"], p["b_f"])
    pr = pr.reshape(bsz, seq_len, RWKV_IN)
    y_r, wkv_new = _rwkv(pr, shift0, wkv0, p)

    seq3 = lambda a: a.reshape(bsz, seq_len, -1)
    k3, v3, lf3 = seq3(k), seq3(v), seq3(lf)
    if past is None:
        k_all, v_all, lf_all, q_off, tq = k3, v3, lf3, 0, FOX_BLOCK
    else:
        k_past, v_past, lf_past = past
        q_off = k_past.shape[1]
        k_all = jnp.concatenate([k_past.reshape(bsz, q_off, D_HEADS), k3], axis=1)
        v_all = jnp.concatenate([v_past.reshape(bsz, q_off, D_HEADS), v3], axis=1)
        lf_all = jnp.concatenate([lf_past, lf3], axis=1)
        tq = seq_len
    c_col, c_row = _cumsum(lf_all)
    y_f = _fox(seq3(q), k_all, v_all, c_col, c_row, p["fox_norm_g"], q_off, tq, FOX_KV_BLOCK)

    h1 = _out_proj(y_r.reshape(n, D_HEADS), y_f.reshape(n, D_HEADS), h, p["w_out_r"], p["w_out_f"],
                   p["ln1_g"], p["ln1_b"])
    idx, gate = _topk(h1, p["peer_w_q"], p["peer_keys1"], p["peer_keys2"])
    mix = _expert_mix(p["peer_u"], p["peer_v"], idx, h1, gate)
    y = _rowwise(_final_body, "final_ln", n, (D_MODEL, D_MODEL), D_MODEL, h1, mix, p["ln2_g"], p["ln2_b"])

    heads = lambda a: a.reshape(1, bsz, seq_len, N_HEADS, HEAD)
    return (y.reshape(bsz, seq_len, D_MODEL), pr[None, :, seq_len - 1:, :], wkv_new[None],
            heads(k), heads(v), lf3[None])


def _pack_halves(a):
    bits = lax.bitcast_convert_type(a.astype(BF16), jnp.uint16).astype(jnp.uint32)
    w = a.shape[1] // 2
    return lax.bitcast_convert_type(bits[:, :w] | (bits[:, w:] << 16), I32)


def kernel(x_prompt, x_sample, state_rwkv_shift, state_rwkv_wkv, cache_fox_k, cache_fox_v, cache_fox_logf,
           meta_tokens, ln_in_g, ln_in_b, w_in, rwkv_mix, rwkv_w0, rwkv_w2, rwkv_a0, rwkv_a2, rwkv_g2,
           rwkv_k_k, rwkv_k_a, rwkv_r_k, rwkv_lnx_g, rwkv_lnx_b, fox_b_f, fox_norm_g, w_out, ln1_g, ln1_b,
           peer_w_q, peer_keys1, peer_keys2, peer_u, peer_v, ln2_g, ln2_b):
    assert w_in.shape[0] == 1, "single-layer step"
    bsz = x_prompt.shape[0]
    row = lambda a: a.reshape(1, -1)
    w = w_in[0]
    fox0 = RWKV_IN
    pad_f = LANES - N_HEADS
    p = dict(
        ln_in_g=row(ln_in_g), ln_in_b=row(ln_in_b),
        w_r=w[:, :RWKV_IN].astype(BF16),
        w_qkv=w[:, fox0:fox0 + 3 * D_HEADS].astype(BF16),
        w_f=jnp.pad(w[:, fox0 + 3 * D_HEADS:], ((0, 0), (0, pad_f))).astype(BF16),
        b_f=jnp.pad(row(fox_b_f[0]), ((0, 0), (0, pad_f))),
        rwkv_mix=rwkv_mix[0], rwkv_w0=rwkv_w0[0], rwkv_w2=rwkv_w2[0], rwkv_a0=rwkv_a0[0], rwkv_a2=rwkv_a2[0],
        rwkv_g2=rwkv_g2[0], rwkv_k_k=rwkv_k_k[0], rwkv_k_a=rwkv_k_a[0], rwkv_r_k=rwkv_r_k[0],
        rwkv_lnx_g=rwkv_lnx_g[0], rwkv_lnx_b=rwkv_lnx_b[0],
        fox_norm_g=row(fox_norm_g[0]),
        w_out_r=w_out[0, :D_HEADS].astype(BF16), w_out_f=w_out[0, D_HEADS:].astype(BF16),
        ln1_g=row(ln1_g[0]), ln1_b=row(ln1_b[0]),
        peer_w_q=peer_w_q[0].astype(BF16), peer_keys1=peer_keys1[0], peer_keys2=peer_keys2[0],
        peer_u=_pack_halves(peer_u[0]), peer_v=_pack_halves(peer_v[0]),
        ln2_g=row(ln2_g[0]), ln2_b=row(ln2_b[0]),
    )
    sizes = PROMPT_SLICES if sum(PROMPT_SLICES) == bsz else (bsz,)
    parts = []
    start = 0
    for pb in sizes:
        meta = jnp.broadcast_to(meta_tokens[None], (pb, N_META, D_MODEL))
        hp0 = jnp.concatenate([meta, x_prompt[start:start + pb]], axis=1)
        yp, *state = _group(hp0, jnp.zeros((pb, 1, RWKV_IN), F32), jnp.zeros((pb, N_HEADS, HEAD, HEAD), F32), None, p)
        parts.append((yp[:, N_META:], *state))
        start += pb
    ys, s_shift, s_wkv, s_k, s_v, s_lf = _group(
        x_sample, state_rwkv_shift[0], state_rwkv_wkv[0],
        (cache_fox_k[0], cache_fox_v[0], cache_fox_logf[0]), p)
    yp = jnp.concatenate([t[0] for t in parts], axis=0)
    p_shift, p_wkv, p_k, p_v, p_lf = (jnp.concatenate([t[j] for t in parts], axis=1) for j in range(1, 6))
    return (yp, ys, p_shift, p_wkv, p_k, p_v, p_lf, s_shift, s_wkv, s_k, s_v, s_lf)
```

```python
import functools

import jax
import jax.numpy as jnp
from jax import lax
from jax.experimental import pallas as pl
from jax.experimental.pallas import tpu as pltpu
from jax.experimental.pallas import tpu_sc as plsc

F32 = jnp.float32
BF16 = jnp.bfloat16
I32 = jnp.int32

D_MODEL = 1024
N_HEADS = 8
HEAD = 64
D_HEADS = N_HEADS * HEAD
LORA_W, LORA_A, LORA_G = 64, 64, 128
RWKV_IN = 3 * D_HEADS + LORA_W + LORA_A + LORA_G
N_META = 16
LNX_EPS = 64e-5
LN_EPS = 1e-5
ALPHA = 2.0 ** 0.25
N_KEYS = 128
PEER_HEADS = 8
PEER_TOPK = 16
PEER_SLOTS = PEER_HEADS * PEER_TOPK
D_KEY_HALF = 128

LANES = 128
CHUNK = 64
TOKEN_BLOCK = 256
FOX_BLOCK = 256
FOX_KV_BLOCK = 512
SC_CORES, SC_SUBCORES, SC_LANES = 2, 16, 16
SC_HALF = PEER_SLOTS // 2
SC_UNIT = 8
PROMPT_SLICES = (1, 1, 2, 2, 2, 2, 2, 2, 2)
VMEM_LIMIT = 48 * 1024 * 1024

HIGHEST = lax.Precision.HIGHEST


def _params(semantics):
    return pltpu.CompilerParams(dimension_semantics=semantics, vmem_limit_bytes=VMEM_LIMIT)


def _dot(a, b, precision=None):
    return jnp.dot(a, b, preferred_element_type=F32, precision=precision)


def _dot_nt(a, b, precision=None):
    return lax.dot_general(a, b, (((1,), (1,)), ((), ())), preferred_element_type=F32, precision=precision)


def _dot_tn(a, b, precision=None):
    return lax.dot_general(a, b, (((0,), (0,)), ((), ())), preferred_element_type=F32, precision=precision)


def _split3(a):
    a1 = a.astype(BF16)
    r1 = a - a1.astype(F32)
    a2 = r1.astype(BF16)
    a3 = (r1 - a2.astype(F32)).astype(BF16)
    return a1, a2, a3


def _split2(a):
    hi = a.astype(BF16)
    return hi, (a - hi.astype(F32)).astype(BF16)


def _mm3(dot, a, b):
    return dot(a[0], b[0]) + (dot(a[0], b[1]) + dot(a[1], b[0]))


def _dot_exact_rhs(a, m_bf16):
    a1, a2, a3 = _split3(a)
    return _dot(a1, m_bf16) + _dot(a2, m_bf16) + _dot(a3, m_bf16)


def _dot_exact_lhs(m_bf16, a):
    a1, a2, a3 = _split3(a)
    return _dot(m_bf16, a1) + _dot(m_bf16, a2) + _dot(m_bf16, a3)


def _layernorm(x, g, b):
    xc = x - jnp.mean(x, axis=-1, keepdims=True)
    var = jnp.mean(xc * xc, axis=-1, keepdims=True)
    return xc * lax.rsqrt(var + LN_EPS) * g + b


def _softplus(x):
    return jnp.maximum(x, 0.0) + jnp.log1p(jnp.exp(-jnp.abs(x)))


def _sigmoid(x):
    return 1.0 / (1.0 + jnp.exp(-x))


def _full(shape):
    return pl.BlockSpec(shape, lambda *_: (0,) * len(shape))


def _in_proj_body(x_ref, g_ref, b_ref, wr_ref, wqkv_ref, wf_ref, bf_ref,
                  h_ref, pr_ref, q_ref, k_ref, v_ref, lf_ref):
    h = _layernorm(x_ref[...], g_ref[...], b_ref[...])
    h_ref[...] = h
    hb = h.astype(BF16)
    pr_ref[...] = _dot(hb, wr_ref[...])
    qkv = _dot(hb, wqkv_ref[...])
    q_ref[...] = qkv[:, :D_HEADS]
    k_ref[...] = qkv[:, D_HEADS:2 * D_HEADS]
    v_ref[...] = qkv[:, 2 * D_HEADS:]
    fl = _dot(hb, wf_ref[...]) + bf_ref[...]
    lf = jnp.minimum(fl, 0.0) - jnp.log1p(jnp.exp(-jnp.abs(fl)))
    lf_ref[...] = lf[:, :N_HEADS]


def _in_proj(x, g, b, wr, wqkv, wf, bfp):
    n = x.shape[0]
    tb = TOKEN_BLOCK
    rows = lambda w: pl.BlockSpec((tb, w), lambda i: (i, 0))
    return pl.pallas_call(
        _in_proj_body,
        grid=(pl.cdiv(n, tb),),
        in_specs=[rows(D_MODEL), _full((1, D_MODEL)), _full((1, D_MODEL)), _full(wr.shape),
                  _full(wqkv.shape), _full(wf.shape), _full(bfp.shape)],
        out_specs=[rows(D_MODEL), rows(RWKV_IN), rows(D_HEADS), rows(D_HEADS), rows(D_HEADS), rows(N_HEADS)],
        out_shape=[jax.ShapeDtypeStruct((n, w), F32)
                   for w in (D_MODEL, RWKV_IN, D_HEADS, D_HEADS, D_HEADS, N_HEADS)],
        compiler_params=_params(("parallel",)),
        name="in_proj",
    )(x, g, b, wr, wqkv, wf, bfp)


def _rwkv_body(seq_len, n_chunks, pr_ref, shift0_ref, s0_ref, mix_ref, w0_ref, w2_ref, a0_ref, a2_ref,
               g2_ref, kk_ref, ka_ref, rk_ref, lng_ref, lnb_ref, seg_ref, tri_ref,
               y_ref, sout_ref, shift_sc, state_sc):
    c = pl.program_id(1)

    @pl.when(c == 0)
    def _():
        shift_sc[...] = shift0_ref[0]
        state_sc[...] = s0_ref[0]

    row = lax.broadcasted_iota(I32, (CHUNK, 1), 0)
    valid = (c * CHUNK + row) < seq_len
    prf = jnp.where(valid, pr_ref[0], 0.0)
    prev = jnp.where(row == 0, shift_sc[...], pltpu.roll(prf, 1, 0))
    shift_sc[...] = prf[CHUNK - 1:CHUNK, :]
    xm = prf + (prev - prf) * mix_ref[...]

    r = xm[:, :D_HEADS]
    k = xm[:, D_HEADS:2 * D_HEADS]
    v = xm[:, 2 * D_HEADS:3 * D_HEADS]
    o = 3 * D_HEADS
    xw = xm[:, o:o + LORA_W]
    xa = xm[:, o + LORA_W:o + LORA_W + LORA_A]
    xg = xm[:, o + LORA_W + LORA_A:]

    seg = seg_ref[...]
    wlog = -_softplus(-(w0_ref[...] + _dot(jnp.tanh(xw), w2_ref[...], HIGHEST))) - 0.5
    lw = jnp.where(valid, -jnp.exp(wlog), 0.0)
    a = _sigmoid(a0_ref[...] + _dot(xa, a2_ref[...], HIGHEST))
    g = _dot(_sigmoid(xg), g2_ref[...], HIGHEST)

    kk = k * kk_ref[...]
    kk_norm = jnp.sqrt(_dot_exact_rhs(kk * kk, seg))
    kk = jnp.where(valid, kk / jnp.maximum(kk_norm, 1e-12), 0.0)
    k = jnp.where(valid, k * (1.0 + (a - 1.0) * ka_ref[...]), 0.0)
    v = jnp.where(valid, v, 0.0)

    cl = _dot_exact_lhs(tri_ref[...], lw)
    dec_in = jnp.exp(cl)
    dec_out = jnp.exp(-cl)
    a_t = -kk * jnp.exp(cl - lw)
    b_t = kk * a * dec_out
    k_t = k * dec_out
    r_t = r * dec_in
    gamma = jnp.exp(cl[CHUNK - 1:CHUNK, :])

    ti = lax.broadcasted_iota(I32, (CHUNK, 2 * CHUNK), 0)
    si = lax.broadcasted_iota(I32, (CHUNK, 2 * CHUNK), 1)
    si = jnp.where(si >= CHUNK, si - CHUNK, si)
    strict = si < ti
    incl = si <= ti
    eye = (lax.broadcasted_iota(I32, (CHUNK, CHUNK), 0) == lax.broadcasted_iota(I32, (CHUNK, CHUNK), 1)).astype(F32)
    zeros = jnp.zeros((CHUNK, HEAD), F32)

    heads = range(N_HEADS)
    sls = [slice(h * HEAD, (h + 1) * HEAD) for h in heads]
    vh = [v[:, sl] for sl in sls]
    ar = [_split2(jnp.concatenate([a_t[:, sl], r_t[:, sl]], axis=0)) for sl in sls]
    bk = [_split2(jnp.concatenate([b_t[:, sl], k_t[:, sl]], axis=0)) for sl in sls]
    s0 = [state_sc[h] for h in heads]
    gram = [_mm3(_dot_nt, ar[h], bk[h]) for h in heads]
    top = [jnp.where(strict, gram[h][:CHUNK], 0.0) for h in heads]
    bot = [jnp.where(incl, gram[h][CHUNK:], 0.0) for h in heads]
    hs = [_mm3(_dot_nt, ar[h], _split2(s0[h])) for h in heads]
    rhs = [hs[h][:CHUNK] + _mm3(_dot, _split2(top[h]), _split2(jnp.concatenate([zeros, vh[h]], axis=0)))
           for h in heads]
    inv = [eye + top[h][:, :CHUNK] for h in heads]
    pws = [_split2(top[h][:, :CHUNK]) for h in heads]
    for _ in range(5):
        pws = [_split2(_mm3(_dot, pws[h], pws[h])) for h in heads]
        inv = [inv[h] + _mm3(_dot, _split2(inv[h]), pws[h]) for h in heads]
    u = [_mm3(_dot, _split2(inv[h]), _split2(rhs[h])) for h in heads]
    uv = [_split2(jnp.concatenate([u[h], vh[h]], axis=0)) for h in heads]
    ys = [hs[h][CHUNK:] + _dot(bot[h].astype(BF16), uv[h][0]) for h in heads]
    for h in heads:
        state_sc[h] = (s0[h] + _mm3(_dot_tn, uv[h], bk[h])) * gamma[:, sls[h]]

    y = jnp.concatenate(ys, axis=1)
    yc = y - _dot_exact_rhs(y, seg) * (1.0 / HEAD)
    var = _dot_exact_rhs(yc * yc, seg) * (1.0 / HEAD)
    y = yc * lax.rsqrt(var + LNX_EPS) * lng_ref[...] + lnb_ref[...]
    r = xm[:, :D_HEADS]
    bonus = _dot_exact_rhs(r * k * rk_ref[...], seg) * v
    y_ref[0] = (y + bonus) * g

    @pl.when(c == n_chunks - 1)
    def _():
        sout_ref[0] = state_sc[...]


def _rwkv(pr, shift0, wkv0, p):
    bsz, seq_len, _ = pr.shape
    n_chunks = pl.cdiv(seq_len, CHUNK)
    lane = jnp.arange(D_HEADS) // HEAD
    seg = (lane[:, None] == lane[None, :]).astype(BF16)
    tri = (jnp.arange(CHUNK)[None, :] <= jnp.arange(CHUNK)[:, None]).astype(BF16)
    vec = lambda a: a.reshape(1, -1)
    consts = [vec(p["rwkv_mix"]), vec(p["rwkv_w0"]), p["rwkv_w2"], vec(p["rwkv_a0"]), p["rwkv_a2"], p["rwkv_g2"],
              vec(p["rwkv_k_k"]), vec(p["rwkv_k_a"]), vec(p["rwkv_r_k"]), vec(p["rwkv_lnx_g"]),
              vec(p["rwkv_lnx_b"]), seg, tri]
    return pl.pallas_call(
        functools.partial(_rwkv_body, seq_len, n_chunks),
        grid=(bsz, n_chunks),
        in_specs=[pl.BlockSpec((1, CHUNK, RWKV_IN), lambda b, c: (b, c, 0)),
                  pl.BlockSpec((1, 1, RWKV_IN), lambda b, c: (b, 0, 0)),
                  pl.BlockSpec((1, N_HEADS, HEAD, HEAD), lambda b, c: (b, 0, 0, 0))]
                 + [_full(a.shape) for a in consts],
        out_specs=[pl.BlockSpec((1, CHUNK, D_HEADS), lambda b, c: (b, c, 0)),
                   pl.BlockSpec((1, N_HEADS, HEAD, HEAD), lambda b, c: (b, 0, 0, 0))],
        out_shape=[jax.ShapeDtypeStruct((bsz, seq_len, D_HEADS), F32),
                   jax.ShapeDtypeStruct((bsz, N_HEADS, HEAD, HEAD), F32)],
        scratch_shapes=[pltpu.VMEM((1, RWKV_IN), F32), pltpu.VMEM((N_HEADS, HEAD, HEAD), F32)],
        compiler_params=_params(("parallel", "arbitrary")),
        name="rwkv",
    )(pr, shift0, wkv0, *consts)


def _cumsum_body(seq_len, lf_ref, col_ref, row_ref):
    carry = jnp.zeros((1, N_HEADS), F32)
    eye = (lax.broadcasted_iota(I32, (N_HEADS, N_HEADS), 0)
           == lax.broadcasted_iota(I32, (N_HEADS, N_HEADS), 1)).astype(BF16)
    for start in range(0, seq_len, LANES):
        n = min(LANES, seq_len - start)
        tri = (lax.broadcasted_iota(I32, (n, n), 1) <= lax.broadcasted_iota(I32, (n, n), 0)).astype(BF16)
        cum = _dot_exact_lhs(tri, lf_ref[0, start:start + n, :]) + carry
        col_ref[0, start:start + n, :] = cum
        c1, c2, c3 = _split3(cum)
        row_ref[0, :, start:start + n] = _dot_nt(eye, c1) + _dot_nt(eye, c2) + _dot_nt(eye, c3)
        carry = cum[n - 1:n, :]


def _cumsum(lf):
    bsz, seq_len, _ = lf.shape
    return pl.pallas_call(
        functools.partial(_cumsum_body, seq_len),
        grid=(bsz,),
        in_specs=[pl.BlockSpec((1, seq_len, N_HEADS), lambda b: (b, 0, 0))],
        out_specs=[pl.BlockSpec((1, seq_len, N_HEADS), lambda b: (b, 0, 0)),
                   pl.BlockSpec((1, N_HEADS, seq_len), lambda b: (b, 0, 0))],
        out_shape=[jax.ShapeDtypeStruct((bsz, seq_len, N_HEADS), F32),
                   jax.ShapeDtypeStruct((bsz, N_HEADS, seq_len), F32)],
        compiler_params=_params(("parallel",)),
        name="cumsum",
    )(lf)


def _fox_body(kv_len, q_off, tq, tk, n_kv, q_ref, k_ref, v_ref, cq_ref, ck_ref, g_ref, o_ref, m_sc, l_sc, acc_sc):
    qi = pl.program_id(1)
    ki = pl.program_id(2)

    @pl.when(ki == 0)
    def _():
        m_sc[...] = jnp.full(m_sc.shape, -jnp.inf, F32)
        l_sc[...] = jnp.zeros(l_sc.shape, F32)
        acc_sc[...] = jnp.zeros(acc_sc.shape, F32)

    q_start = q_off + qi * tq
    k_start = ki * tk

    @pl.when(k_start <= q_start + tq - 1)
    def _():
        rows = q_start + lax.broadcasted_iota(I32, (tq, 1), 0)
        cols = k_start + lax.broadcasted_iota(I32, (1, tk), 1)
        mask = cols <= rows
        k_valid = (k_start + lax.broadcasted_iota(I32, (tk, 1), 0)) < kv_len
        q = q_ref[0] * (HEAD ** -0.5)
        kb = k_ref[0].astype(BF16)
        vb = jnp.where(k_valid, v_ref[0], 0.0).astype(BF16)
        cq = cq_ref[0]
        ck = ck_ref[0]
        ones = jnp.ones((tk, LANES), BF16)
        heads = range(N_HEADS)
        sls = [slice(h * HEAD, (h + 1) * HEAD) for h in heads]
        qb = q.astype(BF16)
        s = [_dot_nt(qb[:, sl], kb[:, sl]) for sl in sls]
        s = [jnp.where(mask, s[h] + (cq[:, h:h + 1] - ck[h:h + 1, :]), -jnp.inf) for h in heads]
        m_prev = [m_sc[h] for h in heads]
        m_new = [jnp.maximum(m_prev[h], jnp.max(s[h], axis=-1, keepdims=True)) for h in heads]
        scale = [jnp.exp(m_prev[h] - m_new[h]) for h in heads]
        pexp = [jnp.exp(s[h] - m_new[h]).astype(BF16) for h in heads]
        psum = [_dot(pexp[h], ones)[:, :1] for h in heads]
        pv = [_dot(pexp[h], vb[:, sls[h]]) for h in heads]
        for h in heads:
            l_sc[h] = scale[h] * l_sc[h] + psum[h]
            acc_sc[:, sls[h]] = scale[h] * acc_sc[:, sls[h]] + pv[h]
            m_sc[h] = m_new[h]

    @pl.when(ki == n_kv - 1)
    def _():
        for h in range(N_HEADS):
            sl = slice(h * HEAD, (h + 1) * HEAD)
            o = acc_sc[:, sl] / l_sc[h]
            o = o * lax.rsqrt(jnp.mean(o * o, axis=-1, keepdims=True) + 1e-6)
            acc_sc[:, sl] = o
        o_ref[0] = acc_sc[...] * g_ref[...]


def _fox(q, k_all, v_all, c_col, c_row, norm_g, q_off, tq, tk):
    bsz, q_len, _ = q.shape
    kv_len = k_all.shape[1]
    n_q = pl.cdiv(q_len, tq)
    n_kv = pl.cdiv(kv_len, tk)
    assert q_off % tq == 0
    last_kv = lambda qi: (q_off + (qi + 1) * tq - 1) // tk
    kv_map = lambda b, qi, ki: (b, jnp.minimum(ki, last_kv(qi)), 0)
    return pl.pallas_call(
        functools.partial(_fox_body, kv_len, q_off, tq, tk, n_kv),
        grid=(bsz, n_q, n_kv),
        in_specs=[pl.BlockSpec((1, tq, D_HEADS), lambda b, qi, ki: (b, qi, 0)),
                  pl.BlockSpec((1, tk, D_HEADS), kv_map),
                  pl.BlockSpec((1, tk, D_HEADS), kv_map),
                  pl.BlockSpec((1, tq, N_HEADS), lambda b, qi, ki: (b, q_off // tq + qi, 0)),
                  pl.BlockSpec((1, N_HEADS, tk), lambda b, qi, ki: (b, 0, jnp.minimum(ki, last_kv(qi)))),
                  _full((1, D_HEADS))],
        out_specs=pl.BlockSpec((1, tq, D_HEADS), lambda b, qi, ki: (b, qi, 0)),
        out_shape=jax.ShapeDtypeStruct((bsz, q_len, D_HEADS), F32),
        scratch_shapes=[pltpu.VMEM((N_HEADS, tq, 1), F32), pltpu.VMEM((N_HEADS, tq, 1), F32),
                        pltpu.VMEM((tq, D_HEADS), F32)],
        compiler_params=_params(("parallel", "parallel", "arbitrary")),
        name="fox",
    )(q, k_all, v_all, c_col, c_row, norm_g)


def _out_proj_body(yr_ref, yf_ref, h_ref, wr_ref, wf_ref, g_ref, b_ref, o_ref):
    mix = _dot(yr_ref[...].astype(BF16), wr_ref[...]) + _dot(yf_ref[...].astype(BF16), wf_ref[...])
    o_ref[...] = _layernorm(ALPHA * h_ref[...] + mix, g_ref[...], b_ref[...])


def _out_proj(yr, yf, h, wr, wf, g, b):
    n = h.shape[0]
    tb = TOKEN_BLOCK
    rows = lambda w: pl.BlockSpec((tb, w), lambda i: (i, 0))
    return pl.pallas_call(
        _out_proj_body,
        grid=(pl.cdiv(n, tb),),
        in_specs=[rows(D_HEADS), rows(D_HEADS), rows(D_MODEL), _full(wr.shape), _full(wf.shape),
                  _full((1, D_MODEL)), _full((1, D_MODEL))],
        out_specs=rows(D_MODEL),
        out_shape=jax.ShapeDtypeStruct((n, D_MODEL), F32),
        compiler_params=_params(("parallel",)),
        name="out_proj",
    )(yr, yf, h, wr, wf, g, b)


_BIG = 2 ** 30
_CAND_COUNTS = tuple(PEER_TOPK // (i + 1) for i in range(PEER_TOPK))
_N_CAND = sum(_CAND_COUNTS)
_CAND_ROWS = -(-_N_CAND // 8) * 8


def _top_rows(s, order, k, payload=None):
    vals, outs = [], []
    for _ in range(k):
        m = jnp.max(s, axis=0, keepdims=True)
        first = jnp.min(jnp.where(s == m, order, _BIG), axis=0, keepdims=True)
        sel = order == first
        vals.append(m)
        outs.append(first if payload is None else jnp.max(jnp.where(sel, payload, -1), axis=0, keepdims=True))
        s = jnp.where(sel, -jnp.inf, s)
    return jnp.concatenate(vals, axis=0), jnp.concatenate(outs, axis=0)


def _topk_body(h_ref, wq_ref, k1_ref, k2_ref, pos_ref, idx_ref, gate_ref, cand_s_sc, cand_i_sc):
    tb = h_ref.shape[0]
    q = _dot(h_ref[...].astype(BF16), wq_ref[...]).astype(BF16)
    key_id = lax.broadcasted_iota(I32, (N_KEYS, LANES), 0)
    cand_pos = pos_ref[...]
    k1 = k1_ref[...].astype(BF16)
    k2 = k2_ref[...].astype(BF16)
    cand_s_sc[_N_CAND:, :] = jnp.full((_CAND_ROWS - _N_CAND, LANES), -jnp.inf, F32)
    cand_i_sc[_N_CAND:, :] = jnp.zeros((_CAND_ROWS - _N_CAND, LANES), I32)
    for part in range(tb // LANES):
        qp = q[part * LANES:(part + 1) * LANES]
        ids, gates = [], []
        for h in range(PEER_HEADS):
            o = h * 2 * D_KEY_HALF
            s1 = _dot_nt(k1, qp[:, o:o + D_KEY_HALF])
            s2 = _dot_nt(k2, qp[:, o + D_KEY_HALF:o + 2 * D_KEY_HALF])
            v1, i1 = _top_rows(s1, key_id, PEER_TOPK)
            v2, i2 = _top_rows(s2, key_id, PEER_TOPK)
            off = 0
            for i, cnt in enumerate(_CAND_COUNTS):
                cand_s_sc[off:off + cnt, :] = v1[i:i + 1] + v2[:cnt]
                cand_i_sc[off:off + cnt, :] = i1[i:i + 1] * N_KEYS + i2[:cnt]
                off += cnt
            best_s, best_i = _top_rows(cand_s_sc[...], cand_pos, PEER_TOPK, payload=cand_i_sc[...])
            e = jnp.exp(best_s - best_s[0:1])
            ids.append(best_i)
            gates.append(e / jnp.sum(e, axis=0, keepdims=True))
        rows = slice(part * LANES, (part + 1) * LANES)
        idx_ref[rows, :] = jnp.concatenate(ids, axis=0).T
        gate_ref[rows, :] = jnp.concatenate(gates, axis=0).T


def _topk(h1, wq, keys1, keys2):
    n = h1.shape[0]
    tb = TOKEN_BLOCK
    pos = [i * PEER_TOPK + j for i, cnt in enumerate(_CAND_COUNTS) for j in range(cnt)]
    pos = jnp.asarray(pos + [_BIG] * (_CAND_ROWS - _N_CAND), I32)
    pos = jnp.broadcast_to(pos[:, None], (_CAND_ROWS, LANES))
    return pl.pallas_call(
        _topk_body,
        grid=(pl.cdiv(n, tb),),
        in_specs=[pl.BlockSpec((tb, D_MODEL), lambda i: (i, 0)), _full(wq.shape), _full(keys1.shape),
                  _full(keys2.shape), _full(pos.shape)],
        out_specs=[pl.BlockSpec((tb, PEER_SLOTS), lambda i: (i, 0)), pl.BlockSpec((tb, PEER_SLOTS), lambda i: (i, 0))],
        out_shape=[jax.ShapeDtypeStruct((n, PEER_SLOTS), I32), jax.ShapeDtypeStruct((n, PEER_SLOTS), F32)],
        scratch_shapes=[pltpu.VMEM((_CAND_ROWS, LANES), F32), pltpu.VMEM((_CAND_ROWS, LANES), I32)],
        compiler_params=_params(("parallel",)),
        name="peer_topk",
    )(h1, wq, keys1, keys2, pos)


def _unpack_pair(w):
    return lax.bitcast_convert_type(w << 16, F32), lax.bitcast_convert_type(w & jnp.int32(-65536), F32)


_ERF_NUM = (-2.72614225801306e-10, 2.77068142495902e-08, -2.10102402082508e-06, -5.69250639462346e-05,
            -7.34990630326855e-04, -2.95459980854025e-03, -1.60960333262415e-02)
_ERF_DEN = (-1.45660718464996e-05, -2.13374055278905e-04, -1.68282697438203e-03, -7.37332916720468e-03,
            -1.42647390514189e-02)


def _erf_f32(x):
    x = jnp.minimum(jnp.maximum(x, -4.0), 4.0)
    x2 = x * x
    num = x2 * _ERF_NUM[0] + _ERF_NUM[1]
    for c in _ERF_NUM[2:]:
        num = num * x2 + c
    den = x2 * _ERF_DEN[0] + _ERF_DEN[1]
    for c in _ERF_DEN[2:]:
        den = den * x2 + c
    return x * num / den


def _expert_mix(u_tab, v_tab, idx, x, gate):
    n_tok, words = x.shape[0], u_tab.shape[1]
    lanes = SC_LANES
    pass_chunks = words // lanes // 2
    n_workers = SC_CORES * SC_SUBCORES
    assert n_tok % SC_UNIT == 0
    base, extra = divmod(n_tok // SC_UNIT, n_workers)
    idx_rows = 2 * SC_UNIT
    mesh = plsc.VectorSubcoreMesh(core_axis_name="c", subcore_axis_name="s",
                                  num_cores=SC_CORES, num_subcores=SC_SUBCORES)

    @functools.partial(
        pl.kernel, mesh=mesh, compiler_params=pltpu.CompilerParams(needs_layout_passes=False),
        out_type=jax.ShapeDtypeStruct((n_tok, 2 * words), F32),
        scratch_types=[pltpu.VMEM((2 * idx_rows, SC_HALF), I32),
                       pltpu.VMEM((SC_HALF, words), I32), pltpu.VMEM((SC_HALF, words), I32),
                       pltpu.VMEM((2 * SC_UNIT, 2 * words), F32),
                       pltpu.VMEM((2 * SC_UNIT, PEER_SLOTS), F32),
                       pltpu.VMEM((SC_UNIT, PEER_SLOTS), F32),
                       pltpu.VMEM((2 * SC_UNIT, 2 * words), F32),
                       pltpu.SemaphoreType.DMA, pltpu.SemaphoreType.DMA, pltpu.SemaphoreType.DMA,
                       pltpu.SemaphoreType.DMA((2,))],
    )
    def sc_kernel(u_hbm, v_hbm, idx_hbm, x_hbm, gate_hbm, out_hbm, idx_v, buf_a, buf_b, x_v, gate_v, act_v, out_v,
                  sem_a, sem_b, sem_in, sem_out):
        worker = lax.axis_index("s") * SC_CORES + lax.axis_index("c")
        first_unit = worker * base + jnp.minimum(worker, extra)
        my_units = base + (worker < extra).astype(I32)
        lane = lax.iota(I32, lanes)

        def first_token(ui):
            return pl.multiple_of((first_unit + ui) * SC_UNIT, SC_UNIT)

        def stage_in(ui, slot):
            tok0 = first_token(ui)
            toks = pl.ds(tok0, SC_UNIT)
            dst = pl.ds(slot * SC_UNIT, SC_UNIT)
            rows = pl.ds(pl.multiple_of(tok0 * 2, idx_rows), idx_rows)
            return (pltpu.make_async_copy(idx_hbm.at[rows], idx_v.at[pl.ds(slot * idx_rows, idx_rows)], sem_in),
                    pltpu.make_async_copy(x_hbm.at[toks], x_v.at[dst], sem_in),
                    pltpu.make_async_copy(gate_hbm.at[toks], gate_v.at[dst], sem_in))

        def write_out(ui, slot):
            return pltpu.make_async_copy(out_v.at[pl.ds(slot * SC_UNIT, SC_UNIT)],
                                         out_hbm.at[pl.ds(first_token(ui), SC_UNIT)], sem_out.at[slot])

        def gather(table_hbm, row, buf, sem):
            return pltpu.make_async_copy(table_hbm.at[idx_v.at[row]], buf, sem)

        def dot(buf, t, xrow, half):
            @pl.loop(0, SC_HALF // lanes)
            def _(g):
                def chunk(j, acc):
                    x_lo = x_v[xrow, pl.ds(j * lanes, lanes)]
                    x_hi = x_v[xrow, pl.ds(words + j * lanes, lanes)]
                    out = []
                    for r in range(lanes):
                        lo, hi = _unpack_pair(buf[g * lanes + r, pl.ds(j * lanes, lanes)])
                        out.append(acc[r] + lo * x_lo + hi * x_hi)
                    return tuple(out)

                acc = lax.fori_loop(0, words // lanes, chunk, tuple(jnp.zeros((lanes,), F32) for _ in range(lanes)))
                res = jnp.zeros((lanes,), F32)
                for r in range(lanes):
                    res = jnp.where(lane == r, jnp.sum(acc[r]), res)
                act_v[t, pl.ds(half * SC_HALF + g * lanes, lanes)] = res

        def gate_gelu(t, xrow):
            for g in range(PEER_SLOTS // lanes):
                cols = pl.ds(g * lanes, lanes)
                a = act_v[t, cols]
                act_v[t, cols] = gate_v[xrow, cols] * (a * 0.5 * (1.0 + _erf_f32(a * (2.0 ** -0.5))))

        def vsum(buf, t, orow, half):
            for p in range(2):
                cols = [(p * pass_chunks + jj) * lanes for jj in range(pass_chunks)]
                if half == 0:
                    init = tuple(jnp.zeros((lanes,), F32) for _ in range(2 * pass_chunks))
                else:
                    init = (tuple(out_v[orow, pl.ds(c, lanes)] for c in cols)
                            + tuple(out_v[orow, pl.ds(words + c, lanes)] for c in cols))

                def row(r, acc):
                    slot = jnp.full((lanes,), half * SC_HALF, I32) + r
                    c = plsc.load_gather(act_v, [jnp.full((lanes,), 0, I32) + t, slot])
                    lo_acc, hi_acc = list(acc[:pass_chunks]), list(acc[pass_chunks:])
                    for jj in range(pass_chunks):
                        lo, hi = _unpack_pair(buf[r, pl.ds(cols[jj], lanes)])
                        lo_acc[jj] = lo_acc[jj] + lo * c
                        hi_acc[jj] = hi_acc[jj] + hi * c
                    return tuple(lo_acc) + tuple(hi_acc)

                acc = lax.fori_loop(0, SC_HALF, row, init)
                for jj in range(pass_chunks):
                    out_v[orow, pl.ds(cols[jj], lanes)] = acc[jj]
                    out_v[orow, pl.ds(words + cols[jj], lanes)] = acc[pass_chunks + jj]

        def unit_body(ui, carry):
            slot = lax.rem(ui, 2)
            has_next = ui + 1 < my_units

            @pl.when(has_next)
            def _():
                for c in stage_in(ui + 1, 1 - slot):
                    c.start()

            @pl.when(ui >= 2)
            def _():
                write_out(ui - 2, slot).wait()

            @pl.loop(0, SC_UNIT)
            def _(t):
                r0 = slot * idx_rows + 2 * t
                urow = slot * SC_UNIT + t
                gather(u_hbm, r0 + 1, buf_b, sem_b).start()
                gather(u_hbm, r0, buf_a, sem_a).wait()
                dot(buf_a, t, urow, 0)
                gather(v_hbm, r0, buf_a, sem_a).start()
                gather(u_hbm, r0 + 1, buf_b, sem_b).wait()
                dot(buf_b, t, urow, 1)
                gate_gelu(t, urow)
                gather(v_hbm, r0 + 1, buf_b, sem_b).start()
                gather(v_hbm, r0, buf_a, sem_a).wait()
                vsum(buf_a, t, urow, 0)

                @pl.when(t + 1 < SC_UNIT)
                def _():
                    gather(u_hbm, r0 + 2, buf_a, sem_a).start()

                @pl.when((t + 1 == SC_UNIT) & has_next)
                def _():
                    for c in stage_in(ui + 1, 1 - slot):
                        c.wait()
                    gather(u_hbm, (1 - slot) * idx_rows, buf_a, sem_a).start()

                gather(v_hbm, r0 + 1, buf_b, sem_b).wait()
                vsum(buf_b, t, urow, 1)

            write_out(ui, slot).start()
            return carry

        @pl.when(my_units > 0)
        def _():
            for c in stage_in(0, 0):
                c.start()
            for c in stage_in(0, 0):
                c.wait()
            gather(u_hbm, 0, buf_a, sem_a).start()
            lax.fori_loop(0, my_units, unit_body, 0)

            @pl.when(my_units >= 2)
            def _():
                write_out(my_units - 2, lax.rem(my_units, 2)).wait()

            write_out(my_units - 1, lax.rem(my_units - 1, 2)).wait()

    return sc_kernel(u_tab, v_tab, idx.reshape(n_tok * 2, SC_HALF), x, gate)


def _final_body(x_ref, mix_ref, g_ref, b_ref, o_ref):
    o_ref[...] = _layernorm(ALPHA * x_ref[...] + mix_ref[...], g_ref[...], b_ref[...])


def _rowwise(body, name, n, widths_in, width_out, *args):
    tb = TOKEN_BLOCK
    rows = lambda w: pl.BlockSpec((tb, w), lambda i: (i, 0))
    n_rows_args = len(widths_in)
    return pl.pallas_call(
        body,
        grid=(pl.cdiv(n, tb),),
        in_specs=[rows(w) for w in widths_in] + [_full(a.shape) for a in args[n_rows_args:]],
        out_specs=rows(width_out),
        out_shape=jax.ShapeDtypeStruct((n, width_out), F32),
        compiler_params=_params(("parallel",)),
        name=name,
    )(*args)


def _group(x, shift0, wkv0, past, p):
    bsz, seq_len, _ = x.shape
    n = bsz * seq_len
    h, pr, q, k, v, lf = _in_proj(x.reshape(n, D_MODEL), p["ln_in_g"], p["ln_in_b"], p["w_r"], p["w_qkv"],
                                  p["w_f"], p["b_f"])
    pr = pr.reshape(bsz, seq_len, RWKV_IN)
    y_r, wkv_new = _rwkv(pr, shift0, wkv0, p)

    seq3 = lambda a: a.reshape(bsz, seq_len, -1)
    k3, v3, lf3 = seq3(k), seq3(v), seq3(lf)
    if past is None:
        k_all, v_all, lf_all, q_off, tq = k3, v3, lf3, 0, FOX_BLOCK
    else:
        k_past, v_past, lf_past = past
        q_off = k_past.shape[1]
        k_all = jnp.concatenate([k_past.reshape(bsz, q_off, D_HEADS), k3], axis=1)
        v_all = jnp.concatenate([v_past.reshape(bsz, q_off, D_HEADS), v3], axis=1)
        lf_all = jnp.concatenate([lf_past, lf3], axis=1)
        tq = seq_len
    c_col, c_row = _cumsum(lf_all)
    y_f = _fox(seq3(q), k_all, v_all, c_col, c_row, p["fox_norm_g"], q_off, tq, FOX_KV_BLOCK)

    h1 = _out_proj(y_r.reshape(n, D_HEADS), y_f.reshape(n, D_HEADS), h, p["w_out_r"], p["w_out_f"],
                   p["ln1_g"], p["ln1_b"])
    idx, gate = _topk(h1, p["peer_w_q"], p["peer_keys1"], p["peer_keys2"])
    mix = _expert_mix(p["peer_u"], p["peer_v"], idx, h1, gate)
    y = _rowwise(_final_body, "final_ln", n, (D_MODEL, D_MODEL), D_MODEL, h1, mix, p["ln2_g"], p["ln2_b"])

    heads = lambda a: a.reshape(1, bsz, seq_len, N_HEADS, HEAD)
    outs = (y.reshape(bsz, seq_len, D_MODEL), pr[None, :, seq_len - 1:, :], wkv_new[None],
            heads(k), heads(v), lf3[None])
    return outs, idx


def _pack_halves(a):
    bits = lax.bitcast_convert_type(a.astype(BF16), jnp.uint16).astype(jnp.uint32)
    w = a.shape[1] // 2
    return lax.bitcast_convert_type(bits[:, :w] | (bits[:, w:] << 16), I32)


def kernel(x_prompt, x_sample, state_rwkv_shift, state_rwkv_wkv, cache_fox_k, cache_fox_v, cache_fox_logf,
           meta_tokens, ln_in_g, ln_in_b, w_in, rwkv_mix, rwkv_w0, rwkv_w2, rwkv_a0, rwkv_a2, rwkv_g2,
           rwkv_k_k, rwkv_k_a, rwkv_r_k, rwkv_lnx_g, rwkv_lnx_b, fox_b_f, fox_norm_g, w_out, ln1_g, ln1_b,
           peer_w_q, peer_keys1, peer_keys2, peer_u, peer_v, ln2_g, ln2_b):
    assert w_in.shape[0] == 1, "single-layer step"
    bsz = x_prompt.shape[0]
    row = lambda a: a.reshape(1, -1)
    w = w_in[0]
    fox0 = RWKV_IN
    pad_f = LANES - N_HEADS
    p = dict(
        ln_in_g=row(ln_in_g), ln_in_b=row(ln_in_b),
        w_r=w[:, :RWKV_IN].astype(BF16),
        w_qkv=w[:, fox0:fox0 + 3 * D_HEADS].astype(BF16),
        w_f=jnp.pad(w[:, fox0 + 3 * D_HEADS:], ((0, 0), (0, pad_f))).astype(BF16),
        b_f=jnp.pad(row(fox_b_f[0]), ((0, 0), (0, pad_f))),
        rwkv_mix=rwkv_mix[0], rwkv_w0=rwkv_w0[0], rwkv_w2=rwkv_w2[0], rwkv_a0=rwkv_a0[0], rwkv_a2=rwkv_a2[0],
        rwkv_g2=rwkv_g2[0], rwkv_k_k=rwkv_k_k[0], rwkv_k_a=rwkv_k_a[0], rwkv_r_k=rwkv_r_k[0],
        rwkv_lnx_g=rwkv_lnx_g[0], rwkv_lnx_b=rwkv_lnx_b[0],
        fox_norm_g=row(fox_norm_g[0]),
        w_out_r=w_out[0, :D_HEADS].astype(BF16), w_out_f=w_out[0, D_HEADS:].astype(BF16),
        ln1_g=row(ln1_g[0]), ln1_b=row(ln1_b[0]),
        peer_w_q=peer_w_q[0].astype(BF16), peer_keys1=peer_keys1[0], peer_keys2=peer_keys2[0],
        peer_u=_pack_halves(peer_u[0]), peer_v=_pack_halves(peer_v[0]),
        ln2_g=row(ln2_g[0]), ln2_b=row(ln2_b[0]),
    )
    sizes = PROMPT_SLICES if sum(PROMPT_SLICES) == bsz else (bsz,)
    parts = []
    start = 0
    first_idx = None
    for pb in sizes:
        x_slice = x_prompt[start:start + pb]
        if first_idx is not None:
            x_slice, first_idx = lax.optimization_barrier((x_slice, first_idx))
        meta = jnp.broadcast_to(meta_tokens[None], (pb, N_META, D_MODEL))
        hp0 = jnp.concatenate([meta, x_slice], axis=1)
        (yp, *state), idx = _group(hp0, jnp.zeros((pb, 1, RWKV_IN), F32), jnp.zeros((pb, N_HEADS, HEAD, HEAD), F32),
                                   None, p)
        first_idx = idx if first_idx is None else first_idx
        parts.append((yp[:, N_META:], *state))
        start += pb
    sample_in = (x_sample, cache_fox_k[0], cache_fox_v[0], cache_fox_logf[0])
    if len(sizes) > 1:
        sample_in, _ = lax.optimization_barrier((sample_in, first_idx))
    (ys, s_shift, s_wkv, s_k, s_v, s_lf), _ = _group(
        sample_in[0], state_rwkv_shift[0], state_rwkv_wkv[0], sample_in[1:], p)
    yp = jnp.concatenate([t[0] for t in parts], axis=0)
    p_shift, p_wkv, p_k, p_v, p_lf = (jnp.concatenate([t[j] for t in parts], axis=1) for j in range(1, 6))
    return (yp, ys, p_shift, p_wkv, p_k, p_v, p_lf, s_shift, s_wkv, s_k, s_v, s_lf)
```

```python
import functools

import jax
import jax.numpy as jnp
from jax import lax
from jax.experimental import pallas as pl
from jax.experimental.pallas import tpu as pltpu
from jax.experimental.pallas import tpu_sc as plsc

F32 = jnp.float32
BF16 = jnp.bfloat16
I32 = jnp.int32

D_MODEL = 1024
N_HEADS = 8
HEAD = 64
D_HEADS = N_HEADS * HEAD
LORA_W, LORA_A, LORA_G = 64, 64, 128
RWKV_IN = 3 * D_HEADS + LORA_W + LORA_A + LORA_G
N_META = 16
LNX_EPS = 64e-5
LN_EPS = 1e-5
ALPHA = 2.0 ** 0.25
N_KEYS = 128
PEER_HEADS = 8
PEER_TOPK = 16
PEER_SLOTS = PEER_HEADS * PEER_TOPK
D_KEY_HALF = 128

LANES = 128
CHUNK = 64
TOKEN_BLOCK = 256
FOX_BLOCK = 256
FOX_KV_BLOCK = 512
SC_CORES, SC_SUBCORES, SC_LANES = 2, 16, 16
SC_HALF = PEER_SLOTS // 2
SC_UNIT = 8
PROMPT_SLICES = (1, 1, 2, 2, 2, 2, 2, 2, 2)
VMEM_LIMIT = 48 * 1024 * 1024

HIGHEST = lax.Precision.HIGHEST


def _params(semantics):
    return pltpu.CompilerParams(dimension_semantics=semantics, vmem_limit_bytes=VMEM_LIMIT)


def _dot(a, b, precision=None):
    return jnp.dot(a, b, preferred_element_type=F32, precision=precision)


def _dot_nt(a, b, precision=None):
    return lax.dot_general(a, b, (((1,), (1,)), ((), ())), preferred_element_type=F32, precision=precision)


def _dot_tn(a, b, precision=None):
    return lax.dot_general(a, b, (((0,), (0,)), ((), ())), preferred_element_type=F32, precision=precision)


def _split3(a):
    a1 = a.astype(BF16)
    r1 = a - a1.astype(F32)
    a2 = r1.astype(BF16)
    a3 = (r1 - a2.astype(F32)).astype(BF16)
    return a1, a2, a3


def _split2(a):
    hi = a.astype(BF16)
    return hi, (a - hi.astype(F32)).astype(BF16)


def _mm3(dot, a, b):
    return dot(a[0], b[0]) + (dot(a[0], b[1]) + dot(a[1], b[0]))


def _dot_exact_rhs(a, m_bf16):
    a1, a2, a3 = _split3(a)
    return _dot(a1, m_bf16) + _dot(a2, m_bf16) + _dot(a3, m_bf16)


def _dot_exact_lhs(m_bf16, a):
    a1, a2, a3 = _split3(a)
    return _dot(m_bf16, a1) + _dot(m_bf16, a2) + _dot(m_bf16, a3)


def _layernorm(x, g, b):
    xc = x - jnp.mean(x, axis=-1, keepdims=True)
    var = jnp.mean(xc * xc, axis=-1, keepdims=True)
    return xc * lax.rsqrt(var + LN_EPS) * g + b


def _softplus(x):
    return jnp.maximum(x, 0.0) + jnp.log1p(jnp.exp(-jnp.abs(x)))


def _sigmoid(x):
    return 1.0 / (1.0 + jnp.exp(-x))


def _full(shape):
    return pl.BlockSpec(shape, lambda *_: (0,) * len(shape))


def _in_proj_body(n_after, x_ref, g_ref, b_ref, wr_ref, wqkv_ref, wf_ref, bf_ref, *refs):
    h_ref, pr_ref, q_ref, k_ref, v_ref, lf_ref = refs[n_after:]
    h = _layernorm(x_ref[...], g_ref[...], b_ref[...])
    h_ref[...] = h
    hb = h.astype(BF16)
    pr_ref[...] = _dot(hb, wr_ref[...])
    qkv = _dot(hb, wqkv_ref[...])
    q_ref[...] = qkv[:, :D_HEADS]
    k_ref[...] = qkv[:, D_HEADS:2 * D_HEADS]
    v_ref[...] = qkv[:, 2 * D_HEADS:]
    fl = _dot(hb, wf_ref[...]) + bf_ref[...]
    lf = jnp.minimum(fl, 0.0) - jnp.log1p(jnp.exp(-jnp.abs(fl)))
    lf_ref[...] = lf[:, :N_HEADS]


def _in_proj(x, g, b, wr, wqkv, wf, bfp, after=()):
    n = x.shape[0]
    tb = TOKEN_BLOCK
    rows = lambda w: pl.BlockSpec((tb, w), lambda i: (i, 0))
    return pl.pallas_call(
        functools.partial(_in_proj_body, len(after)),
        grid=(pl.cdiv(n, tb),),
        in_specs=[rows(D_MODEL), _full((1, D_MODEL)), _full((1, D_MODEL)), _full(wr.shape),
                  _full(wqkv.shape), _full(wf.shape), _full(bfp.shape)]
                 + [pl.BlockSpec(memory_space=pl.ANY)] * len(after),
        out_specs=[rows(D_MODEL), rows(RWKV_IN), rows(D_HEADS), rows(D_HEADS), rows(D_HEADS), rows(N_HEADS)],
        out_shape=[jax.ShapeDtypeStruct((n, w), F32)
                   for w in (D_MODEL, RWKV_IN, D_HEADS, D_HEADS, D_HEADS, N_HEADS)],
        compiler_params=_params(("parallel",)),
        name="in_proj",
    )(x, g, b, wr, wqkv, wf, bfp, *after)


def _rwkv_body(seq_len, n_chunks, pr_ref, shift0_ref, s0_ref, mix_ref, w0_ref, w2_ref, a0_ref, a2_ref,
               g2_ref, kk_ref, ka_ref, rk_ref, lng_ref, lnb_ref, seg_ref, tri_ref,
               y_ref, sout_ref, shift_sc, state_sc):
    c = pl.program_id(1)

    @pl.when(c == 0)
    def _():
        shift_sc[...] = shift0_ref[0]
        state_sc[...] = s0_ref[0]

    row = lax.broadcasted_iota(I32, (CHUNK, 1), 0)
    valid = (c * CHUNK + row) < seq_len
    prf = jnp.where(valid, pr_ref[0], 0.0)
    prev = jnp.where(row == 0, shift_sc[...], pltpu.roll(prf, 1, 0))
    shift_sc[...] = prf[CHUNK - 1:CHUNK, :]
    xm = prf + (prev - prf) * mix_ref[...]

    r = xm[:, :D_HEADS]
    k = xm[:, D_HEADS:2 * D_HEADS]
    v = xm[:, 2 * D_HEADS:3 * D_HEADS]
    o = 3 * D_HEADS
    xw = xm[:, o:o + LORA_W]
    xa = xm[:, o + LORA_W:o + LORA_W + LORA_A]
    xg = xm[:, o + LORA_W + LORA_A:]

    seg = seg_ref[...]
    wlog = -_softplus(-(w0_ref[...] + _dot(jnp.tanh(xw), w2_ref[...], HIGHEST))) - 0.5
    lw = jnp.where(valid, -jnp.exp(wlog), 0.0)
    a = _sigmoid(a0_ref[...] + _dot(xa, a2_ref[...], HIGHEST))
    g = _dot(_sigmoid(xg), g2_ref[...], HIGHEST)

    kk = k * kk_ref[...]
    kk_norm = jnp.sqrt(_dot_exact_rhs(kk * kk, seg))
    kk = jnp.where(valid, kk / jnp.maximum(kk_norm, 1e-12), 0.0)
    k = jnp.where(valid, k * (1.0 + (a - 1.0) * ka_ref[...]), 0.0)
    v = jnp.where(valid, v, 0.0)

    cl = _dot_exact_lhs(tri_ref[...], lw)
    dec_in = jnp.exp(cl)
    dec_out = jnp.exp(-cl)
    a_t = -kk * jnp.exp(cl - lw)
    b_t = kk * a * dec_out
    k_t = k * dec_out
    r_t = r * dec_in
    gamma = jnp.exp(cl[CHUNK - 1:CHUNK, :])

    ti = lax.broadcasted_iota(I32, (CHUNK, 2 * CHUNK), 0)
    si = lax.broadcasted_iota(I32, (CHUNK, 2 * CHUNK), 1)
    si = jnp.where(si >= CHUNK, si - CHUNK, si)
    strict = si < ti
    incl = si <= ti
    eye = (lax.broadcasted_iota(I32, (CHUNK, CHUNK), 0) == lax.broadcasted_iota(I32, (CHUNK, CHUNK), 1)).astype(F32)
    zeros = jnp.zeros((CHUNK, HEAD), F32)

    heads = range(N_HEADS)
    sls = [slice(h * HEAD, (h + 1) * HEAD) for h in heads]
    vh = [v[:, sl] for sl in sls]
    ar = [_split2(jnp.concatenate([a_t[:, sl], r_t[:, sl]], axis=0)) for sl in sls]
    bk = [_split2(jnp.concatenate([b_t[:, sl], k_t[:, sl]], axis=0)) for sl in sls]
    s0 = [state_sc[h] for h in heads]
    gram = [_mm3(_dot_nt, ar[h], bk[h]) for h in heads]
    top = [jnp.where(strict, gram[h][:CHUNK], 0.0) for h in heads]
    bot = [jnp.where(incl, gram[h][CHUNK:], 0.0) for h in heads]
    hs = [_mm3(_dot_nt, ar[h], _split2(s0[h])) for h in heads]
    rhs = [hs[h][:CHUNK] + _mm3(_dot, _split2(top[h]), _split2(jnp.concatenate([zeros, vh[h]], axis=0)))
           for h in heads]
    inv = [eye + top[h][:, :CHUNK] for h in heads]
    pws = [_split2(top[h][:, :CHUNK]) for h in heads]
    for _ in range(5):
        pws = [_split2(_mm3(_dot, pws[h], pws[h])) for h in heads]
        inv = [inv[h] + _mm3(_dot, _split2(inv[h]), pws[h]) for h in heads]
    u = [_mm3(_dot, _split2(inv[h]), _split2(rhs[h])) for h in heads]
    uv = [_split2(jnp.concatenate([u[h], vh[h]], axis=0)) for h in heads]
    ys = [hs[h][CHUNK:] + _dot(bot[h].astype(BF16), uv[h][0]) for h in heads]
    for h in heads:
        state_sc[h] = (s0[h] + _mm3(_dot_tn, uv[h], bk[h])) * gamma[:, sls[h]]

    y = jnp.concatenate(ys, axis=1)
    yc = y - _dot_exact_rhs(y, seg) * (1.0 / HEAD)
    var = _dot_exact_rhs(yc * yc, seg) * (1.0 / HEAD)
    y = yc * lax.rsqrt(var + LNX_EPS) * lng_ref[...] + lnb_ref[...]
    r = xm[:, :D_HEADS]
    bonus = _dot_exact_rhs(r * k * rk_ref[...], seg) * v
    y_ref[0] = (y + bonus) * g

    @pl.when(c == n_chunks - 1)
    def _():
        sout_ref[0] = state_sc[...]


def _rwkv(pr, shift0, wkv0, p):
    bsz, seq_len, _ = pr.shape
    n_chunks = pl.cdiv(seq_len, CHUNK)
    lane = jnp.arange(D_HEADS) // HEAD
    seg = (lane[:, None] == lane[None, :]).astype(BF16)
    tri = (jnp.arange(CHUNK)[None, :] <= jnp.arange(CHUNK)[:, None]).astype(BF16)
    vec = lambda a: a.reshape(1, -1)
    consts = [vec(p["rwkv_mix"]), vec(p["rwkv_w0"]), p["rwkv_w2"], vec(p["rwkv_a0"]), p["rwkv_a2"], p["rwkv_g2"],
              vec(p["rwkv_k_k"]), vec(p["rwkv_k_a"]), vec(p["rwkv_r_k"]), vec(p["rwkv_lnx_g"]),
              vec(p["rwkv_lnx_b"]), seg, tri]
    return pl.pallas_call(
        functools.partial(_rwkv_body, seq_len, n_chunks),
        grid=(bsz, n_chunks),
        in_specs=[pl.BlockSpec((1, CHUNK, RWKV_IN), lambda b, c: (b, c, 0)),
                  pl.BlockSpec((1, 1, RWKV_IN), lambda b, c: (b, 0, 0)),
                  pl.BlockSpec((1, N_HEADS, HEAD, HEAD), lambda b, c: (b, 0, 0, 0))]
                 + [_full(a.shape) for a in consts],
        out_specs=[pl.BlockSpec((1, CHUNK, D_HEADS), lambda b, c: (b, c, 0)),
                   pl.BlockSpec((1, N_HEADS, HEAD, HEAD), lambda b, c: (b, 0, 0, 0))],
        out_shape=[jax.ShapeDtypeStruct((bsz, seq_len, D_HEADS), F32),
                   jax.ShapeDtypeStruct((bsz, N_HEADS, HEAD, HEAD), F32)],
        scratch_shapes=[pltpu.VMEM((1, RWKV_IN), F32), pltpu.VMEM((N_HEADS, HEAD, HEAD), F32)],
        compiler_params=_params(("parallel", "arbitrary")),
        name="rwkv",
    )(pr, shift0, wkv0, *consts)


def _cumsum_body(seq_len, lf_ref, col_ref, row_ref):
    carry = jnp.zeros((1, N_HEADS), F32)
    eye = (lax.broadcasted_iota(I32, (N_HEADS, N_HEADS), 0)
           == lax.broadcasted_iota(I32, (N_HEADS, N_HEADS), 1)).astype(BF16)
    for start in range(0, seq_len, LANES):
        n = min(LANES, seq_len - start)
        tri = (lax.broadcasted_iota(I32, (n, n), 1) <= lax.broadcasted_iota(I32, (n, n), 0)).astype(BF16)
        cum = _dot_exact_lhs(tri, lf_ref[0, start:start + n, :]) + carry
        col_ref[0, start:start + n, :] = cum
        c1, c2, c3 = _split3(cum)
        row_ref[0, :, start:start + n] = _dot_nt(eye, c1) + _dot_nt(eye, c2) + _dot_nt(eye, c3)
        carry = cum[n - 1:n, :]


def _cumsum(lf):
    bsz, seq_len, _ = lf.shape
    return pl.pallas_call(
        functools.partial(_cumsum_body, seq_len),
        grid=(bsz,),
        in_specs=[pl.BlockSpec((1, seq_len, N_HEADS), lambda b: (b, 0, 0))],
        out_specs=[pl.BlockSpec((1, seq_len, N_HEADS), lambda b: (b, 0, 0)),
                   pl.BlockSpec((1, N_HEADS, seq_len), lambda b: (b, 0, 0))],
        out_shape=[jax.ShapeDtypeStruct((bsz, seq_len, N_HEADS), F32),
                   jax.ShapeDtypeStruct((bsz, N_HEADS, seq_len), F32)],
        compiler_params=_params(("parallel",)),
        name="cumsum",
    )(lf)


def _fox_body(kv_len, q_off, tq, tk, n_kv, q_ref, k_ref, v_ref, cq_ref, ck_ref, g_ref, o_ref, m_sc, l_sc, acc_sc):
    qi = pl.program_id(1)
    ki = pl.program_id(2)

    @pl.when(ki == 0)
    def _():
        m_sc[...] = jnp.full(m_sc.shape, -jnp.inf, F32)
        l_sc[...] = jnp.zeros(l_sc.shape, F32)
        acc_sc[...] = jnp.zeros(acc_sc.shape, F32)

    q_start = q_off + qi * tq
    k_start = ki * tk

    @pl.when(k_start <= q_start + tq - 1)
    def _():
        rows = q_start + lax.broadcasted_iota(I32, (tq, 1), 0)
        cols = k_start + lax.broadcasted_iota(I32, (1, tk), 1)
        mask = cols <= rows
        k_valid = (k_start + lax.broadcasted_iota(I32, (tk, 1), 0)) < kv_len
        q = q_ref[0] * (HEAD ** -0.5)
        kb = k_ref[0].astype(BF16)
        vb = jnp.where(k_valid, v_ref[0], 0.0).astype(BF16)
        cq = cq_ref[0]
        ck = ck_ref[0]
        ones = jnp.ones((tk, LANES), BF16)
        heads = range(N_HEADS)
        sls = [slice(h * HEAD, (h + 1) * HEAD) for h in heads]
        qb = q.astype(BF16)
        s = [_dot_nt(qb[:, sl], kb[:, sl]) for sl in sls]
        s = [jnp.where(mask, s[h] + (cq[:, h:h + 1] - ck[h:h + 1, :]), -jnp.inf) for h in heads]
        m_prev = [m_sc[h] for h in heads]
        m_new = [jnp.maximum(m_prev[h], jnp.max(s[h], axis=-1, keepdims=True)) for h in heads]
        scale = [jnp.exp(m_prev[h] - m_new[h]) for h in heads]
        pexp = [jnp.exp(s[h] - m_new[h]).astype(BF16) for h in heads]
        psum = [_dot(pexp[h], ones)[:, :1] for h in heads]
        pv = [_dot(pexp[h], vb[:, sls[h]]) for h in heads]
        for h in heads:
            l_sc[h] = scale[h] * l_sc[h] + psum[h]
            acc_sc[:, sls[h]] = scale[h] * acc_sc[:, sls[h]] + pv[h]
            m_sc[h] = m_new[h]

    @pl.when(ki == n_kv - 1)
    def _():
        for h in range(N_HEADS):
            sl = slice(h * HEAD, (h + 1) * HEAD)
            o = acc_sc[:, sl] / l_sc[h]
            o = o * lax.rsqrt(jnp.mean(o * o, axis=-1, keepdims=True) + 1e-6)
            acc_sc[:, sl] = o
        o_ref[0] = acc_sc[...] * g_ref[...]


def _fox(q, k_all, v_all, c_col, c_row, norm_g, q_off, tq, tk):
    bsz, q_len, _ = q.shape
    kv_len = k_all.shape[1]
    n_q = pl.cdiv(q_len, tq)
    n_kv = pl.cdiv(kv_len, tk)
    assert q_off % tq == 0
    last_kv = lambda qi: (q_off + (qi + 1) * tq - 1) // tk
    kv_map = lambda b, qi, ki: (b, jnp.minimum(ki, last_kv(qi)), 0)
    return pl.pallas_call(
        functools.partial(_fox_body, kv_len, q_off, tq, tk, n_kv),
        grid=(bsz, n_q, n_kv),
        in_specs=[pl.BlockSpec((1, tq, D_HEADS), lambda b, qi, ki: (b, qi, 0)),
                  pl.BlockSpec((1, tk, D_HEADS), kv_map),
                  pl.BlockSpec((1, tk, D_HEADS), kv_map),
                  pl.BlockSpec((1, tq, N_HEADS), lambda b, qi, ki: (b, q_off // tq + qi, 0)),
                  pl.BlockSpec((1, N_HEADS, tk), lambda b, qi, ki: (b, 0, jnp.minimum(ki, last_kv(qi)))),
                  _full((1, D_HEADS))],
        out_specs=pl.BlockSpec((1, tq, D_HEADS), lambda b, qi, ki: (b, qi, 0)),
        out_shape=jax.ShapeDtypeStruct((bsz, q_len, D_HEADS), F32),
        scratch_shapes=[pltpu.VMEM((N_HEADS, tq, 1), F32), pltpu.VMEM((N_HEADS, tq, 1), F32),
                        pltpu.VMEM((tq, D_HEADS), F32)],
        compiler_params=_params(("parallel", "parallel", "arbitrary")),
        name="fox",
    )(q, k_all, v_all, c_col, c_row, norm_g)


def _out_proj_body(yr_ref, yf_ref, h_ref, wr_ref, wf_ref, g_ref, b_ref, o_ref):
    mix = _dot(yr_ref[...].astype(BF16), wr_ref[...]) + _dot(yf_ref[...].astype(BF16), wf_ref[...])
    o_ref[...] = _layernorm(ALPHA * h_ref[...] + mix, g_ref[...], b_ref[...])


def _out_proj(yr, yf, h, wr, wf, g, b):
    n = h.shape[0]
    tb = TOKEN_BLOCK
    rows = lambda w: pl.BlockSpec((tb, w), lambda i: (i, 0))
    return pl.pallas_call(
        _out_proj_body,
        grid=(pl.cdiv(n, tb),),
        in_specs=[rows(D_HEADS), rows(D_HEADS), rows(D_MODEL), _full(wr.shape), _full(wf.shape),
                  _full((1, D_MODEL)), _full((1, D_MODEL))],
        out_specs=rows(D_MODEL),
        out_shape=jax.ShapeDtypeStruct((n, D_MODEL), F32),
        compiler_params=_params(("parallel",)),
        name="out_proj",
    )(yr, yf, h, wr, wf, g, b)


_BIG = 2 ** 30
_CAND_COUNTS = tuple(PEER_TOPK // (i + 1) for i in range(PEER_TOPK))
_N_CAND = sum(_CAND_COUNTS)
_CAND_ROWS = -(-_N_CAND // 8) * 8


def _top_rows(s, order, k, payload=None):
    vals, outs = [], []
    for _ in range(k):
        m = jnp.max(s, axis=0, keepdims=True)
        first = jnp.min(jnp.where(s == m, order, _BIG), axis=0, keepdims=True)
        sel = order == first
        vals.append(m)
        outs.append(first if payload is None else jnp.max(jnp.where(sel, payload, -1), axis=0, keepdims=True))
        s = jnp.where(sel, -jnp.inf, s)
    return jnp.concatenate(vals, axis=0), jnp.concatenate(outs, axis=0)


def _topk_body(h_ref, wq_ref, k1_ref, k2_ref, pos_ref, idx_ref, gate_ref, cand_s_sc, cand_i_sc):
    tb = h_ref.shape[0]
    q = _dot(h_ref[...].astype(BF16), wq_ref[...]).astype(BF16)
    key_id = lax.broadcasted_iota(I32, (N_KEYS, LANES), 0)
    cand_pos = pos_ref[...]
    k1 = k1_ref[...].astype(BF16)
    k2 = k2_ref[...].astype(BF16)
    cand_s_sc[_N_CAND:, :] = jnp.full((_CAND_ROWS - _N_CAND, LANES), -jnp.inf, F32)
    cand_i_sc[_N_CAND:, :] = jnp.zeros((_CAND_ROWS - _N_CAND, LANES), I32)
    for part in range(tb // LANES):
        qp = q[part * LANES:(part + 1) * LANES]
        ids, gates = [], []
        for h in range(PEER_HEADS):
            o = h * 2 * D_KEY_HALF
            s1 = _dot_nt(k1, qp[:, o:o + D_KEY_HALF])
            s2 = _dot_nt(k2, qp[:, o + D_KEY_HALF:o + 2 * D_KEY_HALF])
            v1, i1 = _top_rows(s1, key_id, PEER_TOPK)
            v2, i2 = _top_rows(s2, key_id, PEER_TOPK)
            off = 0
            for i, cnt in enumerate(_CAND_COUNTS):
                cand_s_sc[off:off + cnt, :] = v1[i:i + 1] + v2[:cnt]
                cand_i_sc[off:off + cnt, :] = i1[i:i + 1] * N_KEYS + i2[:cnt]
                off += cnt
            best_s, best_i = _top_rows(cand_s_sc[...], cand_pos, PEER_TOPK, payload=cand_i_sc[...])
            e = jnp.exp(best_s - best_s[0:1])
            ids.append(best_i)
            gates.append(e / jnp.sum(e, axis=0, keepdims=True))
        rows = slice(part * LANES, (part + 1) * LANES)
        idx_ref[rows, :] = jnp.concatenate(ids, axis=0).T
        gate_ref[rows, :] = jnp.concatenate(gates, axis=0).T


def _topk(h1, wq, keys1, keys2):
    n = h1.shape[0]
    tb = TOKEN_BLOCK
    pos = [i * PEER_TOPK + j for i, cnt in enumerate(_CAND_COUNTS) for j in range(cnt)]
    pos = jnp.asarray(pos + [_BIG] * (_CAND_ROWS - _N_CAND), I32)
    pos = jnp.broadcast_to(pos[:, None], (_CAND_ROWS, LANES))
    return pl.pallas_call(
        _topk_body,
        grid=(pl.cdiv(n, tb),),
        in_specs=[pl.BlockSpec((tb, D_MODEL), lambda i: (i, 0)), _full(wq.shape), _full(keys1.shape),
                  _full(keys2.shape), _full(pos.shape)],
        out_specs=[pl.BlockSpec((tb, PEER_SLOTS), lambda i: (i, 0)), pl.BlockSpec((tb, PEER_SLOTS), lambda i: (i, 0))],
        out_shape=[jax.ShapeDtypeStruct((n, PEER_SLOTS), I32), jax.ShapeDtypeStruct((n, PEER_SLOTS), F32)],
        scratch_shapes=[pltpu.VMEM((_CAND_ROWS, LANES), F32), pltpu.VMEM((_CAND_ROWS, LANES), I32)],
        compiler_params=_params(("parallel",)),
        name="peer_topk",
    )(h1, wq, keys1, keys2, pos)


def _unpack_pair(w):
    return lax.bitcast_convert_type(w << 16, F32), lax.bitcast_convert_type(w & jnp.int32(-65536), F32)


_ERF_NUM = (-2.72614225801306e-10, 2.77068142495902e-08, -2.10102402082508e-06, -5.69250639462346e-05,
            -7.34990630326855e-04, -2.95459980854025e-03, -1.60960333262415e-02)
_ERF_DEN = (-1.45660718464996e-05, -2.13374055278905e-04, -1.68282697438203e-03, -7.37332916720468e-03,
            -1.42647390514189e-02)


def _erf_f32(x):
    x = jnp.minimum(jnp.maximum(x, -4.0), 4.0)
    x2 = x * x
    num = x2 * _ERF_NUM[0] + _ERF_NUM[1]
    for c in _ERF_NUM[2:]:
        num = num * x2 + c
    den = x2 * _ERF_DEN[0] + _ERF_DEN[1]
    for c in _ERF_DEN[2:]:
        den = den * x2 + c
    return x * num / den


def _expert_mix(u_tab, v_tab, idx, x, gate):
    n_tok, words = x.shape[0], u_tab.shape[1]
    lanes = SC_LANES
    pass_chunks = words // lanes // 2
    n_workers = SC_CORES * SC_SUBCORES
    assert n_tok % SC_UNIT == 0
    base, extra = divmod(n_tok // SC_UNIT, n_workers)
    idx_rows = 2 * SC_UNIT
    mesh = plsc.VectorSubcoreMesh(core_axis_name="c", subcore_axis_name="s",
                                  num_cores=SC_CORES, num_subcores=SC_SUBCORES)

    @functools.partial(
        pl.kernel, mesh=mesh, compiler_params=pltpu.CompilerParams(needs_layout_passes=False),
        out_type=jax.ShapeDtypeStruct((n_tok, 2 * words), F32),
        scratch_types=[pltpu.VMEM((2 * idx_rows, SC_HALF), I32),
                       pltpu.VMEM((SC_HALF, words), I32), pltpu.VMEM((SC_HALF, words), I32),
                       pltpu.VMEM((2 * SC_UNIT, 2 * words), F32),
                       pltpu.VMEM((2 * SC_UNIT, PEER_SLOTS), F32),
                       pltpu.VMEM((SC_UNIT, PEER_SLOTS), F32),
                       pltpu.VMEM((2 * SC_UNIT, 2 * words), F32),
                       pltpu.SemaphoreType.DMA, pltpu.SemaphoreType.DMA, pltpu.SemaphoreType.DMA,
                       pltpu.SemaphoreType.DMA((2,))],
    )
    def sc_kernel(u_hbm, v_hbm, idx_hbm, x_hbm, gate_hbm, out_hbm, idx_v, buf_a, buf_b, x_v, gate_v, act_v, out_v,
                  sem_a, sem_b, sem_in, sem_out):
        worker = lax.axis_index("s") * SC_CORES + lax.axis_index("c")
        first_unit = worker * base + jnp.minimum(worker, extra)
        my_units = base + (worker < extra).astype(I32)
        lane = lax.iota(I32, lanes)

        def first_token(ui):
            return pl.multiple_of((first_unit + ui) * SC_UNIT, SC_UNIT)

        def stage_in(ui, slot):
            tok0 = first_token(ui)
            toks = pl.ds(tok0, SC_UNIT)
            dst = pl.ds(slot * SC_UNIT, SC_UNIT)
            rows = pl.ds(pl.multiple_of(tok0 * 2, idx_rows), idx_rows)
            return (pltpu.make_async_copy(idx_hbm.at[rows], idx_v.at[pl.ds(slot * idx_rows, idx_rows)], sem_in),
                    pltpu.make_async_copy(x_hbm.at[toks], x_v.at[dst], sem_in),
                    pltpu.make_async_copy(gate_hbm.at[toks], gate_v.at[dst], sem_in))

        def write_out(ui, slot):
            return pltpu.make_async_copy(out_v.at[pl.ds(slot * SC_UNIT, SC_UNIT)],
                                         out_hbm.at[pl.ds(first_token(ui), SC_UNIT)], sem_out.at[slot])

        def gather(table_hbm, row, buf, sem):
            return pltpu.make_async_copy(table_hbm.at[idx_v.at[row]], buf, sem)

        def dot(buf, t, xrow, half):
            @pl.loop(0, SC_HALF // lanes)
            def _(g):
                def chunk(j, acc):
                    x_lo = x_v[xrow, pl.ds(j * lanes, lanes)]
                    x_hi = x_v[xrow, pl.ds(words + j * lanes, lanes)]
                    out = []
                    for r in range(lanes):
                        lo, hi = _unpack_pair(buf[g * lanes + r, pl.ds(j * lanes, lanes)])
                        out.append(acc[r] + lo * x_lo + hi * x_hi)
                    return tuple(out)

                acc = lax.fori_loop(0, words // lanes, chunk, tuple(jnp.zeros((lanes,), F32) for _ in range(lanes)))
                res = jnp.zeros((lanes,), F32)
                for r in range(lanes):
                    res = jnp.where(lane == r, jnp.sum(acc[r]), res)
                act_v[t, pl.ds(half * SC_HALF + g * lanes, lanes)] = res

        def gate_gelu(t, xrow):
            for g in range(PEER_SLOTS // lanes):
                cols = pl.ds(g * lanes, lanes)
                a = act_v[t, cols]
                act_v[t, cols] = gate_v[xrow, cols] * (a * 0.5 * (1.0 + _erf_f32(a * (2.0 ** -0.5))))

        def vsum(buf, t, orow, half):
            for p in range(2):
                cols = [(p * pass_chunks + jj) * lanes for jj in range(pass_chunks)]
                if half == 0:
                    init = tuple(jnp.zeros((lanes,), F32) for _ in range(2 * pass_chunks))
                else:
                    init = (tuple(out_v[orow, pl.ds(c, lanes)] for c in cols)
                            + tuple(out_v[orow, pl.ds(words + c, lanes)] for c in cols))

                def row(r, acc):
                    slot = jnp.full((lanes,), half * SC_HALF, I32) + r
                    c = plsc.load_gather(act_v, [jnp.full((lanes,), 0, I32) + t, slot])
                    lo_acc, hi_acc = list(acc[:pass_chunks]), list(acc[pass_chunks:])
                    for jj in range(pass_chunks):
                        lo, hi = _unpack_pair(buf[r, pl.ds(cols[jj], lanes)])
                        lo_acc[jj] = lo_acc[jj] + lo * c
                        hi_acc[jj] = hi_acc[jj] + hi * c
                    return tuple(lo_acc) + tuple(hi_acc)

                acc = lax.fori_loop(0, SC_HALF, row, init)
                for jj in range(pass_chunks):
                    out_v[orow, pl.ds(cols[jj], lanes)] = acc[jj]
                    out_v[orow, pl.ds(words + cols[jj], lanes)] = acc[pass_chunks + jj]

        def unit_body(ui, carry):
            slot = lax.rem(ui, 2)
            has_next = ui + 1 < my_units

            @pl.when(has_next)
            def _():
                for c in stage_in(ui + 1, 1 - slot):
                    c.start()

            @pl.when(ui >= 2)
            def _():
                write_out(ui - 2, slot).wait()

            @pl.loop(0, SC_UNIT)
            def _(t):
                r0 = slot * idx_rows + 2 * t
                urow = slot * SC_UNIT + t
                gather(u_hbm, r0 + 1, buf_b, sem_b).start()
                gather(u_hbm, r0, buf_a, sem_a).wait()
                dot(buf_a, t, urow, 0)
                gather(v_hbm, r0, buf_a, sem_a).start()
                gather(u_hbm, r0 + 1, buf_b, sem_b).wait()
                dot(buf_b, t, urow, 1)
                gate_gelu(t, urow)
                gather(v_hbm, r0 + 1, buf_b, sem_b).start()
                gather(v_hbm, r0, buf_a, sem_a).wait()
                vsum(buf_a, t, urow, 0)

                @pl.when(t + 1 < SC_UNIT)
                def _():
                    gather(u_hbm, r0 + 2, buf_a, sem_a).start()

                @pl.when((t + 1 == SC_UNIT) & has_next)
                def _():
                    for c in stage_in(ui + 1, 1 - slot):
                        c.wait()
                    gather(u_hbm, (1 - slot) * idx_rows, buf_a, sem_a).start()

                gather(v_hbm, r0 + 1, buf_b, sem_b).wait()
                vsum(buf_b, t, urow, 1)

            write_out(ui, slot).start()
            return carry

        @pl.when(my_units > 0)
        def _():
            for c in stage_in(0, 0):
                c.start()
            for c in stage_in(0, 0):
                c.wait()
            gather(u_hbm, 0, buf_a, sem_a).start()
            lax.fori_loop(0, my_units, unit_body, 0)

            @pl.when(my_units >= 2)
            def _():
                write_out(my_units - 2, lax.rem(my_units, 2)).wait()

            write_out(my_units - 1, lax.rem(my_units - 1, 2)).wait()

    return sc_kernel(u_tab, v_tab, idx.reshape(n_tok * 2, SC_HALF), x, gate)


def _final_body(x_ref, mix_ref, g_ref, b_ref, o_ref):
    o_ref[...] = _layernorm(ALPHA * x_ref[...] + mix_ref[...], g_ref[...], b_ref[...])


def _rowwise(body, name, n, widths_in, width_out, *args):
    tb = TOKEN_BLOCK
    rows = lambda w: pl.BlockSpec((tb, w), lambda i: (i, 0))
    n_rows_args = len(widths_in)
    return pl.pallas_call(
        body,
        grid=(pl.cdiv(n, tb),),
        in_specs=[rows(w) for w in widths_in] + [_full(a.shape) for a in args[n_rows_args:]],
        out_specs=rows(width_out),
        out_shape=jax.ShapeDtypeStruct((n, width_out), F32),
        compiler_params=_params(("parallel",)),
        name=name,
    )(*args)


def _group(x, shift0, wkv0, past, p, after):
    bsz, seq_len, _ = x.shape
    n = bsz * seq_len
    h, pr, q, k, v, lf = _in_proj(x.reshape(n, D_MODEL), p["ln_in_g"], p["ln_in_b"], p["w_r"], p["w_qkv"],
                                  p["w_f"], p["b_f"], after)
    pr = pr.reshape(bsz, seq_len, RWKV_IN)
    y_r, wkv_new = _rwkv(pr, shift0, wkv0, p)

    seq3 = lambda a: a.reshape(bsz, seq_len, -1)
    k3, v3, lf3 = seq3(k), seq3(v), seq3(lf)
    if past is None:
        k_all, v_all, lf_all, q_off, tq = k3, v3, lf3, 0, FOX_BLOCK
    else:
        k_past, v_past, lf_past = past
        q_off = k_past.shape[1]
        k_all = jnp.concatenate([k_past.reshape(bsz, q_off, D_HEADS), k3], axis=1)
        v_all = jnp.concatenate([v_past.reshape(bsz, q_off, D_HEADS), v3], axis=1)
        lf_all = jnp.concatenate([lf_past, lf3], axis=1)
        tq = seq_len
    c_col, c_row = _cumsum(lf_all)
    y_f = _fox(seq3(q), k_all, v_all, c_col, c_row, p["fox_norm_g"], q_off, tq, FOX_KV_BLOCK)

    h1 = _out_proj(y_r.reshape(n, D_HEADS), y_f.reshape(n, D_HEADS), h, p["w_out_r"], p["w_out_f"],
                   p["ln1_g"], p["ln1_b"])
    idx, gate = _topk(h1, p["peer_w_q"], p["peer_keys1"], p["peer_keys2"])
    mix = _expert_mix(p["peer_u"], p["peer_v"], idx, h1, gate)
    y = _rowwise(_final_body, "final_ln", n, (D_MODEL, D_MODEL), D_MODEL, h1, mix, p["ln2_g"], p["ln2_b"])

    heads = lambda a: a.reshape(1, bsz, seq_len, N_HEADS, HEAD)
    outs = (y.reshape(bsz, seq_len, D_MODEL), pr[None, :, seq_len - 1:, :], wkv_new[None],
            heads(k), heads(v), lf3[None])
    return outs, idx


def _pack_halves(a):
    bits = lax.bitcast_convert_type(a.astype(BF16), jnp.uint16).astype(jnp.uint32)
    w = a.shape[1] // 2
    return lax.bitcast_convert_type(bits[:, :w] | (bits[:, w:] << 16), I32)


def kernel(x_prompt, x_sample, state_rwkv_shift, state_rwkv_wkv, cache_fox_k, cache_fox_v, cache_fox_logf,
           meta_tokens, ln_in_g, ln_in_b, w_in, rwkv_mix, rwkv_w0, rwkv_w2, rwkv_a0, rwkv_a2, rwkv_g2,
           rwkv_k_k, rwkv_k_a, rwkv_r_k, rwkv_lnx_g, rwkv_lnx_b, fox_b_f, fox_norm_g, w_out, ln1_g, ln1_b,
           peer_w_q, peer_keys1, peer_keys2, peer_u, peer_v, ln2_g, ln2_b):
    assert w_in.shape[0] == 1, "single-layer step"
    bsz = x_prompt.shape[0]
    row = lambda a: a.reshape(1, -1)
    w = w_in[0]
    fox0 = RWKV_IN
    pad_f = LANES - N_HEADS
    p = dict(
        ln_in_g=row(ln_in_g), ln_in_b=row(ln_in_b),
        w_r=w[:, :RWKV_IN].astype(BF16),
        w_qkv=w[:, fox0:fox0 + 3 * D_HEADS].astype(BF16),
        w_f=jnp.pad(w[:, fox0 + 3 * D_HEADS:], ((0, 0), (0, pad_f))).astype(BF16),
        b_f=jnp.pad(row(fox_b_f[0]), ((0, 0), (0, pad_f))),
        rwkv_mix=rwkv_mix[0], rwkv_w0=rwkv_w0[0], rwkv_w2=rwkv_w2[0], rwkv_a0=rwkv_a0[0], rwkv_a2=rwkv_a2[0],
        rwkv_g2=rwkv_g2[0], rwkv_k_k=rwkv_k_k[0], rwkv_k_a=rwkv_k_a[0], rwkv_r_k=rwkv_r_k[0],
        rwkv_lnx_g=rwkv_lnx_g[0], rwkv_lnx_b=rwkv_lnx_b[0],
        fox_norm_g=row(fox_norm_g[0]),
        w_out_r=w_out[0, :D_HEADS].astype(BF16), w_out_f=w_out[0, D_HEADS:].astype(BF16),
        ln1_g=row(ln1_g[0]), ln1_b=row(ln1_b[0]),
        peer_w_q=peer_w_q[0].astype(BF16), peer_keys1=peer_keys1[0], peer_keys2=peer_keys2[0],
        peer_u=_pack_halves(peer_u[0]), peer_v=_pack_halves(peer_v[0]),
        ln2_g=row(ln2_g[0]), ln2_b=row(ln2_b[0]),
    )
    sizes = PROMPT_SLICES if sum(PROMPT_SLICES) == bsz else (bsz,)
    parts = []
    start = 0
    after = (p["peer_u"], p["peer_v"])
    for pb in sizes:
        meta = jnp.broadcast_to(meta_tokens[None], (pb, N_META, D_MODEL))
        hp0 = jnp.concatenate([meta, x_prompt[start:start + pb]], axis=1)
        (yp, *state), idx = _group(hp0, jnp.zeros((pb, 1, RWKV_IN), F32),
                                   jnp.zeros((pb, N_HEADS, HEAD, HEAD), F32), None, p, after)
        after = (idx,) if start == 0 else after
        parts.append((yp[:, N_META:], *state))
        start += pb
    (ys, s_shift, s_wkv, s_k, s_v, s_lf), _ = _group(
        x_sample, state_rwkv_shift[0], state_rwkv_wkv[0],
        (cache_fox_k[0], cache_fox_v[0], cache_fox_logf[0]), p, after)
    yp = jnp.concatenate([t[0] for t in parts], axis=0)
    p_shift, p_wkv, p_k, p_v, p_lf = (jnp.concatenate([t[j] for t in parts], axis=1) for j in range(1, 6))
    return (yp, ys, p_shift, p_wkv, p_k, p_v, p_lf, s_shift, s_wkv, s_k, s_v, s_lf)
```

```python
import functools

import jax
import jax.numpy as jnp
from jax import lax
from jax.experimental import pallas as pl
from jax.experimental.pallas import tpu as pltpu
from jax.experimental.pallas import tpu_sc as plsc

F32 = jnp.float32
BF16 = jnp.bfloat16
I32 = jnp.int32

D_MODEL = 1024
N_HEADS = 8
HEAD = 64
D_HEADS = N_HEADS * HEAD
LORA_W, LORA_A, LORA_G = 64, 64, 128
RWKV_IN = 3 * D_HEADS + LORA_W + LORA_A + LORA_G
N_META = 16
LNX_EPS = 64e-5
LN_EPS = 1e-5
ALPHA = 2.0 ** 0.25
N_KEYS = 128
PEER_HEADS = 8
PEER_TOPK = 16
PEER_SLOTS = PEER_HEADS * PEER_TOPK
D_KEY_HALF = 128

LANES = 128
CHUNK = 64
TOKEN_BLOCK = 256
FOX_BLOCK = 256
FOX_KV_BLOCK = 512
SC_CORES, SC_SUBCORES, SC_LANES = 2, 16, 16
SC_HALF = PEER_SLOTS // 2
SC_UNIT = 8
PROMPT_SLICES = (1, 1, 2, 2, 2, 2, 2, 2, 2)
VMEM_LIMIT = 48 * 1024 * 1024

HIGHEST = lax.Precision.HIGHEST


def _params(semantics):
    return pltpu.CompilerParams(dimension_semantics=semantics, vmem_limit_bytes=VMEM_LIMIT)


def _dot(a, b, precision=None):
    return jnp.dot(a, b, preferred_element_type=F32, precision=precision)


def _dot_nt(a, b, precision=None):
    return lax.dot_general(a, b, (((1,), (1,)), ((), ())), preferred_element_type=F32, precision=precision)


def _dot_tn(a, b, precision=None):
    return lax.dot_general(a, b, (((0,), (0,)), ((), ())), preferred_element_type=F32, precision=precision)


def _split3(a):
    a1 = a.astype(BF16)
    r1 = a - a1.astype(F32)
    a2 = r1.astype(BF16)
    a3 = (r1 - a2.astype(F32)).astype(BF16)
    return a1, a2, a3


def _split2(a):
    hi = a.astype(BF16)
    return hi, (a - hi.astype(F32)).astype(BF16)


def _mm3(dot, a, b):
    return dot(a[0], b[0]) + (dot(a[0], b[1]) + dot(a[1], b[0]))


def _dot_exact_rhs(a, m_bf16):
    a1, a2, a3 = _split3(a)
    return _dot(a1, m_bf16) + _dot(a2, m_bf16) + _dot(a3, m_bf16)


def _dot_exact_lhs(m_bf16, a):
    a1, a2, a3 = _split3(a)
    return _dot(m_bf16, a1) + _dot(m_bf16, a2) + _dot(m_bf16, a3)


def _layernorm(x, g, b):
    xc = x - jnp.mean(x, axis=-1, keepdims=True)
    var = jnp.mean(xc * xc, axis=-1, keepdims=True)
    return xc * lax.rsqrt(var + LN_EPS) * g + b


def _softplus(x):
    return jnp.maximum(x, 0.0) + jnp.log1p(jnp.exp(-jnp.abs(x)))


def _sigmoid(x):
    return 1.0 / (1.0 + jnp.exp(-x))


def _full(shape):
    return pl.BlockSpec(shape, lambda *_: (0,) * len(shape))


def _in_proj_body(n_after, x_ref, g_ref, b_ref, wr_ref, wqkv_ref, wf_ref, bf_ref, *refs):
    h_ref, pr_ref, q_ref, k_ref, v_ref, lf_ref = refs[n_after:]
    h = _layernorm(x_ref[...], g_ref[...], b_ref[...])
    h_ref[...] = h
    hb = h.astype(BF16)
    pr_ref[...] = _dot(hb, wr_ref[...])
    qkv = _dot(hb, wqkv_ref[...])
    q_ref[...] = qkv[:, :D_HEADS]
    k_ref[...] = qkv[:, D_HEADS:2 * D_HEADS]
    v_ref[...] = qkv[:, 2 * D_HEADS:]
    fl = _dot(hb, wf_ref[...]) + bf_ref[...]
    lf = jnp.minimum(fl, 0.0) - jnp.log1p(jnp.exp(-jnp.abs(fl)))
    lf_ref[...] = lf[:, :N_HEADS]


def _in_proj(x, g, b, wr, wqkv, wf, bfp, after=()):
    n = x.shape[0]
    tb = TOKEN_BLOCK
    rows = lambda w: pl.BlockSpec((tb, w), lambda i: (i, 0))
    return pl.pallas_call(
        functools.partial(_in_proj_body, len(after)),
        grid=(pl.cdiv(n, tb),),
        in_specs=[rows(D_MODEL), _full((1, D_MODEL)), _full((1, D_MODEL)), _full(wr.shape),
                  _full(wqkv.shape), _full(wf.shape), _full(bfp.shape)]
                 + [pl.BlockSpec(memory_space=pl.ANY)] * len(after),
        out_specs=[rows(D_MODEL), rows(RWKV_IN), rows(D_HEADS), rows(D_HEADS), rows(D_HEADS), rows(N_HEADS)],
        out_shape=[jax.ShapeDtypeStruct((n, w), F32)
                   for w in (D_MODEL, RWKV_IN, D_HEADS, D_HEADS, D_HEADS, N_HEADS)],
        compiler_params=_params(("parallel",)),
        name="in_proj",
    )(x, g, b, wr, wqkv, wf, bfp, *after)


def _rwkv_body(seq_len, n_chunks, pr_ref, shift0_ref, s0_ref, mix_ref, w0_ref, w2_ref, a0_ref, a2_ref,
               g2_ref, kk_ref, ka_ref, rk_ref, lng_ref, lnb_ref, seg_ref, tri_ref,
               y_ref, sout_ref, shift_sc, state_sc):
    c = pl.program_id(1)

    @pl.when(c == 0)
    def _():
        shift_sc[...] = shift0_ref[0]
        state_sc[...] = s0_ref[0]

    row = lax.broadcasted_iota(I32, (CHUNK, 1), 0)
    valid = (c * CHUNK + row) < seq_len
    prf = jnp.where(valid, pr_ref[0], 0.0)
    prev = jnp.where(row == 0, shift_sc[...], pltpu.roll(prf, 1, 0))
    shift_sc[...] = prf[CHUNK - 1:CHUNK, :]
    xm = prf + (prev - prf) * mix_ref[...]

    r = xm[:, :D_HEADS]
    k = xm[:, D_HEADS:2 * D_HEADS]
    v = xm[:, 2 * D_HEADS:3 * D_HEADS]
    o = 3 * D_HEADS
    xw = xm[:, o:o + LORA_W]
    xa = xm[:, o + LORA_W:o + LORA_W + LORA_A]
    xg = xm[:, o + LORA_W + LORA_A:]

    seg = seg_ref[...]
    wlog = -_softplus(-(w0_ref[...] + _dot(jnp.tanh(xw), w2_ref[...], HIGHEST))) - 0.5
    lw = jnp.where(valid, -jnp.exp(wlog), 0.0)
    a = _sigmoid(a0_ref[...] + _dot(xa, a2_ref[...], HIGHEST))
    g = _dot(_sigmoid(xg), g2_ref[...], HIGHEST)

    kk = k * kk_ref[...]
    kk_norm = jnp.sqrt(_dot_exact_rhs(kk * kk, seg))
    kk = jnp.where(valid, kk / jnp.maximum(kk_norm, 1e-12), 0.0)
    k = jnp.where(valid, k * (1.0 + (a - 1.0) * ka_ref[...]), 0.0)
    v = jnp.where(valid, v, 0.0)

    cl = _dot_exact_lhs(tri_ref[...], lw)
    dec_in = jnp.exp(cl)
    dec_out = jnp.exp(-cl)
    a_t = -kk * jnp.exp(cl - lw)
    b_t = kk * a * dec_out
    k_t = k * dec_out
    r_t = r * dec_in
    gamma = jnp.exp(cl[CHUNK - 1:CHUNK, :])

    ti = lax.broadcasted_iota(I32, (CHUNK, 2 * CHUNK), 0)
    si = lax.broadcasted_iota(I32, (CHUNK, 2 * CHUNK), 1)
    si = jnp.where(si >= CHUNK, si - CHUNK, si)
    strict = si < ti
    incl = si <= ti
    eye = (lax.broadcasted_iota(I32, (CHUNK, CHUNK), 0) == lax.broadcasted_iota(I32, (CHUNK, CHUNK), 1)).astype(F32)
    zeros = jnp.zeros((CHUNK, HEAD), F32)

    heads = range(N_HEADS)
    sls = [slice(h * HEAD, (h + 1) * HEAD) for h in heads]
    vh = [v[:, sl] for sl in sls]
    ar = [_split2(jnp.concatenate([a_t[:, sl], r_t[:, sl]], axis=0)) for sl in sls]
    bk = [_split2(jnp.concatenate([b_t[:, sl], k_t[:, sl]], axis=0)) for sl in sls]
    s0 = [state_sc[h] for h in heads]
    gram = [_mm3(_dot_nt, ar[h], bk[h]) for h in heads]
    top = [jnp.where(strict, gram[h][:CHUNK], 0.0) for h in heads]
    bot = [jnp.where(incl, gram[h][CHUNK:], 0.0) for h in heads]
    hs = [_mm3(_dot_nt, ar[h], _split2(s0[h])) for h in heads]
    rhs = [hs[h][:CHUNK] + _mm3(_dot, _split2(top[h]), _split2(jnp.concatenate([zeros, vh[h]], axis=0)))
           for h in heads]
    inv = [eye + top[h][:, :CHUNK] for h in heads]
    pws = [_split2(top[h][:, :CHUNK]) for h in heads]
    for _ in range(5):
        pws = [_split2(_mm3(_dot, pws[h], pws[h])) for h in heads]
        inv = [inv[h] + _mm3(_dot, _split2(inv[h]), pws[h]) for h in heads]
    u = [_mm3(_dot, _split2(inv[h]), _split2(rhs[h])) for h in heads]
    uv = [_split2(jnp.concatenate([u[h], vh[h]], axis=0)) for h in heads]
    ys = [hs[h][CHUNK:] + _dot(bot[h].astype(BF16), uv[h][0]) for h in heads]
    for h in heads:
        state_sc[h] = (s0[h] + _mm3(_dot_tn, uv[h], bk[h])) * gamma[:, sls[h]]

    y = jnp.concatenate(ys, axis=1)
    yc = y - _dot_exact_rhs(y, seg) * (1.0 / HEAD)
    var = _dot_exact_rhs(yc * yc, seg) * (1.0 / HEAD)
    y = yc * lax.rsqrt(var + LNX_EPS) * lng_ref[...] + lnb_ref[...]
    r = xm[:, :D_HEADS]
    bonus = _dot_exact_rhs(r * k * rk_ref[...], seg) * v
    y_ref[0] = (y + bonus) * g

    @pl.when(c == n_chunks - 1)
    def _():
        sout_ref[0] = state_sc[...]


def _rwkv(pr, shift0, wkv0, p):
    bsz, seq_len, _ = pr.shape
    n_chunks = pl.cdiv(seq_len, CHUNK)
    lane = jnp.arange(D_HEADS) // HEAD
    seg = (lane[:, None] == lane[None, :]).astype(BF16)
    tri = (jnp.arange(CHUNK)[None, :] <= jnp.arange(CHUNK)[:, None]).astype(BF16)
    vec = lambda a: a.reshape(1, -1)
    consts = [vec(p["rwkv_mix"]), vec(p["rwkv_w0"]), p["rwkv_w2"], vec(p["rwkv_a0"]), p["rwkv_a2"], p["rwkv_g2"],
              vec(p["rwkv_k_k"]), vec(p["rwkv_k_a"]), vec(p["rwkv_r_k"]), vec(p["rwkv_lnx_g"]),
              vec(p["rwkv_lnx_b"]), seg, tri]
    return pl.pallas_call(
        functools.partial(_rwkv_body, seq_len, n_chunks),
        grid=(bsz, n_chunks),
        in_specs=[pl.BlockSpec((1, CHUNK, RWKV_IN), lambda b, c: (b, c, 0)),
                  pl.BlockSpec((1, 1, RWKV_IN), lambda b, c: (b, 0, 0)),
                  pl.BlockSpec((1, N_HEADS, HEAD, HEAD), lambda b, c: (b, 0, 0, 0))]
                 + [_full(a.shape) for a in consts],
        out_specs=[pl.BlockSpec((1, CHUNK, D_HEADS), lambda b, c: (b, c, 0)),
                   pl.BlockSpec((1, N_HEADS, HEAD, HEAD), lambda b, c: (b, 0, 0, 0))],
        out_shape=[jax.ShapeDtypeStruct((bsz, seq_len, D_HEADS), F32),
                   jax.ShapeDtypeStruct((bsz, N_HEADS, HEAD, HEAD), F32)],
        scratch_shapes=[pltpu.VMEM((1, RWKV_IN), F32), pltpu.VMEM((N_HEADS, HEAD, HEAD), F32)],
        compiler_params=_params(("parallel", "arbitrary")),
        name="rwkv",
    )(pr, shift0, wkv0, *consts)


def _cumsum_body(seq_len, lf_ref, col_ref, row_ref):
    carry = jnp.zeros((1, N_HEADS), F32)
    eye = (lax.broadcasted_iota(I32, (N_HEADS, N_HEADS), 0)
           == lax.broadcasted_iota(I32, (N_HEADS, N_HEADS), 1)).astype(BF16)
    for start in range(0, seq_len, LANES):
        n = min(LANES, seq_len - start)
        tri = (lax.broadcasted_iota(I32, (n, n), 1) <= lax.broadcasted_iota(I32, (n, n), 0)).astype(BF16)
        cum = _dot_exact_lhs(tri, lf_ref[0, start:start + n, :]) + carry
        col_ref[0, start:start + n, :] = cum
        c1, c2, c3 = _split3(cum)
        row_ref[0, :, start:start + n] = _dot_nt(eye, c1) + _dot_nt(eye, c2) + _dot_nt(eye, c3)
        carry = cum[n - 1:n, :]


def _cumsum(lf):
    bsz, seq_len, _ = lf.shape
    return pl.pallas_call(
        functools.partial(_cumsum_body, seq_len),
        grid=(bsz,),
        in_specs=[pl.BlockSpec((1, seq_len, N_HEADS), lambda b: (b, 0, 0))],
        out_specs=[pl.BlockSpec((1, seq_len, N_HEADS), lambda b: (b, 0, 0)),
                   pl.BlockSpec((1, N_HEADS, seq_len), lambda b: (b, 0, 0))],
        out_shape=[jax.ShapeDtypeStruct((bsz, seq_len, N_HEADS), F32),
                   jax.ShapeDtypeStruct((bsz, N_HEADS, seq_len), F32)],
        compiler_params=_params(("parallel",)),
        name="cumsum",
    )(lf)


def _fox_body(kv_len, q_off, tq, tk, n_kv, q_ref, k_ref, v_ref, cq_ref, ck_ref, g_ref, o_ref, m_sc, l_sc, acc_sc):
    qi = pl.program_id(1)
    ki = pl.program_id(2)

    @pl.when(ki == 0)
    def _():
        m_sc[...] = jnp.full(m_sc.shape, -jnp.inf, F32)
        l_sc[...] = jnp.zeros(l_sc.shape, F32)
        acc_sc[...] = jnp.zeros(acc_sc.shape, F32)

    q_start = q_off + qi * tq
    k_start = ki * tk

    @pl.when(k_start <= q_start + tq - 1)
    def _():
        rows = q_start + lax.broadcasted_iota(I32, (tq, 1), 0)
        cols = k_start + lax.broadcasted_iota(I32, (1, tk), 1)
        mask = cols <= rows
        k_valid = (k_start + lax.broadcasted_iota(I32, (tk, 1), 0)) < kv_len
        q = q_ref[0] * (HEAD ** -0.5)
        kb = k_ref[0].astype(BF16)
        vb = jnp.where(k_valid, v_ref[0], 0.0).astype(BF16)
        cq = cq_ref[0]
        ck = ck_ref[0]
        ones = jnp.ones((tk, LANES), BF16)
        heads = range(N_HEADS)
        sls = [slice(h * HEAD, (h + 1) * HEAD) for h in heads]
        qb = q.astype(BF16)
        s = [_dot_nt(qb[:, sl], kb[:, sl]) for sl in sls]
        s = [jnp.where(mask, s[h] + (cq[:, h:h + 1] - ck[h:h + 1, :]), -jnp.inf) for h in heads]
        m_prev = [m_sc[h] for h in heads]
        m_new = [jnp.maximum(m_prev[h], jnp.max(s[h], axis=-1, keepdims=True)) for h in heads]
        scale = [jnp.exp(m_prev[h] - m_new[h]) for h in heads]
        pexp = [jnp.exp(s[h] - m_new[h]).astype(BF16) for h in heads]
        psum = [_dot(pexp[h], ones)[:, :1] for h in heads]
        pv = [_dot(pexp[h], vb[:, sls[h]]) for h in heads]
        for h in heads:
            l_sc[h] = scale[h] * l_sc[h] + psum[h]
            acc_sc[:, sls[h]] = scale[h] * acc_sc[:, sls[h]] + pv[h]
            m_sc[h] = m_new[h]

    @pl.when(ki == n_kv - 1)
    def _():
        for h in range(N_HEADS):
            sl = slice(h * HEAD, (h + 1) * HEAD)
            o = acc_sc[:, sl] / l_sc[h]
            o = o * lax.rsqrt(jnp.mean(o * o, axis=-1, keepdims=True) + 1e-6)
            acc_sc[:, sl] = o
        o_ref[0] = acc_sc[...] * g_ref[...]


def _fox(q, k_all, v_all, c_col, c_row, norm_g, q_off, tq, tk):
    bsz, q_len, _ = q.shape
    kv_len = k_all.shape[1]
    n_q = pl.cdiv(q_len, tq)
    n_kv = pl.cdiv(kv_len, tk)
    assert q_off % tq == 0
    last_kv = lambda qi: (q_off + (qi + 1) * tq - 1) // tk
    kv_map = lambda b, qi, ki: (b, jnp.minimum(ki, last_kv(qi)), 0)
    return pl.pallas_call(
        functools.partial(_fox_body, kv_len, q_off, tq, tk, n_kv),
        grid=(bsz, n_q, n_kv),
        in_specs=[pl.BlockSpec((1, tq, D_HEADS), lambda b, qi, ki: (b, qi, 0)),
                  pl.BlockSpec((1, tk, D_HEADS), kv_map),
                  pl.BlockSpec((1, tk, D_HEADS), kv_map),
                  pl.BlockSpec((1, tq, N_HEADS), lambda b, qi, ki: (b, q_off // tq + qi, 0)),
                  pl.BlockSpec((1, N_HEADS, tk), lambda b, qi, ki: (b, 0, jnp.minimum(ki, last_kv(qi)))),
                  _full((1, D_HEADS))],
        out_specs=pl.BlockSpec((1, tq, D_HEADS), lambda b, qi, ki: (b, qi, 0)),
        out_shape=jax.ShapeDtypeStruct((bsz, q_len, D_HEADS), F32),
        scratch_shapes=[pltpu.VMEM((N_HEADS, tq, 1), F32), pltpu.VMEM((N_HEADS, tq, 1), F32),
                        pltpu.VMEM((tq, D_HEADS), F32)],
        compiler_params=_params(("parallel", "parallel", "arbitrary")),
        name="fox",
    )(q, k_all, v_all, c_col, c_row, norm_g)


def _out_proj_body(yr_ref, yf_ref, h_ref, wr_ref, wf_ref, g_ref, b_ref, o_ref):
    mix = _dot(yr_ref[...].astype(BF16), wr_ref[...]) + _dot(yf_ref[...].astype(BF16), wf_ref[...])
    o_ref[...] = _layernorm(ALPHA * h_ref[...] + mix, g_ref[...], b_ref[...])


def _out_proj(yr, yf, h, wr, wf, g, b):
    n = h.shape[0]
    tb = TOKEN_BLOCK
    rows = lambda w: pl.BlockSpec((tb, w), lambda i: (i, 0))
    return pl.pallas_call(
        _out_proj_body,
        grid=(pl.cdiv(n, tb),),
        in_specs=[rows(D_HEADS), rows(D_HEADS), rows(D_MODEL), _full(wr.shape), _full(wf.shape),
                  _full((1, D_MODEL)), _full((1, D_MODEL))],
        out_specs=rows(D_MODEL),
        out_shape=jax.ShapeDtypeStruct((n, D_MODEL), F32),
        compiler_params=_params(("parallel",)),
        name="out_proj",
    )(yr, yf, h, wr, wf, g, b)


_BIG = 2 ** 30
_CAND_COUNTS = tuple(PEER_TOPK // (i + 1) for i in range(PEER_TOPK))
_N_CAND = sum(_CAND_COUNTS)
_CAND_ROWS = -(-_N_CAND // 8) * 8


def _top_rows(s, order, k, payload=None):
    vals, outs = [], []
    for _ in range(k):
        m = jnp.max(s, axis=0, keepdims=True)
        first = jnp.min(jnp.where(s == m, order, _BIG), axis=0, keepdims=True)
        sel = order == first
        vals.append(m)
        outs.append(first if payload is None else jnp.max(jnp.where(sel, payload, -1), axis=0, keepdims=True))
        s = jnp.where(sel, -jnp.inf, s)
    return jnp.concatenate(vals, axis=0), jnp.concatenate(outs, axis=0)


def _topk_body(h_ref, wq_ref, k1_ref, k2_ref, pos_ref, idx_ref, gate_ref, cand_s_sc, cand_i_sc):
    tb = h_ref.shape[0]
    q = _dot(h_ref[...].astype(BF16), wq_ref[...]).astype(BF16)
    key_id = lax.broadcasted_iota(I32, (N_KEYS, LANES), 0)
    cand_pos = pos_ref[...]
    k1 = k1_ref[...].astype(BF16)
    k2 = k2_ref[...].astype(BF16)
    cand_s_sc[_N_CAND:, :] = jnp.full((_CAND_ROWS - _N_CAND, LANES), -jnp.inf, F32)
    cand_i_sc[_N_CAND:, :] = jnp.zeros((_CAND_ROWS - _N_CAND, LANES), I32)
    for part in range(tb // LANES):
        qp = q[part * LANES:(part + 1) * LANES]
        ids, gates = [], []
        for h in range(PEER_HEADS):
            o = h * 2 * D_KEY_HALF
            s1 = _dot_nt(k1, qp[:, o:o + D_KEY_HALF])
            s2 = _dot_nt(k2, qp[:, o + D_KEY_HALF:o + 2 * D_KEY_HALF])
            v1, i1 = _top_rows(s1, key_id, PEER_TOPK)
            v2, i2 = _top_rows(s2, key_id, PEER_TOPK)
            off = 0
            for i, cnt in enumerate(_CAND_COUNTS):
                cand_s_sc[off:off + cnt, :] = v1[i:i + 1] + v2[:cnt]
                cand_i_sc[off:off + cnt, :] = i1[i:i + 1] * N_KEYS + i2[:cnt]
                off += cnt
            best_s, best_i = _top_rows(cand_s_sc[...], cand_pos, PEER_TOPK, payload=cand_i_sc[...])
            e = jnp.exp(best_s - best_s[0:1])
            ids.append(best_i)
            gates.append(e / jnp.sum(e, axis=0, keepdims=True))
        rows = slice(part * LANES, (part + 1) * LANES)
        idx_ref[rows, :] = jnp.concatenate(ids, axis=0).T
        gate_ref[rows, :] = jnp.concatenate(gates, axis=0).T


def _topk(h1, wq, keys1, keys2):
    n = h1.shape[0]
    tb = TOKEN_BLOCK
    pos = [i * PEER_TOPK + j for i, cnt in enumerate(_CAND_COUNTS) for j in range(cnt)]
    pos = jnp.asarray(pos + [_BIG] * (_CAND_ROWS - _N_CAND), I32)
    pos = jnp.broadcast_to(pos[:, None], (_CAND_ROWS, LANES))
    return pl.pallas_call(
        _topk_body,
        grid=(pl.cdiv(n, tb),),
        in_specs=[pl.BlockSpec((tb, D_MODEL), lambda i: (i, 0)), _full(wq.shape), _full(keys1.shape),
                  _full(keys2.shape), _full(pos.shape)],
        out_specs=[pl.BlockSpec((tb, PEER_SLOTS), lambda i: (i, 0)), pl.BlockSpec((tb, PEER_SLOTS), lambda i: (i, 0))],
        out_shape=[jax.ShapeDtypeStruct((n, PEER_SLOTS), I32), jax.ShapeDtypeStruct((n, PEER_SLOTS), F32)],
        scratch_shapes=[pltpu.VMEM((_CAND_ROWS, LANES), F32), pltpu.VMEM((_CAND_ROWS, LANES), I32)],
        compiler_params=_params(("parallel",)),
        name="peer_topk",
    )(h1, wq, keys1, keys2, pos)


def _unpack_pair(w):
    return lax.bitcast_convert_type(w << 16, F32), lax.bitcast_convert_type(w & jnp.int32(-65536), F32)


_ERF_NUM = (-2.72614225801306e-10, 2.77068142495902e-08, -2.10102402082508e-06, -5.69250639462346e-05,
            -7.34990630326855e-04, -2.95459980854025e-03, -1.60960333262415e-02)
_ERF_DEN = (-1.45660718464996e-05, -2.13374055278905e-04, -1.68282697438203e-03, -7.37332916720468e-03,
            -1.42647390514189e-02)


def _erf_f32(x):
    x = jnp.minimum(jnp.maximum(x, -4.0), 4.0)
    x2 = x * x
    num = x2 * _ERF_NUM[0] + _ERF_NUM[1]
    for c in _ERF_NUM[2:]:
        num = num * x2 + c
    den = x2 * _ERF_DEN[0] + _ERF_DEN[1]
    for c in _ERF_DEN[2:]:
        den = den * x2 + c
    return x * num / den


def _expert_mix(u_tab, v_tab, idx, x, gate):
    n_tok, words = x.shape[0], u_tab.shape[1]
    lanes = SC_LANES
    pass_chunks = words // lanes // 2
    n_workers = SC_CORES * SC_SUBCORES
    assert n_tok % SC_UNIT == 0
    base, extra = divmod(n_tok // SC_UNIT, n_workers)
    idx_rows = 2 * SC_UNIT
    mesh = plsc.VectorSubcoreMesh(core_axis_name="c", subcore_axis_name="s",
                                  num_cores=SC_CORES, num_subcores=SC_SUBCORES)

    @functools.partial(
        pl.kernel, mesh=mesh, compiler_params=pltpu.CompilerParams(needs_layout_passes=False),
        out_type=jax.ShapeDtypeStruct((n_tok, 2 * words), F32),
        scratch_types=[pltpu.VMEM((2 * idx_rows, SC_HALF), I32),
                       pltpu.VMEM((SC_HALF, words), I32), pltpu.VMEM((SC_HALF, words), I32),
                       pltpu.VMEM((2 * SC_UNIT, 2 * words), F32),
                       pltpu.VMEM((2 * SC_UNIT, PEER_SLOTS), F32),
                       pltpu.VMEM((SC_UNIT, PEER_SLOTS), F32),
                       pltpu.VMEM((2 * SC_UNIT, 2 * words), F32),
                       pltpu.SemaphoreType.DMA, pltpu.SemaphoreType.DMA, pltpu.SemaphoreType.DMA,
                       pltpu.SemaphoreType.DMA((2,))],
    )
    def sc_kernel(u_hbm, v_hbm, idx_hbm, x_hbm, gate_hbm, out_hbm, idx_v, buf_a, buf_b, x_v, gate_v, act_v, out_v,
                  sem_a, sem_b, sem_in, sem_out):
        worker = lax.axis_index("s") * SC_CORES + lax.axis_index("c")
        first_unit = worker * base + jnp.minimum(worker, extra)
        my_units = base + (worker < extra).astype(I32)
        lane = lax.iota(I32, lanes)

        def first_token(ui):
            return pl.multiple_of((first_unit + ui) * SC_UNIT, SC_UNIT)

        def stage_in(ui, slot):
            tok0 = first_token(ui)
            toks = pl.ds(tok0, SC_UNIT)
            dst = pl.ds(slot * SC_UNIT, SC_UNIT)
            rows = pl.ds(pl.multiple_of(tok0 * 2, idx_rows), idx_rows)
            return (pltpu.make_async_copy(idx_hbm.at[rows], idx_v.at[pl.ds(slot * idx_rows, idx_rows)], sem_in),
                    pltpu.make_async_copy(x_hbm.at[toks], x_v.at[dst], sem_in),
                    pltpu.make_async_copy(gate_hbm.at[toks], gate_v.at[dst], sem_in))

        def write_out(ui, slot):
            return pltpu.make_async_copy(out_v.at[pl.ds(slot * SC_UNIT, SC_UNIT)],
                                         out_hbm.at[pl.ds(first_token(ui), SC_UNIT)], sem_out.at[slot])

        def gather(table_hbm, row, buf, sem):
            return pltpu.make_async_copy(table_hbm.at[idx_v.at[row]], buf, sem)

        def dot(buf, t, xrow, half):
            @pl.loop(0, SC_HALF // lanes)
            def _(g):
                def chunk(j, acc):
                    x_lo = x_v[xrow, pl.ds(j * lanes, lanes)]
                    x_hi = x_v[xrow, pl.ds(words + j * lanes, lanes)]
                    out = []
                    for r in range(lanes):
                        lo, hi = _unpack_pair(buf[g * lanes + r, pl.ds(j * lanes, lanes)])
                        out.append(acc[r] + lo * x_lo + hi * x_hi)
                    return tuple(out)

                acc = lax.fori_loop(0, words // lanes, chunk, tuple(jnp.zeros((lanes,), F32) for _ in range(lanes)))
                res = jnp.zeros((lanes,), F32)
                for r in range(lanes):
                    res = jnp.where(lane == r, jnp.sum(acc[r]), res)
                act_v[t, pl.ds(half * SC_HALF + g * lanes, lanes)] = res

        def gate_gelu(t, xrow):
            for g in range(PEER_SLOTS // lanes):
                cols = pl.ds(g * lanes, lanes)
                a = act_v[t, cols]
                act_v[t, cols] = gate_v[xrow, cols] * (a * 0.5 * (1.0 + _erf_f32(a * (2.0 ** -0.5))))

        def vsum(buf, t, orow, half):
            for p in range(2):
                cols = [(p * pass_chunks + jj) * lanes for jj in range(pass_chunks)]
                if half == 0:
                    init = tuple(jnp.zeros((lanes,), F32) for _ in range(2 * pass_chunks))
                else:
                    init = (tuple(out_v[orow, pl.ds(c, lanes)] for c in cols)
                            + tuple(out_v[orow, pl.ds(words + c, lanes)] for c in cols))

                def row(r, acc):
                    slot = jnp.full((lanes,), half * SC_HALF, I32) + r
                    c = plsc.load_gather(act_v, [jnp.full((lanes,), 0, I32) + t, slot])
                    lo_acc, hi_acc = list(acc[:pass_chunks]), list(acc[pass_chunks:])
                    for jj in range(pass_chunks):
                        lo, hi = _unpack_pair(buf[r, pl.ds(cols[jj], lanes)])
                        lo_acc[jj] = lo_acc[jj] + lo * c
                        hi_acc[jj] = hi_acc[jj] + hi * c
                    return tuple(lo_acc) + tuple(hi_acc)

                acc = lax.fori_loop(0, SC_HALF, row, init)
                for jj in range(pass_chunks):
                    out_v[orow, pl.ds(cols[jj], lanes)] = acc[jj]
                    out_v[orow, pl.ds(words + cols[jj], lanes)] = acc[pass_chunks + jj]

        def unit_body(ui, carry):
            slot = lax.rem(ui, 2)
            has_next = ui + 1 < my_units

            @pl.when(has_next)
            def _():
                for c in stage_in(ui + 1, 1 - slot):
                    c.start()

            @pl.when(ui >= 2)
            def _():
                write_out(ui - 2, slot).wait()

            @pl.loop(0, SC_UNIT)
            def _(t):
                r0 = slot * idx_rows + 2 * t
                urow = slot * SC_UNIT + t
                gather(u_hbm, r0 + 1, buf_b, sem_b).start()
                gather(u_hbm, r0, buf_a, sem_a).wait()
                dot(buf_a, t, urow, 0)
                gather(v_hbm, r0, buf_a, sem_a).start()
                gather(u_hbm, r0 + 1, buf_b, sem_b).wait()
                dot(buf_b, t, urow, 1)
                gate_gelu(t, urow)
                gather(v_hbm, r0 + 1, buf_b, sem_b).start()
                gather(v_hbm, r0, buf_a, sem_a).wait()
                vsum(buf_a, t, urow, 0)

                @pl.when(t + 1 < SC_UNIT)
                def _():
                    gather(u_hbm, r0 + 2, buf_a, sem_a).start()

                @pl.when((t + 1 == SC_UNIT) & has_next)
                def _():
                    for c in stage_in(ui + 1, 1 - slot):
                        c.wait()
                    gather(u_hbm, (1 - slot) * idx_rows, buf_a, sem_a).start()

                gather(v_hbm, r0 + 1, buf_b, sem_b).wait()
                vsum(buf_b, t, urow, 1)

            write_out(ui, slot).start()
            return carry

        @pl.when(my_units > 0)
        def _():
            for c in stage_in(0, 0):
                c.start()
            for c in stage_in(0, 0):
                c.wait()
            gather(u_hbm, 0, buf_a, sem_a).start()
            lax.fori_loop(0, my_units, unit_body, 0)

            @pl.when(my_units >= 2)
            def _():
                write_out(my_units - 2, lax.rem(my_units, 2)).wait()

            write_out(my_units - 1, lax.rem(my_units - 1, 2)).wait()

    return sc_kernel(u_tab, v_tab, idx.reshape(n_tok * 2, SC_HALF), x, gate)


def _final_body(x_ref, mix_ref, g_ref, b_ref, o_ref):
    o_ref[...] = _layernorm(ALPHA * x_ref[...] + mix_ref[...], g_ref[...], b_ref[...])


def _rowwise(body, name, n, widths_in, width_out, *args):
    tb = TOKEN_BLOCK
    rows = lambda w: pl.BlockSpec((tb, w), lambda i: (i, 0))
    n_rows_args = len(widths_in)
    return pl.pallas_call(
        body,
        grid=(pl.cdiv(n, tb),),
        in_specs=[rows(w) for w in widths_in] + [_full(a.shape) for a in args[n_rows_args:]],
        out_specs=rows(width_out),
        out_shape=jax.ShapeDtypeStruct((n, width_out), F32),
        compiler_params=_params(("parallel",)),
        name=name,
    )(*args)


def _group(x, shift0, wkv0, past, p, after):
    bsz, seq_len, _ = x.shape
    n = bsz * seq_len
    h, pr, q, k, v, lf = _in_proj(x.reshape(n, D_MODEL), p["ln_in_g"], p["ln_in_b"], p["w_r"], p["w_qkv"],
                                  p["w_f"], p["b_f"], after)
    pr = pr.reshape(bsz, seq_len, RWKV_IN)
    y_r, wkv_new = _rwkv(pr, shift0, wkv0, p)

    seq3 = lambda a: a.reshape(bsz, seq_len, -1)
    k3, v3, lf3 = seq3(k), seq3(v), seq3(lf)
    if past is None:
        k_all, v_all, lf_all, q_off, tq = k3, v3, lf3, 0, FOX_BLOCK
    else:
        k_past, v_past, lf_past = past
        q_off = k_past.shape[1]
        k_all = jnp.concatenate([k_past.reshape(bsz, q_off, D_HEADS), k3], axis=1)
        v_all = jnp.concatenate([v_past.reshape(bsz, q_off, D_HEADS), v3], axis=1)
        lf_all = jnp.concatenate([lf_past, lf3], axis=1)
        tq = seq_len
    c_col, c_row = _cumsum(lf_all)
    y_f = _fox(seq3(q), k_all, v_all, c_col, c_row, p["fox_norm_g"], q_off, tq, FOX_KV_BLOCK)

    h1 = _out_proj(y_r.reshape(n, D_HEADS), y_f.reshape(n, D_HEADS), h, p["w_out_r"], p["w_out_f"],
                   p["ln1_g"], p["ln1_b"])
    idx, gate = _topk(h1, p["peer_w_q"], p["peer_keys1"], p["peer_keys2"])
    mix = _expert_mix(p["peer_u"], p["peer_v"], idx, h1, gate)
    y = _rowwise(_final_body, "final_ln", n, (D_MODEL, D_MODEL), D_MODEL, h1, mix, p["ln2_g"], p["ln2_b"])

    outs = (y.reshape(bsz, seq_len, D_MODEL), pr[None, :, seq_len - 1:, :], wkv_new[None], k3[None], v3[None],
            lf3[None])
    return outs, idx


def _pack_halves(a):
    bits = lax.bitcast_convert_type(a.astype(BF16), jnp.uint16).astype(jnp.uint32)
    w = a.shape[1] // 2
    return lax.bitcast_convert_type(bits[:, :w] | (bits[:, w:] << 16), I32)


def kernel(x_prompt, x_sample, state_rwkv_shift, state_rwkv_wkv, cache_fox_k, cache_fox_v, cache_fox_logf,
           meta_tokens, ln_in_g, ln_in_b, w_in, rwkv_mix, rwkv_w0, rwkv_w2, rwkv_a0, rwkv_a2, rwkv_g2,
           rwkv_k_k, rwkv_k_a, rwkv_r_k, rwkv_lnx_g, rwkv_lnx_b, fox_b_f, fox_norm_g, w_out, ln1_g, ln1_b,
           peer_w_q, peer_keys1, peer_keys2, peer_u, peer_v, ln2_g, ln2_b):
    assert w_in.shape[0] == 1, "single-layer step"
    bsz = x_prompt.shape[0]
    row = lambda a: a.reshape(1, -1)
    w = w_in[0]
    fox0 = RWKV_IN
    pad_f = LANES - N_HEADS
    p = dict(
        ln_in_g=row(ln_in_g), ln_in_b=row(ln_in_b),
        w_r=w[:, :RWKV_IN].astype(BF16),
        w_qkv=w[:, fox0:fox0 + 3 * D_HEADS].astype(BF16),
        w_f=jnp.pad(w[:, fox0 + 3 * D_HEADS:], ((0, 0), (0, pad_f))).astype(BF16),
        b_f=jnp.pad(row(fox_b_f[0]), ((0, 0), (0, pad_f))),
        rwkv_mix=rwkv_mix[0], rwkv_w0=rwkv_w0[0], rwkv_w2=rwkv_w2[0], rwkv_a0=rwkv_a0[0], rwkv_a2=rwkv_a2[0],
        rwkv_g2=rwkv_g2[0], rwkv_k_k=rwkv_k_k[0], rwkv_k_a=rwkv_k_a[0], rwkv_r_k=rwkv_r_k[0],
        rwkv_lnx_g=rwkv_lnx_g[0], rwkv_lnx_b=rwkv_lnx_b[0],
        fox_norm_g=row(fox_norm_g[0]),
        w_out_r=w_out[0, :D_HEADS].astype(BF16), w_out_f=w_out[0, D_HEADS:].astype(BF16),
        ln1_g=row(ln1_g[0]), ln1_b=row(ln1_b[0]),
        peer_w_q=peer_w_q[0].astype(BF16), peer_keys1=peer_keys1[0], peer_keys2=peer_keys2[0],
        peer_u=_pack_halves(peer_u[0]), peer_v=_pack_halves(peer_v[0]),
        ln2_g=row(ln2_g[0]), ln2_b=row(ln2_b[0]),
    )
    sizes = PROMPT_SLICES if sum(PROMPT_SLICES) == bsz else (bsz,)
    parts = []
    start = 0
    after = (p["peer_u"], p["peer_v"])
    for pb in sizes:
        meta = jnp.broadcast_to(meta_tokens[None], (pb, N_META, D_MODEL))
        hp0 = jnp.concatenate([meta, x_prompt[start:start + pb]], axis=1)
        (yp, *state), idx = _group(hp0, jnp.zeros((pb, 1, RWKV_IN), F32),
                                   jnp.zeros((pb, N_HEADS, HEAD, HEAD), F32), None, p, after)
        after = (idx,) if start == 0 else after
        parts.append((yp[:, N_META:], *state))
        start += pb
    (ys, s_shift, s_wkv, s_k, s_v, s_lf), _ = _group(
        x_sample, state_rwkv_shift[0], state_rwkv_wkv[0],
        (cache_fox_k[0], cache_fox_v[0], cache_fox_logf[0]), p, after)
    yp = jnp.concatenate([t[0] for t in parts], axis=0)
    p_shift, p_wkv, p_k, p_v, p_lf = (jnp.concatenate([t[j] for t in parts], axis=1) for j in range(1, 6))
    heads = lambda a: a.reshape(a.shape[:3] + (N_HEADS, HEAD))
    return (yp, ys, p_shift, p_wkv, heads(p_k), heads(p_v), p_lf, s_shift, s_wkv, heads(s_k), heads(s_v), s_lf)
```

```python
import functools

import jax
import jax.numpy as jnp
from jax import lax
from jax.experimental import pallas as pl
from jax.experimental.pallas import tpu as pltpu
from jax.experimental.pallas import tpu_sc as plsc

F32 = jnp.float32
BF16 = jnp.bfloat16
I32 = jnp.int32

D_MODEL = 1024
N_HEADS = 8
HEAD = 64
D_HEADS = N_HEADS * HEAD
LORA_W, LORA_A, LORA_G = 64, 64, 128
RWKV_IN = 3 * D_HEADS + LORA_W + LORA_A + LORA_G
N_META = 16
LNX_EPS = 64e-5
LN_EPS = 1e-5
ALPHA = 2.0 ** 0.25
N_KEYS = 128
PEER_HEADS = 8
PEER_TOPK = 16
PEER_SLOTS = PEER_HEADS * PEER_TOPK
D_KEY_HALF = 128

LANES = 128
CHUNK = 64
TOKEN_BLOCK = 256
FOX_BLOCK = 256
FOX_KV_BLOCK = 512
SC_CORES, SC_SUBCORES, SC_LANES = 2, 16, 16
SC_HALF = PEER_SLOTS // 2
SC_UNIT = 8
PROMPT_SLICES = (1, 1, 2, 2, 2, 2, 2, 2, 2)
VMEM_LIMIT = 48 * 1024 * 1024

HIGHEST = lax.Precision.HIGHEST


def _params(semantics):
    return pltpu.CompilerParams(dimension_semantics=semantics, vmem_limit_bytes=VMEM_LIMIT)


def _dot(a, b, precision=None):
    return jnp.dot(a, b, preferred_element_type=F32, precision=precision)


def _dot_nt(a, b, precision=None):
    return lax.dot_general(a, b, (((1,), (1,)), ((), ())), preferred_element_type=F32, precision=precision)


def _dot_tn(a, b, precision=None):
    return lax.dot_general(a, b, (((0,), (0,)), ((), ())), preferred_element_type=F32, precision=precision)


def _split3(a):
    a1 = a.astype(BF16)
    r1 = a - a1.astype(F32)
    a2 = r1.astype(BF16)
    a3 = (r1 - a2.astype(F32)).astype(BF16)
    return a1, a2, a3


def _split2(a):
    hi = a.astype(BF16)
    return hi, (a - hi.astype(F32)).astype(BF16)


def _mm3(dot, a, b):
    return dot(a[0], b[0]) + (dot(a[0], b[1]) + dot(a[1], b[0]))


def _dot_exact_rhs(a, m_bf16):
    a1, a2, a3 = _split3(a)
    return _dot(a1, m_bf16) + _dot(a2, m_bf16) + _dot(a3, m_bf16)


def _dot_exact_lhs(m_bf16, a):
    a1, a2, a3 = _split3(a)
    return _dot(m_bf16, a1) + _dot(m_bf16, a2) + _dot(m_bf16, a3)


def _layernorm(x, g, b):
    xc = x - jnp.mean(x, axis=-1, keepdims=True)
    var = jnp.mean(xc * xc, axis=-1, keepdims=True)
    return xc * lax.rsqrt(var + LN_EPS) * g + b


def _softplus(x):
    return jnp.maximum(x, 0.0) + jnp.log1p(jnp.exp(-jnp.abs(x)))


def _sigmoid(x):
    return 1.0 / (1.0 + jnp.exp(-x))


def _full(shape):
    return pl.BlockSpec(shape, lambda *_: (0,) * len(shape))


def _in_proj_body(x_ref, g_ref, b_ref, wr_ref, wqkv_ref, wf_ref, bf_ref,
                  h_ref, pr_ref, q_ref, k_ref, v_ref, lf_ref):
    h = _layernorm(x_ref[...], g_ref[...], b_ref[...])
    h_ref[...] = h
    hb = h.astype(BF16)
    pr_ref[...] = _dot(hb, wr_ref[...])
    qkv = _dot(hb, wqkv_ref[...])
    q_ref[...] = qkv[:, :D_HEADS]
    k_ref[...] = qkv[:, D_HEADS:2 * D_HEADS]
    v_ref[...] = qkv[:, 2 * D_HEADS:]
    fl = _dot(hb, wf_ref[...]) + bf_ref[...]
    lf = jnp.minimum(fl, 0.0) - jnp.log1p(jnp.exp(-jnp.abs(fl)))
    lf_ref[...] = lf[:, :N_HEADS]


def _in_proj(x, g, b, wr, wqkv, wf, bfp):
    n = x.shape[0]
    tb = TOKEN_BLOCK
    rows = lambda w: pl.BlockSpec((tb, w), lambda i: (i, 0))
    return pl.pallas_call(
        _in_proj_body,
        grid=(pl.cdiv(n, tb),),
        in_specs=[rows(D_MODEL), _full((1, D_MODEL)), _full((1, D_MODEL)), _full(wr.shape),
                  _full(wqkv.shape), _full(wf.shape), _full(bfp.shape)],
        out_specs=[rows(D_MODEL), rows(RWKV_IN), rows(D_HEADS), rows(D_HEADS), rows(D_HEADS), rows(N_HEADS)],
        out_shape=[jax.ShapeDtypeStruct((n, w), F32)
                   for w in (D_MODEL, RWKV_IN, D_HEADS, D_HEADS, D_HEADS, N_HEADS)],
        compiler_params=_params(("parallel",)),
        name="in_proj",
    )(x, g, b, wr, wqkv, wf, bfp)


def _rwkv_body(seq_len, n_chunks, pr_ref, shift0_ref, s0_ref, mix_ref, w0_ref, w2_ref, a0_ref, a2_ref,
               g2_ref, kk_ref, ka_ref, rk_ref, lng_ref, lnb_ref, seg_ref, tri_ref,
               y_ref, sout_ref, shift_sc, state_sc):
    c = pl.program_id(1)

    @pl.when(c == 0)
    def _():
        shift_sc[...] = shift0_ref[0]
        state_sc[...] = s0_ref[0]

    row = lax.broadcasted_iota(I32, (CHUNK, 1), 0)
    valid = (c * CHUNK + row) < seq_len
    prf = jnp.where(valid, pr_ref[0], 0.0)
    prev = jnp.where(row == 0, shift_sc[...], pltpu.roll(prf, 1, 0))
    shift_sc[...] = prf[CHUNK - 1:CHUNK, :]
    xm = prf + (prev - prf) * mix_ref[...]

    r = xm[:, :D_HEADS]
    k = xm[:, D_HEADS:2 * D_HEADS]
    v = xm[:, 2 * D_HEADS:3 * D_HEADS]
    o = 3 * D_HEADS
    xw = xm[:, o:o + LORA_W]
    xa = xm[:, o + LORA_W:o + LORA_W + LORA_A]
    xg = xm[:, o + LORA_W + LORA_A:]

    seg = seg_ref[...]
    wlog = -_softplus(-(w0_ref[...] + _dot(jnp.tanh(xw), w2_ref[...], HIGHEST))) - 0.5
    lw = jnp.where(valid, -jnp.exp(wlog), 0.0)
    a = _sigmoid(a0_ref[...] + _dot(xa, a2_ref[...], HIGHEST))
    g = _dot(_sigmoid(xg), g2_ref[...], HIGHEST)

    kk = k * kk_ref[...]
    kk_norm = jnp.sqrt(_dot_exact_rhs(kk * kk, seg))
    kk = jnp.where(valid, kk / jnp.maximum(kk_norm, 1e-12), 0.0)
    k = jnp.where(valid, k * (1.0 + (a - 1.0) * ka_ref[...]), 0.0)
    v = jnp.where(valid, v, 0.0)

    cl = _dot_exact_lhs(tri_ref[...], lw)
    dec_in = jnp.exp(cl)
    dec_out = jnp.exp(-cl)
    a_t = -kk * jnp.exp(cl - lw)
    b_t = kk * a * dec_out
    k_t = k * dec_out
    r_t = r * dec_in
    gamma = jnp.exp(cl[CHUNK - 1:CHUNK, :])

    ti = lax.broadcasted_iota(I32, (CHUNK, 2 * CHUNK), 0)
    si = lax.broadcasted_iota(I32, (CHUNK, 2 * CHUNK), 1)
    si = jnp.where(si >= CHUNK, si - CHUNK, si)
    strict = si < ti
    incl = si <= ti
    eye = (lax.broadcasted_iota(I32, (CHUNK, CHUNK), 0) == lax.broadcasted_iota(I32, (CHUNK, CHUNK), 1)).astype(F32)
    zeros = jnp.zeros((CHUNK, HEAD), F32)

    heads = range(N_HEADS)
    sls = [slice(h * HEAD, (h + 1) * HEAD) for h in heads]
    vh = [v[:, sl] for sl in sls]
    ar = [_split2(jnp.concatenate([a_t[:, sl], r_t[:, sl]], axis=0)) for sl in sls]
    bk = [_split2(jnp.concatenate([b_t[:, sl], k_t[:, sl]], axis=0)) for sl in sls]
    s0 = [state_sc[h] for h in heads]
    gram = [_mm3(_dot_nt, ar[h], bk[h]) for h in heads]
    top = [jnp.where(strict, gram[h][:CHUNK], 0.0) for h in heads]
    bot = [jnp.where(incl, gram[h][CHUNK:], 0.0) for h in heads]
    hs = [_mm3(_dot_nt, ar[h], _split2(s0[h])) for h in heads]
    rhs = [hs[h][:CHUNK] + _mm3(_dot, _split2(top[h]), _split2(jnp.concatenate([zeros, vh[h]], axis=0)))
           for h in heads]
    inv = [eye + top[h][:, :CHUNK] for h in heads]
    pws = [_split2(top[h][:, :CHUNK]) for h in heads]
    for _ in range(5):
        pws = [_split2(_mm3(_dot, pws[h], pws[h])) for h in heads]
        inv = [inv[h] + _mm3(_dot, _split2(inv[h]), pws[h]) for h in heads]
    u = [_mm3(_dot, _split2(inv[h]), _split2(rhs[h])) for h in heads]
    uv = [_split2(jnp.concatenate([u[h], vh[h]], axis=0)) for h in heads]
    ys = [hs[h][CHUNK:] + _dot(bot[h].astype(BF16), uv[h][0]) for h in heads]
    for h in heads:
        state_sc[h] = (s0[h] + _mm3(_dot_tn, uv[h], bk[h])) * gamma[:, sls[h]]

    y = jnp.concatenate(ys, axis=1)
    yc = y - _dot_exact_rhs(y, seg) * (1.0 / HEAD)
    var = _dot_exact_rhs(yc * yc, seg) * (1.0 / HEAD)
    y = yc * lax.rsqrt(var + LNX_EPS) * lng_ref[...] + lnb_ref[...]
    r = xm[:, :D_HEADS]
    bonus = _dot_exact_rhs(r * k * rk_ref[...], seg) * v
    y_ref[0] = (y + bonus) * g

    @pl.when(c == n_chunks - 1)
    def _():
        sout_ref[0] = state_sc[...]


def _rwkv(pr, shift0, wkv0, p):
    bsz, seq_len, _ = pr.shape
    n_chunks = pl.cdiv(seq_len, CHUNK)
    lane = jnp.arange(D_HEADS) // HEAD
    seg = (lane[:, None] == lane[None, :]).astype(BF16)
    tri = (jnp.arange(CHUNK)[None, :] <= jnp.arange(CHUNK)[:, None]).astype(BF16)
    vec = lambda a: a.reshape(1, -1)
    consts = [vec(p["rwkv_mix"]), vec(p["rwkv_w0"]), p["rwkv_w2"], vec(p["rwkv_a0"]), p["rwkv_a2"], p["rwkv_g2"],
              vec(p["rwkv_k_k"]), vec(p["rwkv_k_a"]), vec(p["rwkv_r_k"]), vec(p["rwkv_lnx_g"]),
              vec(p["rwkv_lnx_b"]), seg, tri]
    return pl.pallas_call(
        functools.partial(_rwkv_body, seq_len, n_chunks),
        grid=(bsz, n_chunks),
        in_specs=[pl.BlockSpec((1, CHUNK, RWKV_IN), lambda b, c: (b, c, 0)),
                  pl.BlockSpec((1, 1, RWKV_IN), lambda b, c: (b, 0, 0)),
                  pl.BlockSpec((1, N_HEADS, HEAD, HEAD), lambda b, c: (b, 0, 0, 0))]
                 + [_full(a.shape) for a in consts],
        out_specs=[pl.BlockSpec((1, CHUNK, D_HEADS), lambda b, c: (b, c, 0)),
                   pl.BlockSpec((1, N_HEADS, HEAD, HEAD), lambda b, c: (b, 0, 0, 0))],
        out_shape=[jax.ShapeDtypeStruct((bsz, seq_len, D_HEADS), F32),
                   jax.ShapeDtypeStruct((bsz, N_HEADS, HEAD, HEAD), F32)],
        scratch_shapes=[pltpu.VMEM((1, RWKV_IN), F32), pltpu.VMEM((N_HEADS, HEAD, HEAD), F32)],
        compiler_params=_params(("parallel", "arbitrary")),
        name="rwkv",
    )(pr, shift0, wkv0, *consts)


def _cumsum_body(seq_len, lf_ref, col_ref, row_ref):
    carry = jnp.zeros((1, N_HEADS), F32)
    eye = (lax.broadcasted_iota(I32, (N_HEADS, N_HEADS), 0)
           == lax.broadcasted_iota(I32, (N_HEADS, N_HEADS), 1)).astype(BF16)
    for start in range(0, seq_len, LANES):
        n = min(LANES, seq_len - start)
        tri = (lax.broadcasted_iota(I32, (n, n), 1) <= lax.broadcasted_iota(I32, (n, n), 0)).astype(BF16)
        cum = _dot_exact_lhs(tri, lf_ref[0, start:start + n, :]) + carry
        col_ref[0, start:start + n, :] = cum
        c1, c2, c3 = _split3(cum)
        row_ref[0, :, start:start + n] = _dot_nt(eye, c1) + _dot_nt(eye, c2) + _dot_nt(eye, c3)
        carry = cum[n - 1:n, :]


def _cumsum(lf):
    bsz, seq_len, _ = lf.shape
    return pl.pallas_call(
        functools.partial(_cumsum_body, seq_len),
        grid=(bsz,),
        in_specs=[pl.BlockSpec((1, seq_len, N_HEADS), lambda b: (b, 0, 0))],
        out_specs=[pl.BlockSpec((1, seq_len, N_HEADS), lambda b: (b, 0, 0)),
                   pl.BlockSpec((1, N_HEADS, seq_len), lambda b: (b, 0, 0))],
        out_shape=[jax.ShapeDtypeStruct((bsz, seq_len, N_HEADS), F32),
                   jax.ShapeDtypeStruct((bsz, N_HEADS, seq_len), F32)],
        compiler_params=_params(("parallel",)),
        name="cumsum",
    )(lf)


def _fox_body(kv_len, q_off, tq, tk, n_kv, q_ref, k_ref, v_ref, cq_ref, ck_ref, g_ref, o_ref, m_sc, l_sc, acc_sc):
    qi = pl.program_id(1)
    ki = pl.program_id(2)

    @pl.when(ki == 0)
    def _():
        m_sc[...] = jnp.full(m_sc.shape, -jnp.inf, F32)
        l_sc[...] = jnp.zeros(l_sc.shape, F32)
        acc_sc[...] = jnp.zeros(acc_sc.shape, F32)

    q_start = q_off + qi * tq
    k_start = ki * tk

    @pl.when(k_start <= q_start + tq - 1)
    def _():
        rows = q_start + lax.broadcasted_iota(I32, (tq, 1), 0)
        cols = k_start + lax.broadcasted_iota(I32, (1, tk), 1)
        mask = cols <= rows
        k_valid = (k_start + lax.broadcasted_iota(I32, (tk, 1), 0)) < kv_len
        q = q_ref[0] * (HEAD ** -0.5)
        kb = k_ref[0].astype(BF16)
        vb = jnp.where(k_valid, v_ref[0], 0.0).astype(BF16)
        cq = cq_ref[0]
        ck = ck_ref[0]
        ones = jnp.ones((tk, LANES), BF16)
        heads = range(N_HEADS)
        sls = [slice(h * HEAD, (h + 1) * HEAD) for h in heads]
        qb = q.astype(BF16)
        s = [_dot_nt(qb[:, sl], kb[:, sl]) for sl in sls]
        s = [jnp.where(mask, s[h] + (cq[:, h:h + 1] - ck[h:h + 1, :]), -jnp.inf) for h in heads]
        m_prev = [m_sc[h] for h in heads]
        m_new = [jnp.maximum(m_prev[h], jnp.max(s[h], axis=-1, keepdims=True)) for h in heads]
        scale = [jnp.exp(m_prev[h] - m_new[h]) for h in heads]
        pexp = [jnp.exp(s[h] - m_new[h]).astype(BF16) for h in heads]
        psum = [_dot(pexp[h], ones)[:, :1] for h in heads]
        pv = [_dot(pexp[h], vb[:, sls[h]]) for h in heads]
        for h in heads:
            l_sc[h] = scale[h] * l_sc[h] + psum[h]
            acc_sc[:, sls[h]] = scale[h] * acc_sc[:, sls[h]] + pv[h]
            m_sc[h] = m_new[h]

    @pl.when(ki == n_kv - 1)
    def _():
        for h in range(N_HEADS):
            sl = slice(h * HEAD, (h + 1) * HEAD)
            o = acc_sc[:, sl] / l_sc[h]
            o = o * lax.rsqrt(jnp.mean(o * o, axis=-1, keepdims=True) + 1e-6)
            acc_sc[:, sl] = o
        o_ref[0] = acc_sc[...] * g_ref[...]


def _fox(q, k_all, v_all, c_col, c_row, norm_g, q_off, tq, tk):
    bsz, q_len, _ = q.shape
    kv_len = k_all.shape[1]
    n_q = pl.cdiv(q_len, tq)
    n_kv = pl.cdiv(kv_len, tk)
    assert q_off % tq == 0
    last_kv = lambda qi: (q_off + (qi + 1) * tq - 1) // tk
    kv_map = lambda b, qi, ki: (b, jnp.minimum(ki, last_kv(qi)), 0)
    return pl.pallas_call(
        functools.partial(_fox_body, kv_len, q_off, tq, tk, n_kv),
        grid=(bsz, n_q, n_kv),
        in_specs=[pl.BlockSpec((1, tq, D_HEADS), lambda b, qi, ki: (b, qi, 0)),
                  pl.BlockSpec((1, tk, D_HEADS), kv_map),
                  pl.BlockSpec((1, tk, D_HEADS), kv_map),
                  pl.BlockSpec((1, tq, N_HEADS), lambda b, qi, ki: (b, q_off // tq + qi, 0)),
                  pl.BlockSpec((1, N_HEADS, tk), lambda b, qi, ki: (b, 0, jnp.minimum(ki, last_kv(qi)))),
                  _full((1, D_HEADS))],
        out_specs=pl.BlockSpec((1, tq, D_HEADS), lambda b, qi, ki: (b, qi, 0)),
        out_shape=jax.ShapeDtypeStruct((bsz, q_len, D_HEADS), F32),
        scratch_shapes=[pltpu.VMEM((N_HEADS, tq, 1), F32), pltpu.VMEM((N_HEADS, tq, 1), F32),
                        pltpu.VMEM((tq, D_HEADS), F32)],
        compiler_params=_params(("parallel", "parallel", "arbitrary")),
        name="fox",
    )(q, k_all, v_all, c_col, c_row, norm_g)


def _out_proj_body(yr_ref, yf_ref, h_ref, wr_ref, wf_ref, g_ref, b_ref, o_ref):
    mix = _dot(yr_ref[...].astype(BF16), wr_ref[...]) + _dot(yf_ref[...].astype(BF16), wf_ref[...])
    o_ref[...] = _layernorm(ALPHA * h_ref[...] + mix, g_ref[...], b_ref[...])


def _out_proj(yr, yf, h, wr, wf, g, b):
    n = h.shape[0]
    tb = TOKEN_BLOCK
    rows = lambda w: pl.BlockSpec((tb, w), lambda i: (i, 0))
    return pl.pallas_call(
        _out_proj_body,
        grid=(pl.cdiv(n, tb),),
        in_specs=[rows(D_HEADS), rows(D_HEADS), rows(D_MODEL), _full(wr.shape), _full(wf.shape),
                  _full((1, D_MODEL)), _full((1, D_MODEL))],
        out_specs=rows(D_MODEL),
        out_shape=jax.ShapeDtypeStruct((n, D_MODEL), F32),
        compiler_params=_params(("parallel",)),
        name="out_proj",
    )(yr, yf, h, wr, wf, g, b)


_BIG = 2 ** 30
_CAND_COUNTS = tuple(PEER_TOPK // (i + 1) for i in range(PEER_TOPK))
_N_CAND = sum(_CAND_COUNTS)
_CAND_ROWS = -(-_N_CAND // 8) * 8


def _top_rows(s, order, k, payload=None):
    vals, outs = [], []
    for _ in range(k):
        m = jnp.max(s, axis=0, keepdims=True)
        first = jnp.min(jnp.where(s == m, order, _BIG), axis=0, keepdims=True)
        sel = order == first
        vals.append(m)
        outs.append(first if payload is None else jnp.max(jnp.where(sel, payload, -1), axis=0, keepdims=True))
        s = jnp.where(sel, -jnp.inf, s)
    return jnp.concatenate(vals, axis=0), jnp.concatenate(outs, axis=0)


def _topk_body(h_ref, wq_ref, k1_ref, k2_ref, pos_ref, idx_ref, gate_ref, cand_s_sc, cand_i_sc):
    tb = h_ref.shape[0]
    q = _dot(h_ref[...].astype(BF16), wq_ref[...]).astype(BF16)
    key_id = lax.broadcasted_iota(I32, (N_KEYS, LANES), 0)
    cand_pos = pos_ref[...]
    k1 = k1_ref[...].astype(BF16)
    k2 = k2_ref[...].astype(BF16)
    cand_s_sc[_N_CAND:, :] = jnp.full((_CAND_ROWS - _N_CAND, LANES), -jnp.inf, F32)
    cand_i_sc[_N_CAND:, :] = jnp.zeros((_CAND_ROWS - _N_CAND, LANES), I32)
    for part in range(tb // LANES):
        qp = q[part * LANES:(part + 1) * LANES]
        ids, gates = [], []
        for h in range(PEER_HEADS):
            o = h * 2 * D_KEY_HALF
            s1 = _dot_nt(k1, qp[:, o:o + D_KEY_HALF])
            s2 = _dot_nt(k2, qp[:, o + D_KEY_HALF:o + 2 * D_KEY_HALF])
            v1, i1 = _top_rows(s1, key_id, PEER_TOPK)
            v2, i2 = _top_rows(s2, key_id, PEER_TOPK)
            off = 0
            for i, cnt in enumerate(_CAND_COUNTS):
                cand_s_sc[off:off + cnt, :] = v1[i:i + 1] + v2[:cnt]
                cand_i_sc[off:off + cnt, :] = i1[i:i + 1] * N_KEYS + i2[:cnt]
                off += cnt
            best_s, best_i = _top_rows(cand_s_sc[...], cand_pos, PEER_TOPK, payload=cand_i_sc[...])
            e = jnp.exp(best_s - best_s[0:1])
            ids.append(best_i)
            gates.append(e / jnp.sum(e, axis=0, keepdims=True))
        rows = slice(part * LANES, (part + 1) * LANES)
        idx_ref[rows, :] = jnp.concatenate(ids, axis=0).T
        gate_ref[rows, :] = jnp.concatenate(gates, axis=0).T


def _topk(h1, wq, keys1, keys2):
    n = h1.shape[0]
    tb = TOKEN_BLOCK
    pos = [i * PEER_TOPK + j for i, cnt in enumerate(_CAND_COUNTS) for j in range(cnt)]
    pos = jnp.asarray(pos + [_BIG] * (_CAND_ROWS - _N_CAND), I32)
    pos = jnp.broadcast_to(pos[:, None], (_CAND_ROWS, LANES))
    return pl.pallas_call(
        _topk_body,
        grid=(pl.cdiv(n, tb),),
        in_specs=[pl.BlockSpec((tb, D_MODEL), lambda i: (i, 0)), _full(wq.shape), _full(keys1.shape),
                  _full(keys2.shape), _full(pos.shape)],
        out_specs=[pl.BlockSpec((tb, PEER_SLOTS), lambda i: (i, 0)), pl.BlockSpec((tb, PEER_SLOTS), lambda i: (i, 0))],
        out_shape=[jax.ShapeDtypeStruct((n, PEER_SLOTS), I32), jax.ShapeDtypeStruct((n, PEER_SLOTS), F32)],
        scratch_shapes=[pltpu.VMEM((_CAND_ROWS, LANES), F32), pltpu.VMEM((_CAND_ROWS, LANES), I32)],
        compiler_params=_params(("parallel",)),
        name="peer_topk",
    )(h1, wq, keys1, keys2, pos)


def _unpack_pair(w):
    return lax.bitcast_convert_type(w << 16, F32), lax.bitcast_convert_type(w & jnp.int32(-65536), F32)


_ERF_NUM = (-2.72614225801306e-10, 2.77068142495902e-08, -2.10102402082508e-06, -5.69250639462346e-05,
            -7.34990630326855e-04, -2.95459980854025e-03, -1.60960333262415e-02)
_ERF_DEN = (-1.45660718464996e-05, -2.13374055278905e-04, -1.68282697438203e-03, -7.37332916720468e-03,
            -1.42647390514189e-02)


def _erf_f32(x):
    x = jnp.minimum(jnp.maximum(x, -4.0), 4.0)
    x2 = x * x
    num = x2 * _ERF_NUM[0] + _ERF_NUM[1]
    for c in _ERF_NUM[2:]:
        num = num * x2 + c
    den = x2 * _ERF_DEN[0] + _ERF_DEN[1]
    for c in _ERF_DEN[2:]:
        den = den * x2 + c
    return x * num / den


def _expert_mix(u_tab, v_tab, idx, x, gate):
    n_tok, words = x.shape[0], u_tab.shape[1]
    lanes = SC_LANES
    pass_chunks = words // lanes // 2
    n_workers = SC_CORES * SC_SUBCORES
    assert n_tok % SC_UNIT == 0
    base, extra = divmod(n_tok // SC_UNIT, n_workers)
    idx_rows = 2 * SC_UNIT
    mesh = plsc.VectorSubcoreMesh(core_axis_name="c", subcore_axis_name="s",
                                  num_cores=SC_CORES, num_subcores=SC_SUBCORES)

    @functools.partial(
        pl.kernel, mesh=mesh, compiler_params=pltpu.CompilerParams(needs_layout_passes=False),
        out_type=jax.ShapeDtypeStruct((n_tok, 2 * words), F32),
        scratch_types=[pltpu.VMEM((2 * idx_rows, SC_HALF), I32),
                       pltpu.VMEM((SC_HALF, words), I32), pltpu.VMEM((SC_HALF, words), I32),
                       pltpu.VMEM((2 * SC_UNIT, 2 * words), F32),
                       pltpu.VMEM((2 * SC_UNIT, PEER_SLOTS), F32),
                       pltpu.VMEM((SC_UNIT, PEER_SLOTS), F32),
                       pltpu.VMEM((2 * SC_UNIT, 2 * words), F32),
                       pltpu.SemaphoreType.DMA, pltpu.SemaphoreType.DMA, pltpu.SemaphoreType.DMA,
                       pltpu.SemaphoreType.DMA((2,))],
    )
    def sc_kernel(u_hbm, v_hbm, idx_hbm, x_hbm, gate_hbm, out_hbm, idx_v, buf_a, buf_b, x_v, gate_v, act_v, out_v,
                  sem_a, sem_b, sem_in, sem_out):
        worker = lax.axis_index("s") * SC_CORES + lax.axis_index("c")
        first_unit = worker * base + jnp.minimum(worker, extra)
        my_units = base + (worker < extra).astype(I32)
        lane = lax.iota(I32, lanes)

        def first_token(ui):
            return pl.multiple_of((first_unit + ui) * SC_UNIT, SC_UNIT)

        def stage_in(ui, slot):
            tok0 = first_token(ui)
            toks = pl.ds(tok0, SC_UNIT)
            dst = pl.ds(slot * SC_UNIT, SC_UNIT)
            rows = pl.ds(pl.multiple_of(tok0 * 2, idx_rows), idx_rows)
            return (pltpu.make_async_copy(idx_hbm.at[rows], idx_v.at[pl.ds(slot * idx_rows, idx_rows)], sem_in),
                    pltpu.make_async_copy(x_hbm.at[toks], x_v.at[dst], sem_in),
                    pltpu.make_async_copy(gate_hbm.at[toks], gate_v.at[dst], sem_in))

        def write_out(ui, slot):
            return pltpu.make_async_copy(out_v.at[pl.ds(slot * SC_UNIT, SC_UNIT)],
                                         out_hbm.at[pl.ds(first_token(ui), SC_UNIT)], sem_out.at[slot])

        def gather(table_hbm, row, buf, sem):
            return pltpu.make_async_copy(table_hbm.at[idx_v.at[row]], buf, sem)

        def dot(buf, t, xrow, half):
            @pl.loop(0, SC_HALF // lanes)
            def _(g):
                def chunk(j, acc):
                    x_lo = x_v[xrow, pl.ds(j * lanes, lanes)]
                    x_hi = x_v[xrow, pl.ds(words + j * lanes, lanes)]
                    out = []
                    for r in range(lanes):
                        lo, hi = _unpack_pair(buf[g * lanes + r, pl.ds(j * lanes, lanes)])
                        out.append(acc[r] + lo * x_lo + hi * x_hi)
                    return tuple(out)

                acc = lax.fori_loop(0, words // lanes, chunk, tuple(jnp.zeros((lanes,), F32) for _ in range(lanes)))
                res = jnp.zeros((lanes,), F32)
                for r in range(lanes):
                    res = jnp.where(lane == r, jnp.sum(acc[r]), res)
                act_v[t, pl.ds(half * SC_HALF + g * lanes, lanes)] = res

        def gate_gelu(t, xrow):
            for g in range(PEER_SLOTS // lanes):
                cols = pl.ds(g * lanes, lanes)
                a = act_v[t, cols]
                act_v[t, cols] = gate_v[xrow, cols] * (a * 0.5 * (1.0 + _erf_f32(a * (2.0 ** -0.5))))

        def vsum(buf, t, orow, half):
            for p in range(2):
                cols = [(p * pass_chunks + jj) * lanes for jj in range(pass_chunks)]
                if half == 0:
                    init = tuple(jnp.zeros((lanes,), F32) for _ in range(2 * pass_chunks))
                else:
                    init = (tuple(out_v[orow, pl.ds(c, lanes)] for c in cols)
                            + tuple(out_v[orow, pl.ds(words + c, lanes)] for c in cols))

                def row(r, acc):
                    slot = jnp.full((lanes,), half * SC_HALF, I32) + r
                    c = plsc.load_gather(act_v, [jnp.full((lanes,), 0, I32) + t, slot])
                    lo_acc, hi_acc = list(acc[:pass_chunks]), list(acc[pass_chunks:])
                    for jj in range(pass_chunks):
                        lo, hi = _unpack_pair(buf[r, pl.ds(cols[jj], lanes)])
                        lo_acc[jj] = lo_acc[jj] + lo * c
                        hi_acc[jj] = hi_acc[jj] + hi * c
                    return tuple(lo_acc) + tuple(hi_acc)

                acc = lax.fori_loop(0, SC_HALF, row, init)
                for jj in range(pass_chunks):
                    out_v[orow, pl.ds(cols[jj], lanes)] = acc[jj]
                    out_v[orow, pl.ds(words + cols[jj], lanes)] = acc[pass_chunks + jj]

        def unit_body(ui, carry):
            slot = lax.rem(ui, 2)
            has_next = ui + 1 < my_units

            @pl.when(has_next)
            def _():
                for c in stage_in(ui + 1, 1 - slot):
                    c.start()

            @pl.when(ui >= 2)
            def _():
                write_out(ui - 2, slot).wait()

            @pl.loop(0, SC_UNIT)
            def _(t):
                r0 = slot * idx_rows + 2 * t
                urow = slot * SC_UNIT + t
                gather(u_hbm, r0 + 1, buf_b, sem_b).start()
                gather(u_hbm, r0, buf_a, sem_a).wait()
                dot(buf_a, t, urow, 0)
                gather(v_hbm, r0, buf_a, sem_a).start()
                gather(u_hbm, r0 + 1, buf_b, sem_b).wait()
                dot(buf_b, t, urow, 1)
                gate_gelu(t, urow)
                gather(v_hbm, r0 + 1, buf_b, sem_b).start()
                gather(v_hbm, r0, buf_a, sem_a).wait()
                vsum(buf_a, t, urow, 0)

                @pl.when(t + 1 < SC_UNIT)
                def _():
                    gather(u_hbm, r0 + 2, buf_a, sem_a).start()

                @pl.when((t + 1 == SC_UNIT) & has_next)
                def _():
                    for c in stage_in(ui + 1, 1 - slot):
                        c.wait()
                    gather(u_hbm, (1 - slot) * idx_rows, buf_a, sem_a).start()

                gather(v_hbm, r0 + 1, buf_b, sem_b).wait()
                vsum(buf_b, t, urow, 1)

            write_out(ui, slot).start()
            return carry

        @pl.when(my_units > 0)
        def _():
            for c in stage_in(0, 0):
                c.start()
            for c in stage_in(0, 0):
                c.wait()
            gather(u_hbm, 0, buf_a, sem_a).start()
            lax.fori_loop(0, my_units, unit_body, 0)

            @pl.when(my_units >= 2)
            def _():
                write_out(my_units - 2, lax.rem(my_units, 2)).wait()

            write_out(my_units - 1, lax.rem(my_units - 1, 2)).wait()

    return sc_kernel(u_tab, v_tab, idx.reshape(n_tok * 2, SC_HALF), x, gate)


def _final_body(x_ref, mix_ref, g_ref, b_ref, o_ref):
    o_ref[...] = _layernorm(ALPHA * x_ref[...] + mix_ref[...], g_ref[...], b_ref[...])


def _rowwise(body, name, n, widths_in, width_out, *args):
    tb = TOKEN_BLOCK
    rows = lambda w: pl.BlockSpec((tb, w), lambda i: (i, 0))
    n_rows_args = len(widths_in)
    return pl.pallas_call(
        body,
        grid=(pl.cdiv(n, tb),),
        in_specs=[rows(w) for w in widths_in] + [_full(a.shape) for a in args[n_rows_args:]],
        out_specs=rows(width_out),
        out_shape=jax.ShapeDtypeStruct((n, width_out), F32),
        compiler_params=_params(("parallel",)),
        name=name,
    )(*args)


def _group(x, shift0, wkv0, past, p):
    bsz, seq_len, _ = x.shape
    n = bsz * seq_len
    h, pr, q, k, v, lf = _in_proj(x.reshape(n, D_MODEL), p["ln_in_g"], p["ln_in_b"], p["w_r"], p["w_qkv"],
                                  p["w_f"], p["b_f"])
    pr = pr.reshape(bsz, seq_len, RWKV_IN)
    y_r, wkv_new = _rwkv(pr, shift0, wkv0, p)

    seq3 = lambda a: a.reshape(bsz, seq_len, -1)
    k3, v3, lf3 = seq3(k), seq3(v), seq3(lf)
    if past is None:
        k_all, v_all, lf_all, q_off, tq = k3, v3, lf3, 0, FOX_BLOCK
    else:
        k_past, v_past, lf_past = past
        q_off = k_past.shape[1]
        k_all = jnp.concatenate([k_past.reshape(bsz, q_off, D_HEADS), k3], axis=1)
        v_all = jnp.concatenate([v_past.reshape(bsz, q_off, D_HEADS), v3], axis=1)
        lf_all = jnp.concatenate([lf_past, lf3], axis=1)
        tq = seq_len
    c_col, c_row = _cumsum(lf_all)
    y_f = _fox(seq3(q), k_all, v_all, c_col, c_row, p["fox_norm_g"], q_off, tq, FOX_KV_BLOCK)

    h1 = _out_proj(y_r.reshape(n, D_HEADS), y_f.reshape(n, D_HEADS), h, p["w_out_r"], p["w_out_f"],
                   p["ln1_g"], p["ln1_b"])
    idx, gate = _topk(h1, p["peer_w_q"], p["peer_keys1"], p["peer_keys2"])
    mix = _expert_mix(p["peer_u"], p["peer_v"], idx, h1, gate)
    y = _rowwise(_final_body, "final_ln", n, (D_MODEL, D_MODEL), D_MODEL, h1, mix, p["ln2_g"], p["ln2_b"])

    heads = lambda a: a.reshape(1, bsz, seq_len, N_HEADS, HEAD)
    return (y.reshape(bsz, seq_len, D_MODEL), pr[None, :, seq_len - 1:, :], wkv_new[None],
            heads(k), heads(v), lf3[None])


def _pack_halves(a):
    bits = lax.bitcast_convert_type(a.astype(BF16), jnp.uint16).astype(jnp.uint32)
    w = a.shape[1] // 2
    return lax.bitcast_convert_type(bits[:, :w] | (bits[:, w:] << 16), I32)


def kernel(x_prompt, x_sample, state_rwkv_shift, state_rwkv_wkv, cache_fox_k, cache_fox_v, cache_fox_logf,
           meta_tokens, ln_in_g, ln_in_b, w_in, rwkv_mix, rwkv_w0, rwkv_w2, rwkv_a0, rwkv_a2, rwkv_g2,
           rwkv_k_k, rwkv_k_a, rwkv_r_k, rwkv_lnx_g, rwkv_lnx_b, fox_b_f, fox_norm_g, w_out, ln1_g, ln1_b,
           peer_w_q, peer_keys1, peer_keys2, peer_u, peer_v, ln2_g, ln2_b):
    assert w_in.shape[0] == 1, "single-layer step"
    bsz = x_prompt.shape[0]
    row = lambda a: a.reshape(1, -1)
    w = w_in[0]
    fox0 = RWKV_IN
    pad_f = LANES - N_HEADS
    p = dict(
        ln_in_g=row(ln_in_g), ln_in_b=row(ln_in_b),
        w_r=w[:, :RWKV_IN].astype(BF16),
        w_qkv=w[:, fox0:fox0 + 3 * D_HEADS].astype(BF16),
        w_f=jnp.pad(w[:, fox0 + 3 * D_HEADS:], ((0, 0), (0, pad_f))).astype(BF16),
        b_f=jnp.pad(row(fox_b_f[0]), ((0, 0), (0, pad_f))),
        rwkv_mix=rwkv_mix[0], rwkv_w0=rwkv_w0[0], rwkv_w2=rwkv_w2[0], rwkv_a0=rwkv_a0[0], rwkv_a2=rwkv_a2[0],
        rwkv_g2=rwkv_g2[0], rwkv_k_k=rwkv_k_k[0], rwkv_k_a=rwkv_k_a[0], rwkv_r_k=rwkv_r_k[0],
        rwkv_lnx_g=rwkv_lnx_g[0], rwkv_lnx_b=rwkv_lnx_b[0],
        fox_norm_g=row(fox_norm_g[0]),
        w_out_r=w_out[0, :D_HEADS].astype(BF16), w_out_f=w_out[0, D_HEADS:].astype(BF16),
        ln1_g=row(ln1_g[0]), ln1_b=row(ln1_b[0]),
        peer_w_q=peer_w_q[0].astype(BF16), peer_keys1=peer_keys1[0], peer_keys2=peer_keys2[0],
        peer_u=_pack_halves(peer_u[0]), peer_v=_pack_halves(peer_v[0]),
        ln2_g=row(ln2_g[0]), ln2_b=row(ln2_b[0]),
    )
    sizes = PROMPT_SLICES if sum(PROMPT_SLICES) == bsz else (bsz,)
    ys, s_shift, s_wkv, s_k, s_v, s_lf = _group(
        x_sample, state_rwkv_shift[0], state_rwkv_wkv[0],
        (cache_fox_k[0], cache_fox_v[0], cache_fox_logf[0]), p)
    parts = []
    start = 0
    for pb in sizes:
        meta = jnp.broadcast_to(meta_tokens[None], (pb, N_META, D_MODEL))
        hp0 = jnp.concatenate([meta, x_prompt[start:start + pb]], axis=1)
        yp, *state = _group(hp0, jnp.zeros((pb, 1, RWKV_IN), F32), jnp.zeros((pb, N_HEADS, HEAD, HEAD), F32), None, p)
        parts.append((yp[:, N_META:], *state))
        start += pb
    yp = jnp.concatenate([t[0] for t in parts], axis=0)
    p_shift, p_wkv, p_k, p_v, p_lf = (jnp.concatenate([t[j] for t in parts], axis=1) for j in range(1, 6))
    return (yp, ys, p_shift, p_wkv, p_k, p_v, p_lf, s_shift, s_wkv, s_k, s_v, s_lf)
```

```python
import functools

import jax
import jax.numpy as jnp
from jax import lax
from jax.experimental import pallas as pl
from jax.experimental.pallas import tpu as pltpu
from jax.experimental.pallas import tpu_sc as plsc

F32 = jnp.float32
BF16 = jnp.bfloat16
I32 = jnp.int32

D_MODEL = 1024
N_HEADS = 8
HEAD = 64
D_HEADS = N_HEADS * HEAD
LORA_W, LORA_A, LORA_G = 64, 64, 128
RWKV_IN = 3 * D_HEADS + LORA_W + LORA_A + LORA_G
N_META = 16
LNX_EPS = 64e-5
LN_EPS = 1e-5
ALPHA = 2.0 ** 0.25
N_KEYS = 128
PEER_HEADS = 8
PEER_TOPK = 16
PEER_SLOTS = PEER_HEADS * PEER_TOPK
D_KEY_HALF = 128

LANES = 128
CHUNK = 64
TOKEN_BLOCK = 256
FOX_BLOCK = 256
FOX_KV_BLOCK = 512
SC_CORES, SC_SUBCORES, SC_LANES = 2, 16, 16
SC_HALF = PEER_SLOTS // 2
SC_UNIT = 8
PROMPT_SLICES = (1, 1, 2, 2, 2, 2, 2, 2, 2)
VMEM_LIMIT = 48 * 1024 * 1024

HIGHEST = lax.Precision.HIGHEST


def _params(semantics):
    return pltpu.CompilerParams(dimension_semantics=semantics, vmem_limit_bytes=VMEM_LIMIT)


def _dot(a, b, precision=None):
    return jnp.dot(a, b, preferred_element_type=F32, precision=precision)


def _dot_nt(a, b, precision=None):
    return lax.dot_general(a, b, (((1,), (1,)), ((), ())), preferred_element_type=F32, precision=precision)


def _dot_tn(a, b, precision=None):
    return lax.dot_general(a, b, (((0,), (0,)), ((), ())), preferred_element_type=F32, precision=precision)


def _split3(a):
    a1 = a.astype(BF16)
    r1 = a - a1.astype(F32)
    a2 = r1.astype(BF16)
    a3 = (r1 - a2.astype(F32)).astype(BF16)
    return a1, a2, a3


def _split2(a):
    hi = a.astype(BF16)
    return hi, (a - hi.astype(F32)).astype(BF16)


def _mm3(dot, a, b):
    return dot(a[0], b[0]) + (dot(a[0], b[1]) + dot(a[1], b[0]))


def _dot_exact_rhs(a, m_bf16):
    a1, a2, a3 = _split3(a)
    return _dot(a1, m_bf16) + _dot(a2, m_bf16) + _dot(a3, m_bf16)


def _dot_exact_lhs(m_bf16, a):
    a1, a2, a3 = _split3(a)
    return _dot(m_bf16, a1) + _dot(m_bf16, a2) + _dot(m_bf16, a3)


def _layernorm(x, g, b):
    xc = x - jnp.mean(x, axis=-1, keepdims=True)
    var = jnp.mean(xc * xc, axis=-1, keepdims=True)
    return xc * lax.rsqrt(var + LN_EPS) * g + b


def _softplus(x):
    return jnp.maximum(x, 0.0) + jnp.log1p(jnp.exp(-jnp.abs(x)))


def _sigmoid(x):
    return 1.0 / (1.0 + jnp.exp(-x))


def _full(shape):
    return pl.BlockSpec(shape, lambda *_: (0,) * len(shape))


def _in_proj_body(x_ref, g_ref, b_ref, wr_ref, wqkv_ref, wf_ref, bf_ref,
                  h_ref, pr_ref, q_ref, k_ref, v_ref, lf_ref):
    h = _layernorm(x_ref[...], g_ref[...], b_ref[...])
    h_ref[...] = h
    hb = h.astype(BF16)
    pr_ref[...] = _dot(hb, wr_ref[...])
    qkv = _dot(hb, wqkv_ref[...])
    q_ref[...] = qkv[:, :D_HEADS]
    k_ref[...] = qkv[:, D_HEADS:2 * D_HEADS]
    v_ref[...] = qkv[:, 2 * D_HEADS:]
    fl = _dot(hb, wf_ref[...]) + bf_ref[...]
    lf = jnp.minimum(fl, 0.0) - jnp.log1p(jnp.exp(-jnp.abs(fl)))
    lf_ref[...] = lf[:, :N_HEADS]


def _in_proj(x, g, b, wr, wqkv, wf, bfp):
    n = x.shape[0]
    tb = TOKEN_BLOCK
    rows = lambda w: pl.BlockSpec((tb, w), lambda i: (i, 0))
    return pl.pallas_call(
        _in_proj_body,
        grid=(pl.cdiv(n, tb),),
        in_specs=[rows(D_MODEL), _full((1, D_MODEL)), _full((1, D_MODEL)), _full(wr.shape),
                  _full(wqkv.shape), _full(wf.shape), _full(bfp.shape)],
        out_specs=[rows(D_MODEL), rows(RWKV_IN), rows(D_HEADS), rows(D_HEADS), rows(D_HEADS), rows(N_HEADS)],
        out_shape=[jax.ShapeDtypeStruct((n, w), F32)
                   for w in (D_MODEL, RWKV_IN, D_HEADS, D_HEADS, D_HEADS, N_HEADS)],
        compiler_params=_params(("parallel",)),
        name="in_proj",
    )(x, g, b, wr, wqkv, wf, bfp)


def _rwkv_body(seq_len, n_chunks, pr_ref, shift0_ref, s0_ref, mix_ref, w0_ref, w2_ref, a0_ref, a2_ref,
               g2_ref, kk_ref, ka_ref, rk_ref, lng_ref, lnb_ref, seg_ref, tri_ref,
               y_ref, sout_ref, shift_sc, state_sc):
    c = pl.program_id(1)

    @pl.when(c == 0)
    def _():
        shift_sc[...] = shift0_ref[0]
        state_sc[...] = s0_ref[0]

    row = lax.broadcasted_iota(I32, (CHUNK, 1), 0)
    valid = (c * CHUNK + row) < seq_len
    prf = jnp.where(valid, pr_ref[0], 0.0)
    prev = jnp.where(row == 0, shift_sc[...], pltpu.roll(prf, 1, 0))
    shift_sc[...] = prf[CHUNK - 1:CHUNK, :]
    xm = prf + (prev - prf) * mix_ref[...]

    r = xm[:, :D_HEADS]
    k = xm[:, D_HEADS:2 * D_HEADS]
    v = xm[:, 2 * D_HEADS:3 * D_HEADS]
    o = 3 * D_HEADS
    xw = xm[:, o:o + LORA_W]
    xa = xm[:, o + LORA_W:o + LORA_W + LORA_A]
    xg = xm[:, o + LORA_W + LORA_A:]

    seg = seg_ref[...]
    wlog = -_softplus(-(w0_ref[...] + _dot(jnp.tanh(xw), w2_ref[...], HIGHEST))) - 0.5
    lw = jnp.where(valid, -jnp.exp(wlog), 0.0)
    a = _sigmoid(a0_ref[...] + _dot(xa, a2_ref[...], HIGHEST))
    g = _dot(_sigmoid(xg), g2_ref[...], HIGHEST)

    kk = k * kk_ref[...]
    kk_norm = jnp.sqrt(_dot_exact_rhs(kk * kk, seg))
    kk = jnp.where(valid, kk / jnp.maximum(kk_norm, 1e-12), 0.0)
    k = jnp.where(valid, k * (1.0 + (a - 1.0) * ka_ref[...]), 0.0)
    v = jnp.where(valid, v, 0.0)

    cl = _dot_exact_lhs(tri_ref[...], lw)
    dec_in = jnp.exp(cl)
    dec_out = jnp.exp(-cl)
    a_t = -kk * jnp.exp(cl - lw)
    b_t = kk * a * dec_out
    k_t = k * dec_out
    r_t = r * dec_in
    gamma = jnp.exp(cl[CHUNK - 1:CHUNK, :])

    ti = lax.broadcasted_iota(I32, (CHUNK, 2 * CHUNK), 0)
    si = lax.broadcasted_iota(I32, (CHUNK, 2 * CHUNK), 1)
    si = jnp.where(si >= CHUNK, si - CHUNK, si)
    strict = si < ti
    incl = si <= ti
    eye = (lax.broadcasted_iota(I32, (CHUNK, CHUNK), 0) == lax.broadcasted_iota(I32, (CHUNK, CHUNK), 1)).astype(F32)
    zeros = jnp.zeros((CHUNK, HEAD), F32)

    heads = range(N_HEADS)
    sls = [slice(h * HEAD, (h + 1) * HEAD) for h in heads]
    vh = [v[:, sl] for sl in sls]
    ar = [_split2(jnp.concatenate([a_t[:, sl], r_t[:, sl]], axis=0)) for sl in sls]
    bk = [_split2(jnp.concatenate([b_t[:, sl], k_t[:, sl]], axis=0)) for sl in sls]
    s0 = [state_sc[h] for h in heads]
    gram = [_mm3(_dot_nt, ar[h], bk[h]) for h in heads]
    top = [jnp.where(strict, gram[h][:CHUNK], 0.0) for h in heads]
    bot = [jnp.where(incl, gram[h][CHUNK:], 0.0) for h in heads]
    hs = [_mm3(_dot_nt, ar[h], _split2(s0[h])) for h in heads]
    rhs = [hs[h][:CHUNK] + _mm3(_dot, _split2(top[h]), _split2(jnp.concatenate([zeros, vh[h]], axis=0)))
           for h in heads]
    inv = [eye + top[h][:, :CHUNK] for h in heads]
    pws = [_split2(top[h][:, :CHUNK]) for h in heads]
    for _ in range(5):
        pws = [_split2(_mm3(_dot, pws[h], pws[h])) for h in heads]
        inv = [inv[h] + _mm3(_dot, _split2(inv[h]), pws[h]) for h in heads]
    u = [_mm3(_dot, _split2(inv[h]), _split2(rhs[h])) for h in heads]
    uv = [_split2(jnp.concatenate([u[h], vh[h]], axis=0)) for h in heads]
    ys = [hs[h][CHUNK:] + _dot(bot[h].astype(BF16), uv[h][0]) for h in heads]
    for h in heads:
        state_sc[h] = (s0[h] + _mm3(_dot_tn, uv[h], bk[h])) * gamma[:, sls[h]]

    y = jnp.concatenate(ys, axis=1)
    yc = y - _dot_exact_rhs(y, seg) * (1.0 / HEAD)
    var = _dot_exact_rhs(yc * yc, seg) * (1.0 / HEAD)
    y = yc * lax.rsqrt(var + LNX_EPS) * lng_ref[...] + lnb_ref[...]
    r = xm[:, :D_HEADS]
    bonus = _dot_exact_rhs(r * k * rk_ref[...], seg) * v
    y_ref[0] = (y + bonus) * g

    @pl.when(c == n_chunks - 1)
    def _():
        sout_ref[0] = state_sc[...]


def _rwkv(pr, shift0, wkv0, p):
    bsz, seq_len, _ = pr.shape
    n_chunks = pl.cdiv(seq_len, CHUNK)
    lane = jnp.arange(D_HEADS) // HEAD
    seg = (lane[:, None] == lane[None, :]).astype(BF16)
    tri = (jnp.arange(CHUNK)[None, :] <= jnp.arange(CHUNK)[:, None]).astype(BF16)
    vec = lambda a: a.reshape(1, -1)
    consts = [vec(p["rwkv_mix"]), vec(p["rwkv_w0"]), p["rwkv_w2"], vec(p["rwkv_a0"]), p["rwkv_a2"], p["rwkv_g2"],
              vec(p["rwkv_k_k"]), vec(p["rwkv_k_a"]), vec(p["rwkv_r_k"]), vec(p["rwkv_lnx_g"]),
              vec(p["rwkv_lnx_b"]), seg, tri]
    return pl.pallas_call(
        functools.partial(_rwkv_body, seq_len, n_chunks),
        grid=(bsz, n_chunks),
        in_specs=[pl.BlockSpec((1, CHUNK, RWKV_IN), lambda b, c: (b, c, 0)),
                  pl.BlockSpec((1, 1, RWKV_IN), lambda b, c: (b, 0, 0)),
                  pl.BlockSpec((1, N_HEADS, HEAD, HEAD), lambda b, c: (b, 0, 0, 0))]
                 + [_full(a.shape) for a in consts],
        out_specs=[pl.BlockSpec((1, CHUNK, D_HEADS), lambda b, c: (b, c, 0)),
                   pl.BlockSpec((1, N_HEADS, HEAD, HEAD), lambda b, c: (b, 0, 0, 0))],
        out_shape=[jax.ShapeDtypeStruct((bsz, seq_len, D_HEADS), F32),
                   jax.ShapeDtypeStruct((bsz, N_HEADS, HEAD, HEAD), F32)],
        scratch_shapes=[pltpu.VMEM((1, RWKV_IN), F32), pltpu.VMEM((N_HEADS, HEAD, HEAD), F32)],
        compiler_params=_params(("parallel", "arbitrary")),
        name="rwkv",
    )(pr, shift0, wkv0, *consts)


def _cumsum_body(seq_len, lf_ref, col_ref, row_ref):
    carry = jnp.zeros((1, N_HEADS), F32)
    eye = (lax.broadcasted_iota(I32, (N_HEADS, N_HEADS), 0)
           == lax.broadcasted_iota(I32, (N_HEADS, N_HEADS), 1)).astype(BF16)
    for start in range(0, seq_len, LANES):
        n = min(LANES, seq_len - start)
        tri = (lax.broadcasted_iota(I32, (n, n), 1) <= lax.broadcasted_iota(I32, (n, n), 0)).astype(BF16)
        cum = _dot_exact_lhs(tri, lf_ref[0, start:start + n, :]) + carry
        col_ref[0, start:start + n, :] = cum
        c1, c2, c3 = _split3(cum)
        row_ref[0, :, start:start + n] = _dot_nt(eye, c1) + _dot_nt(eye, c2) + _dot_nt(eye, c3)
        carry = cum[n - 1:n, :]


def _cumsum(lf):
    bsz, seq_len, _ = lf.shape
    return pl.pallas_call(
        functools.partial(_cumsum_body, seq_len),
        grid=(bsz,),
        in_specs=[pl.BlockSpec((1, seq_len, N_HEADS), lambda b: (b, 0, 0))],
        out_specs=[pl.BlockSpec((1, seq_len, N_HEADS), lambda b: (b, 0, 0)),
                   pl.BlockSpec((1, N_HEADS, seq_len), lambda b: (b, 0, 0))],
        out_shape=[jax.ShapeDtypeStruct((bsz, seq_len, N_HEADS), F32),
                   jax.ShapeDtypeStruct((bsz, N_HEADS, seq_len), F32)],
        compiler_params=_params(("parallel",)),
        name="cumsum",
    )(lf)


def _fox_body(kv_len, q_off, tq, tk, n_kv, n_past, q_ref, k_ref, v_ref, *rest):
    if n_past is None:
        cq_ref, ck_ref, g_ref, o_ref, m_sc, l_sc, acc_sc = rest
    else:
        k_new_ref, v_new_ref, cq_ref, ck_ref, g_ref, o_ref, m_sc, l_sc, acc_sc = rest
    qi = pl.program_id(1)
    ki = pl.program_id(2)

    @pl.when(ki == 0)
    def _():
        m_sc[...] = jnp.full(m_sc.shape, -jnp.inf, F32)
        l_sc[...] = jnp.zeros(l_sc.shape, F32)
        acc_sc[...] = jnp.zeros(acc_sc.shape, F32)

    q_start = q_off + qi * tq

    def step(k_blk, v_blk, k_start, n_rows):
        rows = q_start + lax.broadcasted_iota(I32, (tq, 1), 0)
        cols = k_start + lax.broadcasted_iota(I32, (1, n_rows), 1)
        mask = cols <= rows
        k_valid = (k_start + lax.broadcasted_iota(I32, (n_rows, 1), 0)) < kv_len
        q = q_ref[0] * (HEAD ** -0.5)
        kb = k_blk.astype(BF16)
        vb = jnp.where(k_valid, v_blk, 0.0).astype(BF16)
        cq = cq_ref[0]
        ck = ck_ref[0][:, :n_rows]
        ones = jnp.ones((n_rows, LANES), BF16)
        heads = range(N_HEADS)
        sls = [slice(h * HEAD, (h + 1) * HEAD) for h in heads]
        qb = q.astype(BF16)
        s = [_dot_nt(qb[:, sl], kb[:, sl]) for sl in sls]
        s = [jnp.where(mask, s[h] + (cq[:, h:h + 1] - ck[h:h + 1, :]), -jnp.inf) for h in heads]
        m_prev = [m_sc[h] for h in heads]
        m_new = [jnp.maximum(m_prev[h], jnp.max(s[h], axis=-1, keepdims=True)) for h in heads]
        scale = [jnp.exp(m_prev[h] - m_new[h]) for h in heads]
        pexp = [jnp.exp(s[h] - m_new[h]).astype(BF16) for h in heads]
        psum = [_dot(pexp[h], ones)[:, :1] for h in heads]
        pv = [_dot(pexp[h], vb[:, sls[h]]) for h in heads]
        for h in heads:
            l_sc[h] = scale[h] * l_sc[h] + psum[h]
            acc_sc[:, sls[h]] = scale[h] * acc_sc[:, sls[h]] + pv[h]
            m_sc[h] = m_new[h]

    if n_past is None:
        @pl.when(ki * tk <= q_start + tq - 1)
        def _():
            step(k_ref[0], v_ref[0], ki * tk, tk)
    else:
        @pl.when(ki < n_past)
        def _():
            step(k_ref[0], v_ref[0], ki * tk, tk)

        @pl.when(ki == n_past)
        def _():
            step(k_new_ref[0], v_new_ref[0], q_off, tq)

    @pl.when(ki == n_kv - 1)
    def _():
        for h in range(N_HEADS):
            sl = slice(h * HEAD, (h + 1) * HEAD)
            o = acc_sc[:, sl] / l_sc[h]
            o = o * lax.rsqrt(jnp.mean(o * o, axis=-1, keepdims=True) + 1e-6)
            acc_sc[:, sl] = o
        o_ref[0] = acc_sc[...] * g_ref[...]


def _fox(q, k, v, c_col, c_row, norm_g, tq, tk, past=None):
    bsz, q_len, _ = q.shape
    n_q = pl.cdiv(q_len, tq)
    if past is None:
        q_off, n_past, kv_len = 0, None, k.shape[1]
        n_kv = pl.cdiv(kv_len, tk)
        kv_operands, extra_specs = (k, v), []
    else:
        q_off = past[0].shape[1]
        assert q_off % tk == 0 and q_off % tq == 0 and q_len == tq == k.shape[1]
        n_past, kv_len = q_off // tk, q_off + q_len
        n_kv = n_past + 1
        kv_operands = past + (k, v)
        extra_specs = [pl.BlockSpec((1, tq, D_HEADS), lambda b, qi, ki: (b, 0, 0))] * 2
    last_kv = lambda qi: (q_off + (qi + 1) * tq - 1) // tk if n_past is None else n_past - 1
    kv_map = lambda b, qi, ki: (b, jnp.minimum(ki, last_kv(qi)), 0)
    ck_last = lambda qi: last_kv(qi) if n_past is None else n_past
    return pl.pallas_call(
        functools.partial(_fox_body, kv_len, q_off, tq, tk, n_kv, n_past),
        grid=(bsz, n_q, n_kv),
        in_specs=[pl.BlockSpec((1, tq, D_HEADS), lambda b, qi, ki: (b, qi, 0)),
                  pl.BlockSpec((1, tk, D_HEADS), kv_map),
                  pl.BlockSpec((1, tk, D_HEADS), kv_map)]
                 + extra_specs
                 + [pl.BlockSpec((1, tq, N_HEADS), lambda b, qi, ki: (b, q_off // tq + qi, 0)),
                    pl.BlockSpec((1, N_HEADS, tk), lambda b, qi, ki: (b, 0, jnp.minimum(ki, ck_last(qi)))),
                    _full((1, D_HEADS))],
        out_specs=pl.BlockSpec((1, tq, D_HEADS), lambda b, qi, ki: (b, qi, 0)),
        out_shape=jax.ShapeDtypeStruct((bsz, q_len, D_HEADS), F32),
        scratch_shapes=[pltpu.VMEM((N_HEADS, tq, 1), F32), pltpu.VMEM((N_HEADS, tq, 1), F32),
                        pltpu.VMEM((tq, D_HEADS), F32)],
        compiler_params=_params(("parallel", "parallel", "arbitrary")),
        name="fox",
    )(q, *kv_operands, c_col, c_row, norm_g)


def _out_proj_body(yr_ref, yf_ref, h_ref, wr_ref, wf_ref, g_ref, b_ref, o_ref):
    mix = _dot(yr_ref[...].astype(BF16), wr_ref[...]) + _dot(yf_ref[...].astype(BF16), wf_ref[...])
    o_ref[...] = _layernorm(ALPHA * h_ref[...] + mix, g_ref[...], b_ref[...])


def _out_proj(yr, yf, h, wr, wf, g, b):
    n = h.shape[0]
    tb = TOKEN_BLOCK
    rows = lambda w: pl.BlockSpec((tb, w), lambda i: (i, 0))
    return pl.pallas_call(
        _out_proj_body,
        grid=(pl.cdiv(n, tb),),
        in_specs=[rows(D_HEADS), rows(D_HEADS), rows(D_MODEL), _full(wr.shape), _full(wf.shape),
                  _full((1, D_MODEL)), _full((1, D_MODEL))],
        out_specs=rows(D_MODEL),
        out_shape=jax.ShapeDtypeStruct((n, D_MODEL), F32),
        compiler_params=_params(("parallel",)),
        name="out_proj",
    )(yr, yf, h, wr, wf, g, b)


_BIG = 2 ** 30
_CAND_COUNTS = tuple(PEER_TOPK // (i + 1) for i in range(PEER_TOPK))
_N_CAND = sum(_CAND_COUNTS)
_CAND_ROWS = -(-_N_CAND // 8) * 8


def _top_rows(s, order, k, payload=None):
    vals, outs = [], []
    for _ in range(k):
        m = jnp.max(s, axis=0, keepdims=True)
        first = jnp.min(jnp.where(s == m, order, _BIG), axis=0, keepdims=True)
        sel = order == first
        vals.append(m)
        outs.append(first if payload is None else jnp.max(jnp.where(sel, payload, -1), axis=0, keepdims=True))
        s = jnp.where(sel, -jnp.inf, s)
    return jnp.concatenate(vals, axis=0), jnp.concatenate(outs, axis=0)


def _topk_body(h_ref, wq_ref, k1_ref, k2_ref, pos_ref, idx_ref, gate_ref, cand_s_sc, cand_i_sc):
    tb = h_ref.shape[0]
    q = _dot(h_ref[...].astype(BF16), wq_ref[...]).astype(BF16)
    key_id = lax.broadcasted_iota(I32, (N_KEYS, LANES), 0)
    cand_pos = pos_ref[...]
    k1 = k1_ref[...].astype(BF16)
    k2 = k2_ref[...].astype(BF16)
    cand_s_sc[_N_CAND:, :] = jnp.full((_CAND_ROWS - _N_CAND, LANES), -jnp.inf, F32)
    cand_i_sc[_N_CAND:, :] = jnp.zeros((_CAND_ROWS - _N_CAND, LANES), I32)
    for part in range(tb // LANES):
        qp = q[part * LANES:(part + 1) * LANES]
        ids, gates = [], []
        for h in range(PEER_HEADS):
            o = h * 2 * D_KEY_HALF
            s1 = _dot_nt(k1, qp[:, o:o + D_KEY_HALF])
            s2 = _dot_nt(k2, qp[:, o + D_KEY_HALF:o + 2 * D_KEY_HALF])
            v1, i1 = _top_rows(s1, key_id, PEER_TOPK)
            v2, i2 = _top_rows(s2, key_id, PEER_TOPK)
            off = 0
            for i, cnt in enumerate(_CAND_COUNTS):
                cand_s_sc[off:off + cnt, :] = v1[i:i + 1] + v2[:cnt]
                cand_i_sc[off:off + cnt, :] = i1[i:i + 1] * N_KEYS + i2[:cnt]
                off += cnt
            best_s, best_i = _top_rows(cand_s_sc[...], cand_pos, PEER_TOPK, payload=cand_i_sc[...])
            e = jnp.exp(best_s - best_s[0:1])
            ids.append(best_i)
            gates.append(e / jnp.sum(e, axis=0, keepdims=True))
        rows = slice(part * LANES, (part + 1) * LANES)
        idx_ref[rows, :] = jnp.concatenate(ids, axis=0).T
        gate_ref[rows, :] = jnp.concatenate(gates, axis=0).T


def _topk(h1, wq, keys1, keys2):
    n = h1.shape[0]
    tb = TOKEN_BLOCK
    pos = [i * PEER_TOPK + j for i, cnt in enumerate(_CAND_COUNTS) for j in range(cnt)]
    pos = jnp.asarray(pos + [_BIG] * (_CAND_ROWS - _N_CAND), I32)
    pos = jnp.broadcast_to(pos[:, None], (_CAND_ROWS, LANES))
    return pl.pallas_call(
        _topk_body,
        grid=(pl.cdiv(n, tb),),
        in_specs=[pl.BlockSpec((tb, D_MODEL), lambda i: (i, 0)), _full(wq.shape), _full(keys1.shape),
                  _full(keys2.shape), _full(pos.shape)],
        out_specs=[pl.BlockSpec((tb, PEER_SLOTS), lambda i: (i, 0)), pl.BlockSpec((tb, PEER_SLOTS), lambda i: (i, 0))],
        out_shape=[jax.ShapeDtypeStruct((n, PEER_SLOTS), I32), jax.ShapeDtypeStruct((n, PEER_SLOTS), F32)],
        scratch_shapes=[pltpu.VMEM((_CAND_ROWS, LANES), F32), pltpu.VMEM((_CAND_ROWS, LANES), I32)],
        compiler_params=_params(("parallel",)),
        name="peer_topk",
    )(h1, wq, keys1, keys2, pos)


def _unpack_pair(w):
    return lax.bitcast_convert_type(w << 16, F32), lax.bitcast_convert_type(w & jnp.int32(-65536), F32)


_ERF_NUM = (-2.72614225801306e-10, 2.77068142495902e-08, -2.10102402082508e-06, -5.69250639462346e-05,
            -7.34990630326855e-04, -2.95459980854025e-03, -1.60960333262415e-02)
_ERF_DEN = (-1.45660718464996e-05, -2.13374055278905e-04, -1.68282697438203e-03, -7.37332916720468e-03,
            -1.42647390514189e-02)


def _erf_f32(x):
    x = jnp.minimum(jnp.maximum(x, -4.0), 4.0)
    x2 = x * x
    num = x2 * _ERF_NUM[0] + _ERF_NUM[1]
    for c in _ERF_NUM[2:]:
        num = num * x2 + c
    den = x2 * _ERF_DEN[0] + _ERF_DEN[1]
    for c in _ERF_DEN[2:]:
        den = den * x2 + c
    return x * num / den


def _expert_mix(u_tab, v_tab, idx, x, gate):
    n_tok, words = x.shape[0], u_tab.shape[1]
    lanes = SC_LANES
    pass_chunks = words // lanes // 2
    n_workers = SC_CORES * SC_SUBCORES
    assert n_tok % SC_UNIT == 0
    base, extra = divmod(n_tok // SC_UNIT, n_workers)
    idx_rows = 2 * SC_UNIT
    mesh = plsc.VectorSubcoreMesh(core_axis_name="c", subcore_axis_name="s",
                                  num_cores=SC_CORES, num_subcores=SC_SUBCORES)

    @functools.partial(
        pl.kernel, mesh=mesh, compiler_params=pltpu.CompilerParams(needs_layout_passes=False),
        out_type=jax.ShapeDtypeStruct((n_tok, 2 * words), F32),
        scratch_types=[pltpu.VMEM((2 * idx_rows, SC_HALF), I32),
                       pltpu.VMEM((SC_HALF, words), I32), pltpu.VMEM((SC_HALF, words), I32),
                       pltpu.VMEM((2 * SC_UNIT, 2 * words), F32),
                       pltpu.VMEM((2 * SC_UNIT, PEER_SLOTS), F32),
                       pltpu.VMEM((SC_UNIT, PEER_SLOTS), F32),
                       pltpu.VMEM((2 * SC_UNIT, 2 * words), F32),
                       pltpu.SemaphoreType.DMA, pltpu.SemaphoreType.DMA, pltpu.SemaphoreType.DMA,
                       pltpu.SemaphoreType.DMA((2,))],
    )
    def sc_kernel(u_hbm, v_hbm, idx_hbm, x_hbm, gate_hbm, out_hbm, idx_v, buf_a, buf_b, x_v, gate_v, act_v, out_v,
                  sem_a, sem_b, sem_in, sem_out):
        worker = lax.axis_index("s") * SC_CORES + lax.axis_index("c")
        first_unit = worker * base + jnp.minimum(worker, extra)
        my_units = base + (worker < extra).astype(I32)
        lane = lax.iota(I32, lanes)

        def first_token(ui):
            return pl.multiple_of((first_unit + ui) * SC_UNIT, SC_UNIT)

        def stage_in(ui, slot):
            tok0 = first_token(ui)
            toks = pl.ds(tok0, SC_UNIT)
            dst = pl.ds(slot * SC_UNIT, SC_UNIT)
            rows = pl.ds(pl.multiple_of(tok0 * 2, idx_rows), idx_rows)
            return (pltpu.make_async_copy(idx_hbm.at[rows], idx_v.at[pl.ds(slot * idx_rows, idx_rows)], sem_in),
                    pltpu.make_async_copy(x_hbm.at[toks], x_v.at[dst], sem_in),
                    pltpu.make_async_copy(gate_hbm.at[toks], gate_v.at[dst], sem_in))

        def write_out(ui, slot):
            return pltpu.make_async_copy(out_v.at[pl.ds(slot * SC_UNIT, SC_UNIT)],
                                         out_hbm.at[pl.ds(first_token(ui), SC_UNIT)], sem_out.at[slot])

        def gather(table_hbm, row, buf, sem):
            return pltpu.make_async_copy(table_hbm.at[idx_v.at[row]], buf, sem)

        def dot(buf, t, xrow, half):
            @pl.loop(0, SC_HALF // lanes)
            def _(g):
                def chunk(j, acc):
                    x_lo = x_v[xrow, pl.ds(j * lanes, lanes)]
                    x_hi = x_v[xrow, pl.ds(words + j * lanes, lanes)]
                    out = []
                    for r in range(lanes):
                        lo, hi = _unpack_pair(buf[g * lanes + r, pl.ds(j * lanes, lanes)])
                        out.append(acc[r] + lo * x_lo + hi * x_hi)
                    return tuple(out)

                acc = lax.fori_loop(0, words // lanes, chunk, tuple(jnp.zeros((lanes,), F32) for _ in range(lanes)))
                res = jnp.zeros((lanes,), F32)
                for r in range(lanes):
                    res = jnp.where(lane == r, jnp.sum(acc[r]), res)
                act_v[t, pl.ds(half * SC_HALF + g * lanes, lanes)] = res

        def gate_gelu(t, xrow):
            for g in range(PEER_SLOTS // lanes):
                cols = pl.ds(g * lanes, lanes)
                a = act_v[t, cols]
                act_v[t, cols] = gate_v[xrow, cols] * (a * 0.5 * (1.0 + _erf_f32(a * (2.0 ** -0.5))))

        def vsum(buf, t, orow, half):
            for p in range(2):
                cols = [(p * pass_chunks + jj) * lanes for jj in range(pass_chunks)]
                if half == 0:
                    init = tuple(jnp.zeros((lanes,), F32) for _ in range(2 * pass_chunks))
                else:
                    init = (tuple(out_v[orow, pl.ds(c, lanes)] for c in cols)
                            + tuple(out_v[orow, pl.ds(words + c, lanes)] for c in cols))

                def row(r, acc):
                    slot = jnp.full((lanes,), half * SC_HALF, I32) + r
                    c = plsc.load_gather(act_v, [jnp.full((lanes,), 0, I32) + t, slot])
                    lo_acc, hi_acc = list(acc[:pass_chunks]), list(acc[pass_chunks:])
                    for jj in range(pass_chunks):
                        lo, hi = _unpack_pair(buf[r, pl.ds(cols[jj], lanes)])
                        lo_acc[jj] = lo_acc[jj] + lo * c
                        hi_acc[jj] = hi_acc[jj] + hi * c
                    return tuple(lo_acc) + tuple(hi_acc)

                acc = lax.fori_loop(0, SC_HALF, row, init)
                for jj in range(pass_chunks):
                    out_v[orow, pl.ds(cols[jj], lanes)] = acc[jj]
                    out_v[orow, pl.ds(words + cols[jj], lanes)] = acc[pass_chunks + jj]

        def unit_body(ui, carry):
            slot = lax.rem(ui, 2)
            has_next = ui + 1 < my_units

            @pl.when(has_next)
            def _():
                for c in stage_in(ui + 1, 1 - slot):
                    c.start()

            @pl.when(ui >= 2)
            def _():
                write_out(ui - 2, slot).wait()

            @pl.loop(0, SC_UNIT)
            def _(t):
                r0 = slot * idx_rows + 2 * t
                urow = slot * SC_UNIT + t
                gather(u_hbm, r0 + 1, buf_b, sem_b).start()
                gather(u_hbm, r0, buf_a, sem_a).wait()
                dot(buf_a, t, urow, 0)
                gather(v_hbm, r0, buf_a, sem_a).start()
                gather(u_hbm, r0 + 1, buf_b, sem_b).wait()
                dot(buf_b, t, urow, 1)
                gate_gelu(t, urow)
                gather(v_hbm, r0 + 1, buf_b, sem_b).start()
                gather(v_hbm, r0, buf_a, sem_a).wait()
                vsum(buf_a, t, urow, 0)

                @pl.when(t + 1 < SC_UNIT)
                def _():
                    gather(u_hbm, r0 + 2, buf_a, sem_a).start()

                @pl.when((t + 1 == SC_UNIT) & has_next)
                def _():
                    for c in stage_in(ui + 1, 1 - slot):
                        c.wait()
                    gather(u_hbm, (1 - slot) * idx_rows, buf_a, sem_a).start()

                gather(v_hbm, r0 + 1, buf_b, sem_b).wait()
                vsum(buf_b, t, urow, 1)

            write_out(ui, slot).start()
            return carry

        @pl.when(my_units > 0)
        def _():
            for c in stage_in(0, 0):
                c.start()
            for c in stage_in(0, 0):
                c.wait()
            gather(u_hbm, 0, buf_a, sem_a).start()
            lax.fori_loop(0, my_units, unit_body, 0)

            @pl.when(my_units >= 2)
            def _():
                write_out(my_units - 2, lax.rem(my_units, 2)).wait()

            write_out(my_units - 1, lax.rem(my_units - 1, 2)).wait()

    return sc_kernel(u_tab, v_tab, idx.reshape(n_tok * 2, SC_HALF), x, gate)


def _final_body(x_ref, mix_ref, g_ref, b_ref, o_ref):
    o_ref[...] = _layernorm(ALPHA * x_ref[...] + mix_ref[...], g_ref[...], b_ref[...])


def _rowwise(body, name, n, widths_in, width_out, *args):
    tb = TOKEN_BLOCK
    rows = lambda w: pl.BlockSpec((tb, w), lambda i: (i, 0))
    n_rows_args = len(widths_in)
    return pl.pallas_call(
        body,
        grid=(pl.cdiv(n, tb),),
        in_specs=[rows(w) for w in widths_in] + [_full(a.shape) for a in args[n_rows_args:]],
        out_specs=rows(width_out),
        out_shape=jax.ShapeDtypeStruct((n, width_out), F32),
        compiler_params=_params(("parallel",)),
        name=name,
    )(*args)


def _group(x, shift0, wkv0, past, p):
    bsz, seq_len, _ = x.shape
    n = bsz * seq_len
    h, pr, q, k, v, lf = _in_proj(x.reshape(n, D_MODEL), p["ln_in_g"], p["ln_in_b"], p["w_r"], p["w_qkv"],
                                  p["w_f"], p["b_f"])
    pr = pr.reshape(bsz, seq_len, RWKV_IN)
    y_r, wkv_new = _rwkv(pr, shift0, wkv0, p)

    seq3 = lambda a: a.reshape(bsz, seq_len, -1)
    k3, v3, lf3 = seq3(k), seq3(v), seq3(lf)
    if past is None:
        kv_past, lf_all, tq = None, lf3, FOX_BLOCK
    else:
        k_past, v_past, lf_past = past
        flat = lambda a: a.reshape(bsz, a.shape[1], D_HEADS)
        kv_past, lf_all, tq = (flat(k_past), flat(v_past)), jnp.concatenate([lf_past, lf3], axis=1), seq_len
    c_col, c_row = _cumsum(lf_all)
    y_f = _fox(seq3(q), k3, v3, c_col, c_row, p["fox_norm_g"], tq, FOX_KV_BLOCK, kv_past)

    h1 = _out_proj(y_r.reshape(n, D_HEADS), y_f.reshape(n, D_HEADS), h, p["w_out_r"], p["w_out_f"],
                   p["ln1_g"], p["ln1_b"])
    idx, gate = _topk(h1, p["peer_w_q"], p["peer_keys1"], p["peer_keys2"])
    mix = _expert_mix(p["peer_u"], p["peer_v"], idx, h1, gate)
    y = _rowwise(_final_body, "final_ln", n, (D_MODEL, D_MODEL), D_MODEL, h1, mix, p["ln2_g"], p["ln2_b"])

    heads = lambda a: a.reshape(1, bsz, seq_len, N_HEADS, HEAD)
    return (y.reshape(bsz, seq_len, D_MODEL), pr[None, :, seq_len - 1:, :], wkv_new[None],
            heads(k), heads(v), lf3[None])


def _pack_halves(a):
    bits = lax.bitcast_convert_type(a.astype(BF16), jnp.uint16).astype(jnp.uint32)
    w = a.shape[1] // 2
    return lax.bitcast_convert_type(bits[:, :w] | (bits[:, w:] << 16), I32)


def kernel(x_prompt, x_sample, state_rwkv_shift, state_rwkv_wkv, cache_fox_k, cache_fox_v, cache_fox_logf,
           meta_tokens, ln_in_g, ln_in_b, w_in, rwkv_mix, rwkv_w0, rwkv_w2, rwkv_a0, rwkv_a2, rwkv_g2,
           rwkv_k_k, rwkv_k_a, rwkv_r_k, rwkv_lnx_g, rwkv_lnx_b, fox_b_f, fox_norm_g, w_out, ln1_g, ln1_b,
           peer_w_q, peer_keys1, peer_keys2, peer_u, peer_v, ln2_g, ln2_b):
    assert w_in.shape[0] == 1, "single-layer step"
    bsz = x_prompt.shape[0]
    row = lambda a: a.reshape(1, -1)
    w = w_in[0]
    fox0 = RWKV_IN
    pad_f = LANES - N_HEADS
    p = dict(
        ln_in_g=row(ln_in_g), ln_in_b=row(ln_in_b),
        w_r=w[:, :RWKV_IN].astype(BF16),
        w_qkv=w[:, fox0:fox0 + 3 * D_HEADS].astype(BF16),
        w_f=jnp.pad(w[:, fox0 + 3 * D_HEADS:], ((0, 0), (0, pad_f))).astype(BF16),
        b_f=jnp.pad(row(fox_b_f[0]), ((0, 0), (0, pad_f))),
        rwkv_mix=rwkv_mix[0], rwkv_w0=rwkv_w0[0], rwkv_w2=rwkv_w2[0], rwkv_a0=rwkv_a0[0], rwkv_a2=rwkv_a2[0],
        rwkv_g2=rwkv_g2[0], rwkv_k_k=rwkv_k_k[0], rwkv_k_a=rwkv_k_a[0], rwkv_r_k=rwkv_r_k[0],
        rwkv_lnx_g=rwkv_lnx_g[0], rwkv_lnx_b=rwkv_lnx_b[0],
        fox_norm_g=row(fox_norm_g[0]),
        w_out_r=w_out[0, :D_HEADS].astype(BF16), w_out_f=w_out[0, D_HEADS:].astype(BF16),
        ln1_g=row(ln1_g[0]), ln1_b=row(ln1_b[0]),
        peer_w_q=peer_w_q[0].astype(BF16), peer_keys1=peer_keys1[0], peer_keys2=peer_keys2[0],
        peer_u=_pack_halves(peer_u[0]), peer_v=_pack_halves(peer_v[0]),
        ln2_g=row(ln2_g[0]), ln2_b=row(ln2_b[0]),
    )
    sizes = PROMPT_SLICES if sum(PROMPT_SLICES) == bsz else (bsz,)
    parts = []
    start = 0
    for pb in sizes:
        meta = jnp.broadcast_to(meta_tokens[None], (pb, N_META, D_MODEL))
        hp0 = jnp.concatenate([meta, x_prompt[start:start + pb]], axis=1)
        yp, *state = _group(hp0, jnp.zeros((pb, 1, RWKV_IN), F32), jnp.zeros((pb, N_HEADS, HEAD, HEAD), F32), None, p)
        parts.append((yp[:, N_META:], *state))
        start += pb
    ys, s_shift, s_wkv, s_k, s_v, s_lf = _group(
        x_sample, state_rwkv_shift[0], state_rwkv_wkv[0],
        (cache_fox_k[0], cache_fox_v[0], cache_fox_logf[0]), p)
    yp = jnp.concatenate([t[0] for t in parts], axis=0)
    p_shift, p_wkv, p_k, p_v, p_lf = (jnp.concatenate([t[j] for t in parts], axis=1) for j in range(1, 6))
    return (yp, ys, p_shift, p_wkv, p_k, p_v, p_lf, s_shift, s_wkv, s_k, s_v, s_lf)
```

```python
import functools

import jax
import jax.numpy as jnp
from jax import lax
from jax.experimental import pallas as pl
from jax.experimental.pallas import tpu as pltpu
from jax.experimental.pallas import tpu_sc as plsc

F32 = jnp.float32
BF16 = jnp.bfloat16
I32 = jnp.int32

D_MODEL = 1024
N_HEADS = 8
HEAD = 64
D_HEADS = N_HEADS * HEAD
LORA_W, LORA_A, LORA_G = 64, 64, 128
RWKV_IN = 3 * D_HEADS + LORA_W + LORA_A + LORA_G
N_META = 16
LNX_EPS = 64e-5
LN_EPS = 1e-5
ALPHA = 2.0 ** 0.25
N_KEYS = 128
PEER_HEADS = 8
PEER_TOPK = 16
PEER_SLOTS = PEER_HEADS * PEER_TOPK
D_KEY_HALF = 128

LANES = 128
CHUNK = 64
TOKEN_BLOCK = 256
FOX_BLOCK = 256
FOX_KV_BLOCK = 512
SC_CORES, SC_SUBCORES, SC_LANES = 2, 16, 16
SC_HALF = PEER_SLOTS // 2
SC_UNIT = 8
TC_MIX_TOKENS = 8
PROMPT_SLICES = (1, 1, 2, 2, 2, 2, 2, 2, 2)
VMEM_LIMIT = 48 * 1024 * 1024

HIGHEST = lax.Precision.HIGHEST


def _params(semantics):
    return pltpu.CompilerParams(dimension_semantics=semantics, vmem_limit_bytes=VMEM_LIMIT)


def _dot(a, b, precision=None):
    return jnp.dot(a, b, preferred_element_type=F32, precision=precision)


def _dot_nt(a, b, precision=None):
    return lax.dot_general(a, b, (((1,), (1,)), ((), ())), preferred_element_type=F32, precision=precision)


def _dot_tn(a, b, precision=None):
    return lax.dot_general(a, b, (((0,), (0,)), ((), ())), preferred_element_type=F32, precision=precision)


def _split3(a):
    a1 = a.astype(BF16)
    r1 = a - a1.astype(F32)
    a2 = r1.astype(BF16)
    a3 = (r1 - a2.astype(F32)).astype(BF16)
    return a1, a2, a3


def _split2(a):
    hi = a.astype(BF16)
    return hi, (a - hi.astype(F32)).astype(BF16)


def _mm3(dot, a, b):
    return dot(a[0], b[0]) + (dot(a[0], b[1]) + dot(a[1], b[0]))


def _dot_exact_rhs(a, m_bf16):
    a1, a2, a3 = _split3(a)
    return _dot(a1, m_bf16) + _dot(a2, m_bf16) + _dot(a3, m_bf16)


def _dot_exact_lhs(m_bf16, a):
    a1, a2, a3 = _split3(a)
    return _dot(m_bf16, a1) + _dot(m_bf16, a2) + _dot(m_bf16, a3)


def _layernorm(x, g, b):
    xc = x - jnp.mean(x, axis=-1, keepdims=True)
    var = jnp.mean(xc * xc, axis=-1, keepdims=True)
    return xc * lax.rsqrt(var + LN_EPS) * g + b


def _softplus(x):
    return jnp.maximum(x, 0.0) + jnp.log1p(jnp.exp(-jnp.abs(x)))


def _sigmoid(x):
    return 1.0 / (1.0 + jnp.exp(-x))


def _full(shape):
    return pl.BlockSpec(shape, lambda *_: (0,) * len(shape))


def _in_proj_body(x_ref, g_ref, b_ref, wr_ref, wqkv_ref, wf_ref, bf_ref,
                  h_ref, pr_ref, q_ref, k_ref, v_ref, lf_ref):
    h = _layernorm(x_ref[...], g_ref[...], b_ref[...])
    h_ref[...] = h
    hb = h.astype(BF16)
    pr_ref[...] = _dot(hb, wr_ref[...])
    qkv = _dot(hb, wqkv_ref[...])
    q_ref[...] = qkv[:, :D_HEADS]
    k_ref[...] = qkv[:, D_HEADS:2 * D_HEADS]
    v_ref[...] = qkv[:, 2 * D_HEADS:]
    fl = _dot(hb, wf_ref[...]) + bf_ref[...]
    lf = jnp.minimum(fl, 0.0) - jnp.log1p(jnp.exp(-jnp.abs(fl)))
    lf_ref[...] = lf[:, :N_HEADS]


def _in_proj(x, g, b, wr, wqkv, wf, bfp):
    n = x.shape[0]
    tb = TOKEN_BLOCK
    rows = lambda w: pl.BlockSpec((tb, w), lambda i: (i, 0))
    return pl.pallas_call(
        _in_proj_body,
        grid=(pl.cdiv(n, tb),),
        in_specs=[rows(D_MODEL), _full((1, D_MODEL)), _full((1, D_MODEL)), _full(wr.shape),
                  _full(wqkv.shape), _full(wf.shape), _full(bfp.shape)],
        out_specs=[rows(D_MODEL), rows(RWKV_IN), rows(D_HEADS), rows(D_HEADS), rows(D_HEADS), rows(N_HEADS)],
        out_shape=[jax.ShapeDtypeStruct((n, w), F32)
                   for w in (D_MODEL, RWKV_IN, D_HEADS, D_HEADS, D_HEADS, N_HEADS)],
        compiler_params=_params(("parallel",)),
        name="in_proj",
    )(x, g, b, wr, wqkv, wf, bfp)


def _rwkv_body(seq_len, n_chunks, pr_ref, shift0_ref, s0_ref, mix_ref, w0_ref, w2_ref, a0_ref, a2_ref,
               g2_ref, kk_ref, ka_ref, rk_ref, lng_ref, lnb_ref, seg_ref, tri_ref,
               y_ref, sout_ref, shift_sc, state_sc):
    c = pl.program_id(1)

    @pl.when(c == 0)
    def _():
        shift_sc[...] = shift0_ref[0]
        state_sc[...] = s0_ref[0]

    row = lax.broadcasted_iota(I32, (CHUNK, 1), 0)
    valid = (c * CHUNK + row) < seq_len
    prf = jnp.where(valid, pr_ref[0], 0.0)
    prev = jnp.where(row == 0, shift_sc[...], pltpu.roll(prf, 1, 0))
    shift_sc[...] = prf[CHUNK - 1:CHUNK, :]
    xm = prf + (prev - prf) * mix_ref[...]

    r = xm[:, :D_HEADS]
    k = xm[:, D_HEADS:2 * D_HEADS]
    v = xm[:, 2 * D_HEADS:3 * D_HEADS]
    o = 3 * D_HEADS
    xw = xm[:, o:o + LORA_W]
    xa = xm[:, o + LORA_W:o + LORA_W + LORA_A]
    xg = xm[:, o + LORA_W + LORA_A:]

    seg = seg_ref[...]
    wlog = -_softplus(-(w0_ref[...] + _dot(jnp.tanh(xw), w2_ref[...], HIGHEST))) - 0.5
    lw = jnp.where(valid, -jnp.exp(wlog), 0.0)
    a = _sigmoid(a0_ref[...] + _dot(xa, a2_ref[...], HIGHEST))
    g = _dot(_sigmoid(xg), g2_ref[...], HIGHEST)

    kk = k * kk_ref[...]
    kk_norm = jnp.sqrt(_dot_exact_rhs(kk * kk, seg))
    kk = jnp.where(valid, kk / jnp.maximum(kk_norm, 1e-12), 0.0)
    k = jnp.where(valid, k * (1.0 + (a - 1.0) * ka_ref[...]), 0.0)
    v = jnp.where(valid, v, 0.0)

    cl = _dot_exact_lhs(tri_ref[...], lw)
    dec_in = jnp.exp(cl)
    dec_out = jnp.exp(-cl)
    a_t = -kk * jnp.exp(cl - lw)
    b_t = kk * a * dec_out
    k_t = k * dec_out
    r_t = r * dec_in
    gamma = jnp.exp(cl[CHUNK - 1:CHUNK, :])

    ti = lax.broadcasted_iota(I32, (CHUNK, 2 * CHUNK), 0)
    si = lax.broadcasted_iota(I32, (CHUNK, 2 * CHUNK), 1)
    si = jnp.where(si >= CHUNK, si - CHUNK, si)
    strict = si < ti
    incl = si <= ti
    eye = (lax.broadcasted_iota(I32, (CHUNK, CHUNK), 0) == lax.broadcasted_iota(I32, (CHUNK, CHUNK), 1)).astype(F32)
    zeros = jnp.zeros((CHUNK, HEAD), F32)

    heads = range(N_HEADS)
    sls = [slice(h * HEAD, (h + 1) * HEAD) for h in heads]
    vh = [v[:, sl] for sl in sls]
    ar = [_split2(jnp.concatenate([a_t[:, sl], r_t[:, sl]], axis=0)) for sl in sls]
    bk = [_split2(jnp.concatenate([b_t[:, sl], k_t[:, sl]], axis=0)) for sl in sls]
    s0 = [state_sc[h] for h in heads]
    gram = [_mm3(_dot_nt, ar[h], bk[h]) for h in heads]
    top = [jnp.where(strict, gram[h][:CHUNK], 0.0) for h in heads]
    bot = [jnp.where(incl, gram[h][CHUNK:], 0.0) for h in heads]
    hs = [_mm3(_dot_nt, ar[h], _split2(s0[h])) for h in heads]
    rhs = [hs[h][:CHUNK] + _mm3(_dot, _split2(top[h]), _split2(jnp.concatenate([zeros, vh[h]], axis=0)))
           for h in heads]
    inv = [eye + top[h][:, :CHUNK] for h in heads]
    pws = [_split2(top[h][:, :CHUNK]) for h in heads]
    for _ in range(5):
        pws = [_split2(_mm3(_dot, pws[h], pws[h])) for h in heads]
        inv = [inv[h] + _mm3(_dot, _split2(inv[h]), pws[h]) for h in heads]
    u = [_mm3(_dot, _split2(inv[h]), _split2(rhs[h])) for h in heads]
    uv = [_split2(jnp.concatenate([u[h], vh[h]], axis=0)) for h in heads]
    ys = [hs[h][CHUNK:] + _dot(bot[h].astype(BF16), uv[h][0]) for h in heads]
    for h in heads:
        state_sc[h] = (s0[h] + _mm3(_dot_tn, uv[h], bk[h])) * gamma[:, sls[h]]

    y = jnp.concatenate(ys, axis=1)
    yc = y - _dot_exact_rhs(y, seg) * (1.0 / HEAD)
    var = _dot_exact_rhs(yc * yc, seg) * (1.0 / HEAD)
    y = yc * lax.rsqrt(var + LNX_EPS) * lng_ref[...] + lnb_ref[...]
    r = xm[:, :D_HEADS]
    bonus = _dot_exact_rhs(r * k * rk_ref[...], seg) * v
    y_ref[0] = (y + bonus) * g

    @pl.when(c == n_chunks - 1)
    def _():
        sout_ref[0] = state_sc[...]


def _rwkv(pr, shift0, wkv0, p):
    bsz, seq_len, _ = pr.shape
    n_chunks = pl.cdiv(seq_len, CHUNK)
    lane = jnp.arange(D_HEADS) // HEAD
    seg = (lane[:, None] == lane[None, :]).astype(BF16)
    tri = (jnp.arange(CHUNK)[None, :] <= jnp.arange(CHUNK)[:, None]).astype(BF16)
    vec = lambda a: a.reshape(1, -1)
    consts = [vec(p["rwkv_mix"]), vec(p["rwkv_w0"]), p["rwkv_w2"], vec(p["rwkv_a0"]), p["rwkv_a2"], p["rwkv_g2"],
              vec(p["rwkv_k_k"]), vec(p["rwkv_k_a"]), vec(p["rwkv_r_k"]), vec(p["rwkv_lnx_g"]),
              vec(p["rwkv_lnx_b"]), seg, tri]
    return pl.pallas_call(
        functools.partial(_rwkv_body, seq_len, n_chunks),
        grid=(bsz, n_chunks),
        in_specs=[pl.BlockSpec((1, CHUNK, RWKV_IN), lambda b, c: (b, c, 0)),
                  pl.BlockSpec((1, 1, RWKV_IN), lambda b, c: (b, 0, 0)),
                  pl.BlockSpec((1, N_HEADS, HEAD, HEAD), lambda b, c: (b, 0, 0, 0))]
                 + [_full(a.shape) for a in consts],
        out_specs=[pl.BlockSpec((1, CHUNK, D_HEADS), lambda b, c: (b, c, 0)),
                   pl.BlockSpec((1, N_HEADS, HEAD, HEAD), lambda b, c: (b, 0, 0, 0))],
        out_shape=[jax.ShapeDtypeStruct((bsz, seq_len, D_HEADS), F32),
                   jax.ShapeDtypeStruct((bsz, N_HEADS, HEAD, HEAD), F32)],
        scratch_shapes=[pltpu.VMEM((1, RWKV_IN), F32), pltpu.VMEM((N_HEADS, HEAD, HEAD), F32)],
        compiler_params=_params(("parallel", "arbitrary")),
        name="rwkv",
    )(pr, shift0, wkv0, *consts)


def _cumsum_body(seq_len, lf_ref, col_ref, row_ref):
    carry = jnp.zeros((1, N_HEADS), F32)
    eye = (lax.broadcasted_iota(I32, (N_HEADS, N_HEADS), 0)
           == lax.broadcasted_iota(I32, (N_HEADS, N_HEADS), 1)).astype(BF16)
    for start in range(0, seq_len, LANES):
        n = min(LANES, seq_len - start)
        tri = (lax.broadcasted_iota(I32, (n, n), 1) <= lax.broadcasted_iota(I32, (n, n), 0)).astype(BF16)
        cum = _dot_exact_lhs(tri, lf_ref[0, start:start + n, :]) + carry
        col_ref[0, start:start + n, :] = cum
        c1, c2, c3 = _split3(cum)
        row_ref[0, :, start:start + n] = _dot_nt(eye, c1) + _dot_nt(eye, c2) + _dot_nt(eye, c3)
        carry = cum[n - 1:n, :]


def _cumsum(lf):
    bsz, seq_len, _ = lf.shape
    return pl.pallas_call(
        functools.partial(_cumsum_body, seq_len),
        grid=(bsz,),
        in_specs=[pl.BlockSpec((1, seq_len, N_HEADS), lambda b: (b, 0, 0))],
        out_specs=[pl.BlockSpec((1, seq_len, N_HEADS), lambda b: (b, 0, 0)),
                   pl.BlockSpec((1, N_HEADS, seq_len), lambda b: (b, 0, 0))],
        out_shape=[jax.ShapeDtypeStruct((bsz, seq_len, N_HEADS), F32),
                   jax.ShapeDtypeStruct((bsz, N_HEADS, seq_len), F32)],
        compiler_params=_params(("parallel",)),
        name="cumsum",
    )(lf)


def _fox_body(kv_len, q_off, tq, tk, n_kv, n_past, q_ref, k_ref, v_ref, *rest):
    if n_past is None:
        cq_ref, ck_ref, g_ref, o_ref, m_sc, l_sc, acc_sc = rest
    else:
        k_new_ref, v_new_ref, cq_ref, ck_ref, g_ref, o_ref, m_sc, l_sc, acc_sc = rest
    qi = pl.program_id(1)
    ki = pl.program_id(2)

    @pl.when(ki == 0)
    def _():
        m_sc[...] = jnp.full(m_sc.shape, -jnp.inf, F32)
        l_sc[...] = jnp.zeros(l_sc.shape, F32)
        acc_sc[...] = jnp.zeros(acc_sc.shape, F32)

    q_start = q_off + qi * tq

    def step(k_blk, v_blk, k_start, n_rows):
        rows = q_start + lax.broadcasted_iota(I32, (tq, 1), 0)
        cols = k_start + lax.broadcasted_iota(I32, (1, n_rows), 1)
        mask = cols <= rows
        k_valid = (k_start + lax.broadcasted_iota(I32, (n_rows, 1), 0)) < kv_len
        q = q_ref[0] * (HEAD ** -0.5)
        kb = k_blk.astype(BF16)
        vb = jnp.where(k_valid, v_blk, 0.0).astype(BF16)
        cq = cq_ref[0]
        ck = ck_ref[0][:, :n_rows]
        ones = jnp.ones((n_rows, LANES), BF16)
        heads = range(N_HEADS)
        sls = [slice(h * HEAD, (h + 1) * HEAD) for h in heads]
        qb = q.astype(BF16)
        s = [_dot_nt(qb[:, sl], kb[:, sl]) for sl in sls]
        s = [jnp.where(mask, s[h] + (cq[:, h:h + 1] - ck[h:h + 1, :]), -jnp.inf) for h in heads]
        m_prev = [m_sc[h] for h in heads]
        m_new = [jnp.maximum(m_prev[h], jnp.max(s[h], axis=-1, keepdims=True)) for h in heads]
        scale = [jnp.exp(m_prev[h] - m_new[h]) for h in heads]
        pexp = [jnp.exp(s[h] - m_new[h]).astype(BF16) for h in heads]
        psum = [_dot(pexp[h], ones)[:, :1] for h in heads]
        pv = [_dot(pexp[h], vb[:, sls[h]]) for h in heads]
        for h in heads:
            l_sc[h] = scale[h] * l_sc[h] + psum[h]
            acc_sc[:, sls[h]] = scale[h] * acc_sc[:, sls[h]] + pv[h]
            m_sc[h] = m_new[h]

    if n_past is None:
        @pl.when(ki * tk <= q_start + tq - 1)
        def _():
            step(k_ref[0], v_ref[0], ki * tk, tk)
    else:
        @pl.when(ki < n_past)
        def _():
            step(k_ref[0], v_ref[0], ki * tk, tk)

        @pl.when(ki == n_past)
        def _():
            step(k_new_ref[0], v_new_ref[0], q_off, tq)

    @pl.when(ki == n_kv - 1)
    def _():
        for h in range(N_HEADS):
            sl = slice(h * HEAD, (h + 1) * HEAD)
            o = acc_sc[:, sl] / l_sc[h]
            o = o * lax.rsqrt(jnp.mean(o * o, axis=-1, keepdims=True) + 1e-6)
            acc_sc[:, sl] = o
        o_ref[0] = acc_sc[...] * g_ref[...]


def _fox(q, k, v, c_col, c_row, norm_g, tq, tk, past=None):
    bsz, q_len, _ = q.shape
    n_q = pl.cdiv(q_len, tq)
    if past is None:
        q_off, n_past, kv_len = 0, None, k.shape[1]
        n_kv = pl.cdiv(kv_len, tk)
        kv_operands, extra_specs = (k, v), []
    else:
        q_off = past[0].shape[1]
        assert q_off % tk == 0 and q_off % tq == 0 and q_len == tq == k.shape[1]
        n_past, kv_len = q_off // tk, q_off + q_len
        n_kv = n_past + 1
        kv_operands = past + (k, v)
        extra_specs = [pl.BlockSpec((1, tq, D_HEADS), lambda b, qi, ki: (b, 0, 0))] * 2
    last_kv = lambda qi: (q_off + (qi + 1) * tq - 1) // tk if n_past is None else n_past - 1
    kv_map = lambda b, qi, ki: (b, jnp.minimum(ki, last_kv(qi)), 0)
    ck_last = lambda qi: last_kv(qi) if n_past is None else n_past
    return pl.pallas_call(
        functools.partial(_fox_body, kv_len, q_off, tq, tk, n_kv, n_past),
        grid=(bsz, n_q, n_kv),
        in_specs=[pl.BlockSpec((1, tq, D_HEADS), lambda b, qi, ki: (b, qi, 0)),
                  pl.BlockSpec((1, tk, D_HEADS), kv_map),
                  pl.BlockSpec((1, tk, D_HEADS), kv_map)]
                 + extra_specs
                 + [pl.BlockSpec((1, tq, N_HEADS), lambda b, qi, ki: (b, q_off // tq + qi, 0)),
                    pl.BlockSpec((1, N_HEADS, tk), lambda b, qi, ki: (b, 0, jnp.minimum(ki, ck_last(qi)))),
                    _full((1, D_HEADS))],
        out_specs=pl.BlockSpec((1, tq, D_HEADS), lambda b, qi, ki: (b, qi, 0)),
        out_shape=jax.ShapeDtypeStruct((bsz, q_len, D_HEADS), F32),
        scratch_shapes=[pltpu.VMEM((N_HEADS, tq, 1), F32), pltpu.VMEM((N_HEADS, tq, 1), F32),
                        pltpu.VMEM((tq, D_HEADS), F32)],
        compiler_params=_params(("parallel", "parallel", "arbitrary")),
        name="fox",
    )(q, *kv_operands, c_col, c_row, norm_g)


def _out_proj_body(yr_ref, yf_ref, h_ref, wr_ref, wf_ref, g_ref, b_ref, o_ref):
    mix = _dot(yr_ref[...].astype(BF16), wr_ref[...]) + _dot(yf_ref[...].astype(BF16), wf_ref[...])
    o_ref[...] = _layernorm(ALPHA * h_ref[...] + mix, g_ref[...], b_ref[...])


def _out_proj(yr, yf, h, wr, wf, g, b):
    n = h.shape[0]
    tb = TOKEN_BLOCK
    rows = lambda w: pl.BlockSpec((tb, w), lambda i: (i, 0))
    return pl.pallas_call(
        _out_proj_body,
        grid=(pl.cdiv(n, tb),),
        in_specs=[rows(D_HEADS), rows(D_HEADS), rows(D_MODEL), _full(wr.shape), _full(wf.shape),
                  _full((1, D_MODEL)), _full((1, D_MODEL))],
        out_specs=rows(D_MODEL),
        out_shape=jax.ShapeDtypeStruct((n, D_MODEL), F32),
        compiler_params=_params(("parallel",)),
        name="out_proj",
    )(yr, yf, h, wr, wf, g, b)


_BIG = 2 ** 30
_CAND_COUNTS = tuple(PEER_TOPK // (i + 1) for i in range(PEER_TOPK))
_N_CAND = sum(_CAND_COUNTS)
_CAND_ROWS = -(-_N_CAND // 8) * 8


def _top_rows(s, order, k, payload=None):
    vals, outs = [], []
    for _ in range(k):
        m = jnp.max(s, axis=0, keepdims=True)
        first = jnp.min(jnp.where(s == m, order, _BIG), axis=0, keepdims=True)
        sel = order == first
        vals.append(m)
        outs.append(first if payload is None else jnp.max(jnp.where(sel, payload, -1), axis=0, keepdims=True))
        s = jnp.where(sel, -jnp.inf, s)
    return jnp.concatenate(vals, axis=0), jnp.concatenate(outs, axis=0)


def _topk_body(h_ref, wq_ref, k1_ref, k2_ref, pos_ref, idx_ref, gate_ref, cand_s_sc, cand_i_sc):
    tb = h_ref.shape[0]
    q = _dot(h_ref[...].astype(BF16), wq_ref[...]).astype(BF16)
    key_id = lax.broadcasted_iota(I32, (N_KEYS, LANES), 0)
    cand_pos = pos_ref[...]
    k1 = k1_ref[...].astype(BF16)
    k2 = k2_ref[...].astype(BF16)
    cand_s_sc[_N_CAND:, :] = jnp.full((_CAND_ROWS - _N_CAND, LANES), -jnp.inf, F32)
    cand_i_sc[_N_CAND:, :] = jnp.zeros((_CAND_ROWS - _N_CAND, LANES), I32)
    for part in range(tb // LANES):
        qp = q[part * LANES:(part + 1) * LANES]
        ids, gates = [], []
        for h in range(PEER_HEADS):
            o = h * 2 * D_KEY_HALF
            s1 = _dot_nt(k1, qp[:, o:o + D_KEY_HALF])
            s2 = _dot_nt(k2, qp[:, o + D_KEY_HALF:o + 2 * D_KEY_HALF])
            v1, i1 = _top_rows(s1, key_id, PEER_TOPK)
            v2, i2 = _top_rows(s2, key_id, PEER_TOPK)
            off = 0
            for i, cnt in enumerate(_CAND_COUNTS):
                cand_s_sc[off:off + cnt, :] = v1[i:i + 1] + v2[:cnt]
                cand_i_sc[off:off + cnt, :] = i1[i:i + 1] * N_KEYS + i2[:cnt]
                off += cnt
            best_s, best_i = _top_rows(cand_s_sc[...], cand_pos, PEER_TOPK, payload=cand_i_sc[...])
            e = jnp.exp(best_s - best_s[0:1])
            ids.append(best_i)
            gates.append(e / jnp.sum(e, axis=0, keepdims=True))
        rows = slice(part * LANES, (part + 1) * LANES)
        idx_ref[rows, :] = jnp.concatenate(ids, axis=0).T
        gate_ref[rows, :] = jnp.concatenate(gates, axis=0).T


def _topk(h1, wq, keys1, keys2):
    n = h1.shape[0]
    tb = TOKEN_BLOCK
    pos = [i * PEER_TOPK + j for i, cnt in enumerate(_CAND_COUNTS) for j in range(cnt)]
    pos = jnp.asarray(pos + [_BIG] * (_CAND_ROWS - _N_CAND), I32)
    pos = jnp.broadcast_to(pos[:, None], (_CAND_ROWS, LANES))
    return pl.pallas_call(
        _topk_body,
        grid=(pl.cdiv(n, tb),),
        in_specs=[pl.BlockSpec((tb, D_MODEL), lambda i: (i, 0)), _full(wq.shape), _full(keys1.shape),
                  _full(keys2.shape), _full(pos.shape)],
        out_specs=[pl.BlockSpec((tb, PEER_SLOTS), lambda i: (i, 0)), pl.BlockSpec((tb, PEER_SLOTS), lambda i: (i, 0))],
        out_shape=[jax.ShapeDtypeStruct((n, PEER_SLOTS), I32), jax.ShapeDtypeStruct((n, PEER_SLOTS), F32)],
        scratch_shapes=[pltpu.VMEM((_CAND_ROWS, LANES), F32), pltpu.VMEM((_CAND_ROWS, LANES), I32)],
        compiler_params=_params(("parallel",)),
        name="peer_topk",
    )(h1, wq, keys1, keys2, pos)


def _unpack_pair(w):
    return lax.bitcast_convert_type(w << 16, F32), lax.bitcast_convert_type(w & jnp.int32(-65536), F32)


_ERF_NUM = (-2.72614225801306e-10, 2.77068142495902e-08, -2.10102402082508e-06, -5.69250639462346e-05,
            -7.34990630326855e-04, -2.95459980854025e-03, -1.60960333262415e-02)
_ERF_DEN = (-1.45660718464996e-05, -2.13374055278905e-04, -1.68282697438203e-03, -7.37332916720468e-03,
            -1.42647390514189e-02)


def _erf_f32(x):
    x = jnp.minimum(jnp.maximum(x, -4.0), 4.0)
    x2 = x * x
    num = x2 * _ERF_NUM[0] + _ERF_NUM[1]
    for c in _ERF_NUM[2:]:
        num = num * x2 + c
    den = x2 * _ERF_DEN[0] + _ERF_DEN[1]
    for c in _ERF_DEN[2:]:
        den = den * x2 + c
    return x * num / den


def _expert_mix(u_tab, v_tab, idx, x, gate):
    n_tok, words = x.shape[0], u_tab.shape[1]
    lanes = SC_LANES
    pass_chunks = words // lanes // 2
    n_workers = SC_CORES * SC_SUBCORES
    assert n_tok % SC_UNIT == 0
    base, extra = divmod(n_tok // SC_UNIT, n_workers)
    idx_rows = 2 * SC_UNIT
    mesh = plsc.VectorSubcoreMesh(core_axis_name="c", subcore_axis_name="s",
                                  num_cores=SC_CORES, num_subcores=SC_SUBCORES)

    row_bytes = words * 4
    cost = pl.CostEstimate(
        flops=2 * 2 * n_tok * PEER_SLOTS * 2 * words, transcendentals=0,
        bytes_accessed=n_tok * (2 * PEER_SLOTS * row_bytes + 2 * (2 * words + PEER_SLOTS) * 4 + PEER_SLOTS * 4))

    @functools.partial(
        pl.kernel, mesh=mesh, compiler_params=pltpu.CompilerParams(needs_layout_passes=False),
        cost_estimate=cost, name="expert_mix",
        out_type=jax.ShapeDtypeStruct((n_tok, 2 * words), F32),
        scratch_types=[pltpu.VMEM((2 * idx_rows, SC_HALF), I32),
                       pltpu.VMEM((SC_HALF, words), I32), pltpu.VMEM((SC_HALF, words), I32),
                       pltpu.VMEM((2 * SC_UNIT, 2 * words), F32),
                       pltpu.VMEM((2 * SC_UNIT, PEER_SLOTS), F32),
                       pltpu.VMEM((SC_UNIT, PEER_SLOTS), F32),
                       pltpu.VMEM((2 * SC_UNIT, 2 * words), F32),
                       pltpu.SemaphoreType.DMA, pltpu.SemaphoreType.DMA, pltpu.SemaphoreType.DMA,
                       pltpu.SemaphoreType.DMA((2,))],
    )
    def sc_kernel(u_hbm, v_hbm, idx_hbm, x_hbm, gate_hbm, out_hbm, idx_v, buf_a, buf_b, x_v, gate_v, act_v, out_v,
                  sem_a, sem_b, sem_in, sem_out):
        worker = lax.axis_index("s") * SC_CORES + lax.axis_index("c")
        first_unit = worker * base + jnp.minimum(worker, extra)
        my_units = base + (worker < extra).astype(I32)
        lane = lax.iota(I32, lanes)

        def first_token(ui):
            return pl.multiple_of((first_unit + ui) * SC_UNIT, SC_UNIT)

        def stage_in(ui, slot):
            tok0 = first_token(ui)
            toks = pl.ds(tok0, SC_UNIT)
            dst = pl.ds(slot * SC_UNIT, SC_UNIT)
            rows = pl.ds(pl.multiple_of(tok0 * 2, idx_rows), idx_rows)
            return (pltpu.make_async_copy(idx_hbm.at[rows], idx_v.at[pl.ds(slot * idx_rows, idx_rows)], sem_in),
                    pltpu.make_async_copy(x_hbm.at[toks], x_v.at[dst], sem_in),
                    pltpu.make_async_copy(gate_hbm.at[toks], gate_v.at[dst], sem_in))

        def write_out(ui, slot):
            return pltpu.make_async_copy(out_v.at[pl.ds(slot * SC_UNIT, SC_UNIT)],
                                         out_hbm.at[pl.ds(first_token(ui), SC_UNIT)], sem_out.at[slot])

        def gather(table_hbm, row, buf, sem):
            return pltpu.make_async_copy(table_hbm.at[idx_v.at[row]], buf, sem)

        def dot(buf, t, xrow, half):
            @pl.loop(0, SC_HALF // lanes)
            def _(g):
                def chunk(j, acc):
                    x_lo = x_v[xrow, pl.ds(j * lanes, lanes)]
                    x_hi = x_v[xrow, pl.ds(words + j * lanes, lanes)]
                    out = []
                    for r in range(lanes):
                        lo, hi = _unpack_pair(buf[g * lanes + r, pl.ds(j * lanes, lanes)])
                        out.append(acc[r] + lo * x_lo + hi * x_hi)
                    return tuple(out)

                acc = lax.fori_loop(0, words // lanes, chunk, tuple(jnp.zeros((lanes,), F32) for _ in range(lanes)))
                res = jnp.zeros((lanes,), F32)
                for r in range(lanes):
                    res = jnp.where(lane == r, jnp.sum(acc[r]), res)
                act_v[t, pl.ds(half * SC_HALF + g * lanes, lanes)] = res

        def gate_gelu(t, xrow):
            for g in range(PEER_SLOTS // lanes):
                cols = pl.ds(g * lanes, lanes)
                a = act_v[t, cols]
                act_v[t, cols] = gate_v[xrow, cols] * (a * 0.5 * (1.0 + _erf_f32(a * (2.0 ** -0.5))))

        def vsum(buf, t, orow, half):
            for p in range(2):
                cols = [(p * pass_chunks + jj) * lanes for jj in range(pass_chunks)]
                if half == 0:
                    init = tuple(jnp.zeros((lanes,), F32) for _ in range(2 * pass_chunks))
                else:
                    init = (tuple(out_v[orow, pl.ds(c, lanes)] for c in cols)
                            + tuple(out_v[orow, pl.ds(words + c, lanes)] for c in cols))

                def row(r, acc):
                    slot = jnp.full((lanes,), half * SC_HALF, I32) + r
                    c = plsc.load_gather(act_v, [jnp.full((lanes,), 0, I32) + t, slot])
                    lo_acc, hi_acc = list(acc[:pass_chunks]), list(acc[pass_chunks:])
                    for jj in range(pass_chunks):
                        lo, hi = _unpack_pair(buf[r, pl.ds(cols[jj], lanes)])
                        lo_acc[jj] = lo_acc[jj] + lo * c
                        hi_acc[jj] = hi_acc[jj] + hi * c
                    return tuple(lo_acc) + tuple(hi_acc)

                acc = lax.fori_loop(0, SC_HALF, row, init)
                for jj in range(pass_chunks):
                    out_v[orow, pl.ds(cols[jj], lanes)] = acc[jj]
                    out_v[orow, pl.ds(words + cols[jj], lanes)] = acc[pass_chunks + jj]

        def unit_body(ui, carry):
            slot = lax.rem(ui, 2)
            has_next = ui + 1 < my_units

            @pl.when(has_next)
            def _():
                for c in stage_in(ui + 1, 1 - slot):
                    c.start()

            @pl.when(ui >= 2)
            def _():
                write_out(ui - 2, slot).wait()

            @pl.loop(0, SC_UNIT)
            def _(t):
                r0 = slot * idx_rows + 2 * t
                urow = slot * SC_UNIT + t
                gather(u_hbm, r0 + 1, buf_b, sem_b).start()
                gather(u_hbm, r0, buf_a, sem_a).wait()
                dot(buf_a, t, urow, 0)
                gather(v_hbm, r0, buf_a, sem_a).start()
                gather(u_hbm, r0 + 1, buf_b, sem_b).wait()
                dot(buf_b, t, urow, 1)
                gate_gelu(t, urow)
                gather(v_hbm, r0 + 1, buf_b, sem_b).start()
                gather(v_hbm, r0, buf_a, sem_a).wait()
                vsum(buf_a, t, urow, 0)

                @pl.when(t + 1 < SC_UNIT)
                def _():
                    gather(u_hbm, r0 + 2, buf_a, sem_a).start()

                @pl.when((t + 1 == SC_UNIT) & has_next)
                def _():
                    for c in stage_in(ui + 1, 1 - slot):
                        c.wait()
                    gather(u_hbm, (1 - slot) * idx_rows, buf_a, sem_a).start()

                gather(v_hbm, r0 + 1, buf_b, sem_b).wait()
                vsum(buf_b, t, urow, 1)

            write_out(ui, slot).start()
            return carry

        @pl.when(my_units > 0)
        def _():
            for c in stage_in(0, 0):
                c.start()
            for c in stage_in(0, 0):
                c.wait()
            gather(u_hbm, 0, buf_a, sem_a).start()
            lax.fori_loop(0, my_units, unit_body, 0)

            @pl.when(my_units >= 2)
            def _():
                write_out(my_units - 2, lax.rem(my_units, 2)).wait()

            write_out(my_units - 1, lax.rem(my_units - 1, 2)).wait()

    return sc_kernel(u_tab, v_tab, idx.reshape(n_tok * 2, SC_HALF), x, gate)


def _tc_mix_body(n_steps, idx_ref, idx_next_ref, x_ref, gate_ref, g_ref, b_ref, u_ref, v_ref, o_ref, ubuf, vbuf, sems):
    i = pl.program_id(0)
    tg = TC_MIX_TOKENS
    n_rows = tg * PEER_SLOTS

    def issue(idx, slot):
        def step(j, carry):
            e = idx[j // PEER_SLOTS, j % PEER_SLOTS]
            pltpu.make_async_copy(u_ref.at[pl.ds(e, 1), :], ubuf.at[slot, pl.ds(j, 1), :], sems.at[slot]).start()
            pltpu.make_async_copy(v_ref.at[pl.ds(e, 1), :], vbuf.at[slot, pl.ds(j, 1), :], sems.at[slot]).start()
            return carry
        lax.fori_loop(0, n_rows, step, 0)

    slot = i % 2

    @pl.when(i == 0)
    def _():
        issue(idx_ref, 0)

    @pl.when(i + 1 < n_steps)
    def _():
        issue(idx_next_ref, 1 - slot)

    pltpu.make_async_copy(u_ref.at[pl.ds(0, n_rows), :], ubuf.at[slot], sems.at[slot]).wait()
    pltpu.make_async_copy(v_ref.at[pl.ds(0, n_rows), :], vbuf.at[slot], sems.at[slot]).wait()

    x = x_ref[...]
    res = _dot_nt(x.astype(BF16), ubuf[slot].astype(BF16))
    tok = lax.broadcasted_iota(I32, (tg, PEER_SLOTS), 0)
    act = jnp.zeros((tg, PEER_SLOTS), F32)
    for t in range(tg):
        act = act + jnp.where(tok == t, res[:, t * PEER_SLOTS:(t + 1) * PEER_SLOTS], 0.0)
    coef = gate_ref[...] * (act * 0.5 * (1.0 + lax.erf(act * (2.0 ** -0.5))))
    col_tok = lax.broadcasted_iota(I32, (tg, n_rows), 1) // PEER_SLOTS
    row_tok = lax.broadcasted_iota(I32, (tg, n_rows), 0)
    wide = jnp.where(col_tok == row_tok, jnp.concatenate([coef] * tg, axis=1), 0.0)
    out = _dot(wide.astype(BF16), vbuf[slot].astype(BF16))
    o_ref[...] = _layernorm(ALPHA * x + out, g_ref[...], b_ref[...])


def _tc_mix(idx, gate, h1, u, v, g, b):
    n = h1.shape[0]
    tg = TC_MIX_TOKENS
    n_steps = n // tg
    smem_rows = lambda shift: pl.BlockSpec((tg, PEER_SLOTS), lambda i: (jnp.minimum(i + shift, n_steps - 1), 0),
                                           memory_space=pltpu.SMEM)
    return pl.pallas_call(
        functools.partial(_tc_mix_body, n_steps),
        grid=(n_steps,),
        in_specs=[smem_rows(0), smem_rows(1),
                  pl.BlockSpec((tg, D_MODEL), lambda i: (i, 0)),
                  pl.BlockSpec((tg, PEER_SLOTS), lambda i: (i, 0)),
                  _full((1, D_MODEL)), _full((1, D_MODEL)),
                  pl.BlockSpec(memory_space=pl.ANY), pl.BlockSpec(memory_space=pl.ANY)],
        out_specs=pl.BlockSpec((tg, D_MODEL), lambda i: (i, 0)),
        out_shape=jax.ShapeDtypeStruct((n, D_MODEL), F32),
        scratch_shapes=[pltpu.VMEM((2, tg * PEER_SLOTS, D_MODEL), F32), pltpu.VMEM((2, tg * PEER_SLOTS, D_MODEL), F32),
                        pltpu.SemaphoreType.DMA((2,))],
        compiler_params=_params(("arbitrary",)),
        name="tc_expert_mix",
    )(idx, idx, h1, gate, g, b, u, v)


def _final_body(x_ref, mix_ref, g_ref, b_ref, o_ref):
    o_ref[...] = _layernorm(ALPHA * x_ref[...] + mix_ref[...], g_ref[...], b_ref[...])


def _rowwise(body, name, n, widths_in, width_out, *args):
    tb = TOKEN_BLOCK
    rows = lambda w: pl.BlockSpec((tb, w), lambda i: (i, 0))
    n_rows_args = len(widths_in)
    return pl.pallas_call(
        body,
        grid=(pl.cdiv(n, tb),),
        in_specs=[rows(w) for w in widths_in] + [_full(a.shape) for a in args[n_rows_args:]],
        out_specs=rows(width_out),
        out_shape=jax.ShapeDtypeStruct((n, width_out), F32),
        compiler_params=_params(("parallel",)),
        name=name,
    )(*args)


def _group(x, shift0, wkv0, past, p):
    bsz, seq_len, _ = x.shape
    n = bsz * seq_len
    h, pr, q, k, v, lf = _in_proj(x.reshape(n, D_MODEL), p["ln_in_g"], p["ln_in_b"], p["w_r"], p["w_qkv"],
                                  p["w_f"], p["b_f"])
    pr = pr.reshape(bsz, seq_len, RWKV_IN)
    y_r, wkv_new = _rwkv(pr, shift0, wkv0, p)

    seq3 = lambda a: a.reshape(bsz, seq_len, -1)
    k3, v3, lf3 = seq3(k), seq3(v), seq3(lf)
    if past is None:
        kv_past, lf_all, tq = None, lf3, FOX_BLOCK
    else:
        k_past, v_past, lf_past = past
        flat = lambda a: a.reshape(bsz, a.shape[1], D_HEADS)
        kv_past, lf_all, tq = (flat(k_past), flat(v_past)), jnp.concatenate([lf_past, lf3], axis=1), seq_len
    c_col, c_row = _cumsum(lf_all)
    y_f = _fox(seq3(q), k3, v3, c_col, c_row, p["fox_norm_g"], tq, FOX_KV_BLOCK, kv_past)

    h1 = _out_proj(y_r.reshape(n, D_HEADS), y_f.reshape(n, D_HEADS), h, p["w_out_r"], p["w_out_f"],
                   p["ln1_g"], p["ln1_b"])
    idx, gate = _topk(h1, p["peer_w_q"], p["peer_keys1"], p["peer_keys2"])
    if past is not None:
        y = _tc_mix(idx, gate, h1, p["peer_u_f32"], p["peer_v_f32"], p["ln2_g"], p["ln2_b"])
    else:
        mix = _expert_mix(p["peer_u"], p["peer_v"], idx, h1, gate)
        y = _rowwise(_final_body, "final_ln", n, (D_MODEL, D_MODEL), D_MODEL, h1, mix, p["ln2_g"], p["ln2_b"])

    heads = lambda a: a.reshape(1, bsz, seq_len, N_HEADS, HEAD)
    return (y.reshape(bsz, seq_len, D_MODEL), pr[None, :, seq_len - 1:, :], wkv_new[None],
            heads(k), heads(v), lf3[None])


def _pack_halves(a):
    bits = lax.bitcast_convert_type(a.astype(BF16), jnp.uint16).astype(jnp.uint32)
    w = a.shape[1] // 2
    return lax.bitcast_convert_type(bits[:, :w] | (bits[:, w:] << 16), I32)


def kernel(x_prompt, x_sample, state_rwkv_shift, state_rwkv_wkv, cache_fox_k, cache_fox_v, cache_fox_logf,
           meta_tokens, ln_in_g, ln_in_b, w_in, rwkv_mix, rwkv_w0, rwkv_w2, rwkv_a0, rwkv_a2, rwkv_g2,
           rwkv_k_k, rwkv_k_a, rwkv_r_k, rwkv_lnx_g, rwkv_lnx_b, fox_b_f, fox_norm_g, w_out, ln1_g, ln1_b,
           peer_w_q, peer_keys1, peer_keys2, peer_u, peer_v, ln2_g, ln2_b):
    assert w_in.shape[0] == 1, "single-layer step"
    bsz = x_prompt.shape[0]
    row = lambda a: a.reshape(1, -1)
    w = w_in[0]
    fox0 = RWKV_IN
    pad_f = LANES - N_HEADS
    p = dict(
        ln_in_g=row(ln_in_g), ln_in_b=row(ln_in_b),
        w_r=w[:, :RWKV_IN].astype(BF16),
        w_qkv=w[:, fox0:fox0 + 3 * D_HEADS].astype(BF16),
        w_f=jnp.pad(w[:, fox0 + 3 * D_HEADS:], ((0, 0), (0, pad_f))).astype(BF16),
        b_f=jnp.pad(row(fox_b_f[0]), ((0, 0), (0, pad_f))),
        rwkv_mix=rwkv_mix[0], rwkv_w0=rwkv_w0[0], rwkv_w2=rwkv_w2[0], rwkv_a0=rwkv_a0[0], rwkv_a2=rwkv_a2[0],
        rwkv_g2=rwkv_g2[0], rwkv_k_k=rwkv_k_k[0], rwkv_k_a=rwkv_k_a[0], rwkv_r_k=rwkv_r_k[0],
        rwkv_lnx_g=rwkv_lnx_g[0], rwkv_lnx_b=rwkv_lnx_b[0],
        fox_norm_g=row(fox_norm_g[0]),
        w_out_r=w_out[0, :D_HEADS].astype(BF16), w_out_f=w_out[0, D_HEADS:].astype(BF16),
        ln1_g=row(ln1_g[0]), ln1_b=row(ln1_b[0]),
        peer_w_q=peer_w_q[0].astype(BF16), peer_keys1=peer_keys1[0], peer_keys2=peer_keys2[0],
        peer_u=_pack_halves(peer_u[0]), peer_v=_pack_halves(peer_v[0]), peer_u_f32=peer_u[0], peer_v_f32=peer_v[0],
        ln2_g=row(ln2_g[0]), ln2_b=row(ln2_b[0]),
    )
    sizes = PROMPT_SLICES if sum(PROMPT_SLICES) == bsz else (bsz,)
    parts = []
    start = 0
    for pb in sizes:
        meta = jnp.broadcast_to(meta_tokens[None], (pb, N_META, D_MODEL))
        hp0 = jnp.concatenate([meta, x_prompt[start:start + pb]], axis=1)
        yp, *state = _group(hp0, jnp.zeros((pb, 1, RWKV_IN), F32), jnp.zeros((pb, N_HEADS, HEAD, HEAD), F32), None, p)
        parts.append((yp[:, N_META:], *state))
        start += pb
    ys, s_shift, s_wkv, s_k, s_v, s_lf = _group(
        x_sample, state_rwkv_shift[0], state_rwkv_wkv[0],
        (cache_fox_k[0], cache_fox_v[0], cache_fox_logf[0]), p)
    yp = jnp.concatenate([t[0] for t in parts], axis=0)
    p_shift, p_wkv, p_k, p_v, p_lf = (jnp.concatenate([t[j] for t in parts], axis=1) for j in range(1, 6))
    return (yp, ys, p_shift, p_wkv, p_k, p_v, p_lf, s_shift, s_wkv, s_k, s_v, s_lf)
```
